```python
import jax, jax.numpy as jnp
from jax import lax
import numpy as np

D_MODEL = 1024
BATCH = 16
SEQ = 2048
DEPTH = 4

CHUNK = 64
Q_BLOCK = 128
DSA_HEADS = 8
DSA_HEAD_DIM = 64
DSA_WIDTH = DSA_HEADS * DSA_HEAD_DIM
Q_LORA = 256
KV_LORA = 128
IDX_HEADS = 4
IDX_DIM = 64
TOPK_MAX = 256
RWKV_HEADS = 8
RWKV_HEAD_DIM = 64
RWKV_WIDTH = RWKV_HEADS * RWKV_HEAD_DIM
DECAY_LORA = 64
AAA_LORA = 64
MV_LORA = 32
GATE_LORA = 160
MIX_WIDTH = DSA_WIDTH + RWKV_WIDTH
D_FF = ((8 * D_MODEL + 3 * 256 - 1) // (3 * 256)) * 256
NORM_EPS = 1e-6
LNX_EPS = 64e-5
DSA_COLS = [Q_LORA, KV_LORA, IDX_DIM, IDX_HEADS]
RWKV_COLS = [RWKV_WIDTH, RWKV_WIDTH, RWKV_WIDTH, DECAY_LORA, AAA_LORA, GATE_LORA]
N_DSA_COLS = sum(DSA_COLS)
N_RWKV_COLS = sum(RWKV_COLS)

kernel_name = 'hybrid_dsa_rwkv7_sandwich_adaln'


def rms_norm(x, g):
    xf = x.astype(jnp.float32)
    y = xf * lax.rsqrt(jnp.mean(xf * xf, axis=-1, keepdims=True) + NORM_EPS)
    return y.astype(x.dtype) * g


def layer_norm(x, g, b):
    xf = x.astype(jnp.float32)
    mu = jnp.mean(xf, axis=-1, keepdims=True)
    var = jnp.mean(jnp.square(xf - mu), axis=-1, keepdims=True)
    return ((xf - mu) * lax.rsqrt(var + NORM_EPS)).astype(x.dtype) * g + b


def split_cols(a, sizes):
    return jnp.split(a, np.cumsum(sizes)[:-1].tolist(), axis=-1)


def token_shift(u, mu):
    prev = jnp.pad(u, ((0, 0), (1, 0), (0, 0)))[:, :-1]
    return u + (prev - u) * mu


def dsa_mixer(c_q, c_kv, k_idx_raw, w_idx, q_norm_g, kv_norm_g, w_q_up, w_qi_up,
              w_k_up, w_v_up, kidx_ln_g, kidx_ln_b):
    B, S, _ = c_q.shape
    top_k = min(TOPK_MAX, S // 4)
    nblk = S // Q_BLOCK
    cq = rms_norm(c_q, q_norm_g)
    ckv = rms_norm(c_kv, kv_norm_g)
    q = jnp.einsum('bsl,lhd->bshd', cq, w_q_up)
    q_abs = jnp.einsum('bshd,rhd->bshr', q, w_k_up) * (DSA_HEAD_DIM ** -0.5)
    q_idx = jnp.einsum('bsl,lhd->bshd', cq, w_qi_up)
    k_idx = layer_norm(k_idx_raw, kidx_ln_g, kidx_ln_b)
    w_head = w_idx * ((IDX_HEADS ** -0.5) * (IDX_DIM ** -0.5))
    key_chunk = jnp.arange(S) // CHUNK

    def to_blocks(a):
        return a.reshape((B, nblk, Q_BLOCK) + a.shape[2:]).swapaxes(0, 1)

    def block_fn(args):
        qa, qi, wi, start = args
        q_chunk = (start + jnp.arange(Q_BLOCK)) // CHUNK
        admissible = key_chunk[None, :] <= q_chunk[:, None]
        logits = jnp.einsum('bthd,bsd->bths', qi, k_idx)
        score = jnp.einsum('bths,bth->bts', jax.nn.relu(logits), wi).astype(jnp.float32)
        score = jnp.where(admissible[None], score, -jnp.inf)
        top_val, top_idx = lax.top_k(score, top_k)
        valid = jnp.isfinite(top_val)
        kv_sel = jax.vmap(lambda lat, idx: lat[idx])(ckv, top_idx)
        att = jnp.einsum('bthr,btkr->bthk', qa, kv_sel).astype(jnp.float32)
        att = jnp.where(valid[:, :, None, :], att, -jnp.inf)
        p = jax.nn.softmax(att, axis=-1).astype(kv_sel.dtype)
        return jnp.einsum('bthk,btkr->bthr', p, kv_sel)

    starts = jnp.arange(nblk, dtype=jnp.int32) * Q_BLOCK
    o_lat = lax.map(block_fn, (to_blocks(q_abs), to_blocks(q_idx), to_blocks(w_head), starts))
    o_lat = o_lat.swapaxes(0, 1).reshape(B, S, DSA_HEADS, KV_LORA)
    o = jnp.einsum('bshr,rhd->bshd', o_lat, w_v_up)
    return o.reshape(B, S, DSA_WIDTH)


def rwkv7_mixer(r, k, v, w_lo, a_lo, g_lo, w0, w2, a0, a2, g2, k_k, k_a, r_k, lnx_g, lnx_b):
    B, S, _ = r.shape
    H, N = RWKV_HEADS, RWKV_HEAD_DIM
    f32 = jnp.float32

    def heads(t):
        return t.reshape(B, S, H, N)

    w_log = -jax.nn.softplus(-(w0 + jnp.tanh(w_lo) @ w2)) - 0.5
    decay = jnp.exp(-jnp.exp(w_log.astype(f32)))
    a = jax.nn.sigmoid(a0 + a_lo @ a2)
    g = jax.nn.sigmoid(g_lo) @ g2
    kk = heads(k * k_k).astype(f32)
    kk = kk / jnp.maximum(jnp.linalg.norm(kk, axis=-1, keepdims=True), 1e-12)
    k = k * (1 + (a - 1) * k_a)
    r_h, k_h, v_h, a_h = heads(r), heads(k), heads(v), heads(a)
    xs = [jnp.moveaxis(t.astype(f32), 1, 0) for t in (r_h, heads(decay), k_h, v_h, kk, a_h)]

    def step(state, inp):
        r_t, w_t, k_t, v_t, kk_t, a_t = inp
        sa = jnp.einsum('bhvk,bhk->bhv', state, kk_t)
        state = (state * w_t[:, :, None, :]
                 - sa[..., None] * (kk_t * a_t)[:, :, None, :]
                 + v_t[..., None] * k_t[:, :, None, :])
        return state, jnp.einsum('bhvk,bhk->bhv', state, r_t)

    state0 = jnp.zeros((B, H, N, N), f32)
    _, y = lax.scan(step, state0, (xs[0], xs[1], xs[2], xs[3], xs[4], xs[5]))
    y = jnp.moveaxis(y, 0, 1)
    mu = jnp.mean(y, axis=-1, keepdims=True)
    var = jnp.mean(jnp.square(y - mu), axis=-1, keepdims=True)
    y = ((y - mu) * lax.rsqrt(var + LNX_EPS)).reshape(B, S, RWKV_WIDTH).astype(r.dtype)
    y = y * lnx_g + lnx_b
    bonus = (jnp.sum(r_h * k_h * r_k, axis=-1, keepdims=True) * v_h).reshape(B, S, RWKV_WIDTH)
    return (y + bonus) * g


def setup_inputs(seed: int = 0) -> dict:
    key = jax.random.key(seed)
    keys = jax.random.split(key, 40)
    f32 = jnp.float32

    def nrm(i, shape, scale):
        return jax.random.normal(keys[i], shape, f32) * scale

    def gain(i, shape):
        return 1.0 + nrm(i, shape, 0.05)

    def unif(i, shape, lo, hi):
        return jax.random.uniform(keys[i], shape, f32, lo, hi)

    L, Lv, D, RW = DEPTH, DEPTH - 1, D_MODEL, RWKV_WIDTH
    P = N_DSA_COLS + N_RWKV_COLS
    return {
        'x': nrm(0, (BATCH, SEQ, D), 1.0),
        'c': nrm(1, (BATCH, D), 1.0),
        'ada_w': nrm(2, (L, D, 6 * D), 0.5 * D ** -0.5),
        'ada_b': nrm(3, (L, 6 * D), 0.02),
        'pre_g_mix': gain(4, (L, D)),
        'post_g_mix': gain(5, (L, D)),
        'pre_g_ffn': gain(6, (L, D)),
        'post_g_ffn': gain(7, (L, D)),
        'w_in': nrm(8, (L, D, P), D ** -0.5),
        'w_in_vres': nrm(9, (Lv, D, MV_LORA), D ** -0.5),
        'mu_shift': unif(10, (L, N_RWKV_COLS), 0.0, 1.0),
        'mu_vres': unif(11, (Lv, MV_LORA), 0.0, 1.0),
        'w_out': nrm(12, (L, MIX_WIDTH, D), MIX_WIDTH ** -0.5),
        'q_norm_g': gain(13, (L, Q_LORA)),
        'kv_norm_g': gain(14, (L, KV_LORA)),
        'w_q_up': nrm(15, (L, Q_LORA, DSA_HEADS, DSA_HEAD_DIM), Q_LORA ** -0.5),
        'w_qi_up': nrm(16, (L, Q_LORA, IDX_HEADS, IDX_DIM), Q_LORA ** -0.5),
        'w_k_up': nrm(17, (L, KV_LORA, DSA_HEADS, DSA_HEAD_DIM), KV_LORA ** -0.5),
        'w_v_up': nrm(18, (L, KV_LORA, DSA_HEADS, DSA_HEAD_DIM), KV_LORA ** -0.5),
        'kidx_ln_g': gain(19, (L, IDX_DIM)),
        'kidx_ln_b': nrm(20, (L, IDX_DIM), 0.02),
        'w0': unif(21, (L, RW), -3.0, 1.0),
        'w2': nrm(22, (L, DECAY_LORA, RW), 0.1),
        'a0': nrm(23, (L, RW), 0.1),
        'a2': nrm(24, (L, AAA_LORA, RW), 0.5 * AAA_LORA ** -0.5),
        'g2': nrm(25, (L, GATE_LORA, RW), GATE_LORA ** -0.5),
        'v0': nrm(26, (Lv, RW), 0.5),
        'v2': nrm(27, (Lv, MV_LORA, RW), 0.5 * MV_LORA ** -0.5),
        'k_k': 0.85 + nrm(28, (L, RW), 0.05),
        'k_a': gain(29, (L, RW)),
        'r_k': nrm(30, (L, RWKV_HEADS, RWKV_HEAD_DIM), 0.1),
        'lnx_g': gain(31, (L, RW)),
        'lnx_b': nrm(32, (L, RW), 0.02),
        'w_fc': nrm(33, (L, D, 2 * D_FF), D ** -0.5),
        'w_down': nrm(34, (L, D_FF, D), D_FF ** -0.5),
    }


def reference(x, c, ada_w, ada_b, pre_g_mix, post_g_mix, pre_g_ffn, post_g_ffn,
              w_in, w_in_vres, mu_shift, mu_vres, w_out,
              q_norm_g, kv_norm_g, w_q_up, w_qi_up, w_k_up, w_v_up, kidx_ln_g, kidx_ln_b,
              w0, w2, a0, a2, g2, v0, v2, k_k, k_a, r_k, lnx_g, lnx_b,
              w_fc, w_down):
    cond = jax.nn.silu(c)
    v_first = None
    for l in range(DEPTH):
        mod = cond @ ada_w[l] + ada_b[l]
        sh_a, sc_a, gt_a, sh_f, sc_f, gt_f = [m[:, None, :] for m in jnp.split(mod, 6, axis=-1)]

        h = rms_norm(x, pre_g_mix[l]) * (1 + sc_a) + sh_a
        if l == 0:
            w_proj, mu = w_in[l], mu_shift[l]
        else:
            w_proj = jnp.concatenate([w_in[l], w_in_vres[l - 1]], axis=-1)
            mu = jnp.concatenate([mu_shift[l], mu_vres[l - 1]], axis=-1)
        cols = h @ w_proj
        dsa_cols = cols[..., :N_DSA_COLS]
        rwkv_cols = token_shift(cols[..., N_DSA_COLS:], mu)
        c_q, c_kv, k_idx_raw, w_idx = split_cols(dsa_cols, DSA_COLS)
        if l == 0:
            r, k, v, w_lo, a_lo, g_lo = split_cols(rwkv_cols, RWKV_COLS)
            v_first = v
        else:
            r, k, v, w_lo, a_lo, g_lo, mv_lo = split_cols(rwkv_cols, RWKV_COLS + [MV_LORA])
            v = v + (v_first - v) * jax.nn.sigmoid(v0[l - 1] + mv_lo @ v2[l - 1])
        y_dsa = dsa_mixer(c_q, c_kv, k_idx_raw, w_idx, q_norm_g[l], kv_norm_g[l], w_q_up[l],
                          w_qi_up[l], w_k_up[l], w_v_up[l], kidx_ln_g[l], kidx_ln_b[l])
        y_rwkv = rwkv7_mixer(r, k, v, w_lo, a_lo, g_lo, w0[l], w2[l], a0[l], a2[l], g2[l],
                             k_k[l], k_a[l], r_k[l], lnx_g[l], lnx_b[l])
        y = jnp.concatenate([y_dsa, y_rwkv], axis=-1) @ w_out[l]
        x = x + gt_a * rms_norm(y, post_g_mix[l])

        h = rms_norm(x, pre_g_ffn[l]) * (1 + sc_f) + sh_f
        gate, up = jnp.split(h @ w_fc[l], 2, axis=-1)
        y = (jax.nn.silu(gate) * up) @ w_down[l]
        x = x + gt_f * rms_norm(y, post_g_ffn[l])
    return x
```

```python
import functools

import jax
import jax.numpy as jnp
from jax import lax
from jax.experimental import pallas as pl
from jax.experimental.pallas import tpu as pltpu

F32 = jnp.float32
BF16 = jnp.bfloat16

D_MODEL = 1024
CHUNK = 64
Q_BLOCK = 128
DSA_HEADS = 8
DSA_HEAD_DIM = 64
DSA_WIDTH = DSA_HEADS * DSA_HEAD_DIM
Q_LORA = 256
KV_LORA = 128
IDX_HEADS = 4
IDX_DIM = 64
TOPK_MAX = 256
RWKV_HEADS = 8
RWKV_HEAD_DIM = 64
RWKV_WIDTH = RWKV_HEADS * RWKV_HEAD_DIM
DECAY_LORA = 64
AAA_LORA = 64
MV_LORA = 32
GATE_LORA = 160
D_FF = ((8 * D_MODEL + 3 * 256 - 1) // (3 * 256)) * 256
NORM_EPS = 1e-6
LNX_EPS = 64e-5

LANES = 128
VMEM_LIMIT = 56 * 1024 * 1024

RW_R, RW_K, RW_V = 0, RWKV_WIDTH, 2 * RWKV_WIDTH
RW_WA = 3 * RWKV_WIDTH
RW_GM = RW_WA + DECAY_LORA + AAA_LORA
RW_PAD = RW_GM + 2 * LANES
COL_CQ = RW_PAD
COL_CKV = COL_CQ + Q_LORA
COL_KW = COL_CKV + KV_LORA
P_PAD = COL_KW + LANES

RW_CHUNK = 64
RW_TILE = 128
QUAD = 4 * RWKV_HEAD_DIM
N_BISECT = 22
NEG_BIG = -1e30


def _dot(a, b):
    return jnp.dot(a, b, preferred_element_type=F32)


def _dot_nt(a, b):
    return lax.dot_general(a, b, (((1,), (1,)), ((), ())), preferred_element_type=F32)


def _split(a):
    hi = a.astype(BF16)
    lo = (a - hi.astype(F32)).astype(BF16)
    return hi, lo


def _dot3(a, b):
    ah, al = _split(a)
    bh, bl = _split(b)
    return _dot(ah, bh) + _dot(al, bh) + _dot(ah, bl)


def _rms(x, g):
    return x * lax.rsqrt(jnp.mean(x * x, axis=-1, keepdims=True) + NORM_EPS) * g


def _sigmoid(x):
    return 1.0 / (1.0 + jnp.exp(-x))


def _mod_kernel(c_ref, w_ref, b_ref, o_ref):
    c = c_ref[...]
    cond = c * _sigmoid(c)
    o_ref[0] = _dot3(cond, w_ref[0]) + b_ref[0]


def _mod_call(c, ada_w, ada_b):
    L, D, D6 = ada_w.shape
    B = c.shape[0]
    nj = D6 // D
    return pl.pallas_call(
        _mod_kernel,
        grid=(L, nj),
        in_specs=[
            pl.BlockSpec((B, D), lambda l, j: (0, 0)),
            pl.BlockSpec((1, D, D), lambda l, j: (l, 0, j)),
            pl.BlockSpec((1, 1, D), lambda l, j: (l, 0, j)),
        ],
        out_specs=pl.BlockSpec((1, B, D), lambda l, j: (l, 0, j)),
        out_shape=jax.ShapeDtypeStruct((L, B, D6), F32),
        compiler_params=pltpu.CompilerParams(vmem_limit_bytes=VMEM_LIMIT),
    )(c, ada_w, ada_b.reshape(L, 1, D6))


def _inproj_kernel(x_ref, mod_ref, g_ref, w_ref, mu_ref, qg_ref, kvg_ref, lng_ref, lnb_ref,
                   rw_ref, cq_ref, ckv_ref, kw_ref, carry_ref):
    j = pl.program_id(1)
    tm = x_ref.shape[1]
    mod = mod_ref[0]
    h = _rms(x_ref[0], g_ref[...]) * (1.0 + mod[1:2]) + mod[0:1]
    cols = _dot(h.astype(BF16), w_ref[...])

    @pl.when(j == 0)
    def _():
        carry_ref[...] = jnp.zeros_like(carry_ref)

    u = cols[:, :RW_PAD]
    row = lax.broadcasted_iota(jnp.int32, (tm, 1), 0)
    prev = jnp.where(row == 0, carry_ref[0:1, :], pltpu.roll(u, 1, axis=0))
    carry_ref[0:1, :] = u[tm - 1:tm, :]
    rw_ref[0] = u + (prev - u) * mu_ref[...]

    cq_ref[0] = _rms(cols[:, COL_CQ:COL_CKV], qg_ref[...]).astype(BF16)
    ckv_ref[0] = _rms(cols[:, COL_CKV:COL_KW], kvg_ref[...]).astype(BF16)

    kw = cols[:, COL_KW:P_PAD]
    lane = lax.broadcasted_iota(jnp.int32, (1, LANES), 1)
    is_k = lane < IDX_DIM
    mean = jnp.sum(jnp.where(is_k, kw, 0.0), axis=-1, keepdims=True) * (1.0 / IDX_DIM)
    cen = jnp.where(is_k, kw - mean, 0.0)
    var = jnp.sum(cen * cen, axis=-1, keepdims=True) * (1.0 / IDX_DIM)
    kn = cen * lax.rsqrt(var + NORM_EPS) * lng_ref[...] + lnb_ref[...]
    kw_ref[0] = jnp.where(is_k, kn, kw * ((IDX_HEADS ** -0.5) * (IDX_DIM ** -0.5)))


def _inproj_call(x, mod, g, w, mu, qg, kvg, lng, lnb, tm):
    B, S, D = x.shape
    const = lambda b, j: (0, 0)
    tok = lambda b, j: (b, j, 0)
    return pl.pallas_call(
        _inproj_kernel,
        grid=(B, S // tm),
        in_specs=[
            pl.BlockSpec((1, tm, D), tok),
            pl.BlockSpec((1, 6, D), lambda b, j: (b, 0, 0)),
            pl.BlockSpec((1, D), const),
            pl.BlockSpec((D, P_PAD), const, pipeline_mode=pl.Buffered(1)),
            pl.BlockSpec((1, RW_PAD), const),
            pl.BlockSpec((1, Q_LORA), const),
            pl.BlockSpec((1, KV_LORA), const),
            pl.BlockSpec((1, LANES), const),
            pl.BlockSpec((1, LANES), const),
        ],
        out_specs=[
            pl.BlockSpec((1, tm, RW_PAD), tok),
            pl.BlockSpec((1, tm, Q_LORA), tok),
            pl.BlockSpec((1, tm, KV_LORA), tok),
            pl.BlockSpec((1, tm, LANES), tok),
        ],
        out_shape=[
            jax.ShapeDtypeStruct((B, S, RW_PAD), F32),
            jax.ShapeDtypeStruct((B, S, Q_LORA), BF16),
            jax.ShapeDtypeStruct((B, S, KV_LORA), BF16),
            jax.ShapeDtypeStruct((B, S, LANES), F32),
        ],
        scratch_shapes=[pltpu.VMEM((8, RW_PAD), F32)],
        compiler_params=pltpu.CompilerParams(
            dimension_semantics=("arbitrary", "arbitrary"), vmem_limit_bytes=VMEM_LIMIT),
    )(x, mod, g, w, mu, qg, kvg, lng, lnb)


def _softplus(z):
    return jnp.maximum(z, 0.0) + jnp.log(1.0 + jnp.exp(-jnp.abs(z)))


def _rwkv_kernel(*refs, has_vres):
    if has_vres:
        rw_ref, vf_ref, vec_ref, w2_ref, a2_ref, g2_ref, v2_ref, y_ref, h_ref = refs
    else:
        rw_ref, vec_ref, w2_ref, a2_ref, g2_ref, y_ref, h_ref = refs
    W = RWKV_WIDTH
    N = RWKV_HEAD_DIM
    C = RW_CHUNK
    tc = rw_ref.shape[1]

    @pl.when(pl.program_id(1) == 0)
    def _():
        h_ref[...] = jnp.zeros_like(h_ref)

    vec = vec_ref[...]
    w0, a0, v0, k_k, k_a, r_k, lnx_g, lnx_b = [vec[i:i + 1] for i in range(8)]

    r = rw_ref[0, :, RW_R:RW_R + W]
    k = rw_ref[0, :, RW_K:RW_K + W]
    v = rw_ref[0, :, RW_V:RW_V + W]
    wa = rw_ref[0, :, RW_WA:RW_GM]
    gm = rw_ref[0, :, RW_GM:RW_PAD]

    hi_ = lax.broadcasted_iota(jnp.int32, (W, W), 0) // N
    hj_ = lax.broadcasted_iota(jnp.int32, (W, W), 1) // N
    head_ones = (hi_ == hj_).astype(BF16)

    def headsum(t):
        th = t.astype(BF16)
        t1 = t - th.astype(F32)
        tm_ = t1.astype(BF16)
        tl = (t1 - tm_.astype(F32)).astype(BF16)
        return _dot(th, head_ones) + _dot(tm_, head_ones) + _dot(tl, head_ones)

    w_log = -_softplus(-(w0 + _dot3(jnp.tanh(wa), w2_ref[...]))) - 0.5
    logw = -jnp.exp(w_log)
    a = _sigmoid(a0 + _dot3(wa, a2_ref[...]))
    g = _dot3(_sigmoid(gm), g2_ref[...])
    if has_vres:
        v = v + (vf_ref[0] - v) * _sigmoid(v0 + _dot3(gm, v2_ref[...]))
    kk = k * k_k
    kk = kk / jnp.maximum(jnp.sqrt(headsum(kk * kk)), 1e-12)
    k = k * (1.0 + (a - 1.0) * k_a)

    ti = lax.broadcasted_iota(jnp.int32, (tc, tc), 0)
    tj = lax.broadcasted_iota(jnp.int32, (tc, tc), 1)
    tri = ((ti // C == tj // C) & (tj <= ti)).astype(BF16)
    lh = logw.astype(BF16)
    l1 = logw - lh.astype(F32)
    lm = l1.astype(BF16)
    ll = (l1 - lm.astype(F32)).astype(BF16)
    cum = _dot(tri, lh) + _dot(tri, lm) + _dot(tri, ll)

    bt = kk * jnp.exp(cum - logw)
    at = -(a * kk) * jnp.exp(-cum)
    kt = k * jnp.exp(-cum)
    rt = r * jnp.exp(cum)

    lane_q = lax.broadcasted_iota(jnp.int32, (1, QUAD), 1) // N
    hmask = [lane_q == h for h in range(4)]
    gr = lax.broadcasted_iota(jnp.int32, (2 * C, 2 * QUAD), 0)
    gc = lax.broadcasted_iota(jnp.int32, (2 * C, 2 * QUAD), 1) % C
    gmask = ((gr < C) & (gc < gr)) | ((gr >= C) & (gc <= gr - C))
    pr = lax.broadcasted_iota(jnp.int32, (C, LANES), 0)
    pc = lax.broadcasted_iota(jnp.int32, (C, LANES), 1)
    eye_pad = (pr == pc).astype(F32)
    left = pc < C
    bd = (lax.broadcasted_iota(jnp.int32, (QUAD, QUAD), 0) // N
          == lax.broadcasted_iota(jnp.int32, (QUAD, QUAD), 1) // N)
    zeros_cq = jnp.zeros((C, QUAD), F32)

    y_rows = []
    for c in range(tc // C):
        rs = slice(c * C, (c + 1) * C)
        tot = cum[c * C + C - 1:c * C + C, :]
        tail = jnp.exp(tot - cum[rs])
        dp = jnp.exp(tot)
        y_quads = []
        for q in range(W // QUAD):
            cs = slice(q * QUAD, (q + 1) * QUAD)
            bq, aq, kq, rq, vq = bt[rs, cs], at[rs, cs], kt[rs, cs], rt[rs, cs], v[rs, cs]
            atail = -(a[rs, cs] * kk[rs, cs]) * tail[:, cs]
            ktail = k[rs, cs] * tail[:, cs]

            lhs = jnp.concatenate([bq, rq], axis=0).astype(BF16)
            rhs_t = jnp.concatenate(
                [jnp.where(hmask[h], t, 0.0) for h in range(4) for t in (aq, kq)],
                axis=0).astype(BF16)
            gram = jnp.where(gmask, _dot_nt(lhs, rhs_t), 0.0)
            gram_b = gram.astype(BF16)

            t_pads = []
            for h in range(4):
                ak = jnp.where(left, gram[0:C, h * LANES:(h + 1) * LANES], 0.0)
                xk = eye_pad + ak
                for _ in range(5):
                    ak = _dot(ak[:, :C].astype(BF16), ak.astype(BF16))
                    xk = xk + _dot(ak[:, :C].astype(BF16), xk.astype(BF16))
                t_pads.append(xk)
            t_cat = jnp.concatenate(t_pads, axis=1).astype(BF16)

            v_masked = [jnp.where(hmask[h], vq, 0.0) for h in range(4)]
            vs0 = jnp.concatenate(
                [t for h in range(4) for t in (zeros_cq, v_masked[h])], axis=0).astype(BF16)
            akv = _dot(gram_b[0:C], vs0)

            hq = h_ref[q]
            z = _dot(lhs, hq.astype(BF16))
            xs = z[0:C] + akv
            xs_st = jnp.concatenate(
                [t for h in range(4) for t in (jnp.where(hmask[h], xs, 0.0), zeros_cq)],
                axis=0).astype(BF16)
            u = _dot(t_cat, xs_st)
            uv_st = jnp.concatenate(
                [t for h in range(4) for t in (jnp.where(hmask[h], u, 0.0), v_masked[h])],
                axis=0).astype(BF16)
            y_quads.append(z[C:2 * C] + _dot(gram_b[C:2 * C], uv_st))

            ak_t = jnp.concatenate([atail, ktail], axis=0).T.astype(BF16)
            uv = jnp.concatenate([u, vq], axis=0).astype(BF16)
            dp_t = jnp.broadcast_to(dp[:, cs], (LANES, QUAD)).T
            dp_col = jnp.concatenate([dp_t, dp_t], axis=1)
            h_ref[q] = dp_col * hq + jnp.where(bd, _dot(ak_t, uv), 0.0)
        y_rows.append(jnp.concatenate(y_quads, axis=1))
    y = jnp.concatenate(y_rows, axis=0)

    mu = headsum(y) * (1.0 / N)
    yc = y - mu
    var = headsum(yc * yc) * (1.0 / N)
    yn = yc * lax.rsqrt(var + LNX_EPS) * lnx_g + lnx_b
    bonus = headsum(r * k * r_k) * v
    y_ref[0] = ((yn + bonus) * g).astype(BF16)


def _rwkv_call(rw, vfirst_src, vec, w2p, a2p, g2p, v2p):
    B, S, _ = rw.shape
    has_vres = vfirst_src is not None
    tc = RW_TILE
    const = lambda b, j: (0, 0)
    in_specs = [pl.BlockSpec((1, tc, RW_PAD), lambda b, j: (b, j, 0))]
    args = [rw]
    if has_vres:
        in_specs.append(pl.BlockSpec((1, tc, RWKV_WIDTH), lambda b, j: (b, j, RW_V // RWKV_WIDTH)))
        args.append(vfirst_src)
    in_specs += [pl.BlockSpec((8, RWKV_WIDTH), const),
                 pl.BlockSpec((LANES, RWKV_WIDTH), const),
                 pl.BlockSpec((LANES, RWKV_WIDTH), const),
                 pl.BlockSpec((2 * LANES, RWKV_WIDTH), const)]
    args += [vec, w2p, a2p, g2p]
    if has_vres:
        in_specs.append(pl.BlockSpec((2 * LANES, RWKV_WIDTH), const))
        args.append(v2p)
    return pl.pallas_call(
        functools.partial(_rwkv_kernel, has_vres=has_vres),
        grid=(B, S // tc),
        in_specs=in_specs,
        out_specs=pl.BlockSpec((1, tc, RWKV_WIDTH), lambda b, j: (b, j, 0)),
        out_shape=jax.ShapeDtypeStruct((B, S, RWKV_WIDTH), BF16),
        scratch_shapes=[pltpu.VMEM((RWKV_WIDTH // QUAD, QUAD, QUAD), F32)],
        compiler_params=pltpu.CompilerParams(
            dimension_semantics=("arbitrary", "arbitrary"), vmem_limit_bytes=VMEM_LIMIT),
    )(*args)


def _dsa_kernel(cq_ref, kwq_ref, kw_ref, ckv_ref, wqt_ref, wk_ref, wqit_ref, wvt_ref, y_ref,
                kidx_s, ckvt_s, sc_s, bias_s, att_s, yt_s, *, top_k):
    i = pl.program_id(1)
    nblk = kidx_s.shape[0]
    QB = Q_BLOCK

    @pl.when(i == 0)
    def _():
        for b in range(nblk):
            kidx_s[b] = kw_ref[0, b * QB:(b + 1) * QB, :].astype(BF16)
            ckvt_s[b] = ckv_ref[0, b * QB:(b + 1) * QB, :].astype(F32).T.astype(BF16)

    cq_t = cq_ref[0].astype(F32).T.astype(BF16)
    q_t = _dot(wqt_ref[...], cq_t).astype(BF16)
    w_t = kwq_ref[0].T
    qi_t = [_dot(wqit_ref[h], cq_t).astype(BF16) for h in range(IDX_HEADS)]
    w_rows = [w_t[IDX_DIM + h:IDX_DIM + h + 1, :] for h in range(IDX_HEADS)]

    nkb = i + 1
    q_chunk = (i * QB + lax.broadcasted_iota(jnp.int32, (1, QB), 1)) // CHUNK
    k_row = lax.broadcasted_iota(jnp.int32, (QB, 1), 0)
    k_eff = jnp.minimum(top_k, (q_chunk + 1) * CHUNK).astype(F32)
    inf = jnp.float32(jnp.inf)

    def colmin(t):
        return jnp.min(t, axis=0, keepdims=True)

    def colsum(t):
        return jnp.sum(t, axis=0, keepdims=True)

    def score_body(kb, carry):
        lo, hi = carry
        kblk = kidx_s[kb]
        acc = jnp.zeros((QB, QB), F32)
        for h in range(IDX_HEADS):
            acc = acc + jnp.maximum(_dot(kblk, qi_t[h]), 0.0) * w_rows[h]
        adm = (kb * QB + k_row) // CHUNK <= q_chunk
        sc_s[kb] = jnp.where(adm, acc, -inf)
        lo = jnp.minimum(lo, jnp.where(adm, acc, inf))
        hi = jnp.maximum(hi, jnp.where(adm, acc, -inf))
        return lo, hi

    lo, hi = lax.fori_loop(0, nkb, score_body,
                           (jnp.full((QB, QB), inf, F32), jnp.full((QB, QB), -inf, F32)))
    lo = colmin(lo)
    hi = jnp.max(hi, axis=0, keepdims=True)

    def count_ge(t):
        def body(kb, acc):
            return acc + jnp.where(sc_s[kb] >= t, 1.0, 0.0)
        return colsum(lax.fori_loop(0, nkb, body, jnp.zeros((QB, QB), F32)))

    def bisect(_, carry):
        lo, hi = carry
        mid = 0.5 * lo + 0.5 * hi
        ge = count_ge(mid) >= k_eff
        return jnp.where(ge, mid, lo), jnp.where(ge, hi, mid)

    lo, hi = lax.fori_loop(0, N_BISECT, bisect, (lo, hi))

    def min_ge(t):
        def body(kb, acc):
            s = sc_s[kb]
            return jnp.minimum(acc, jnp.where(s >= t, s, inf))
        return colmin(lax.fori_loop(0, nkb, body, jnp.full((QB, QB), inf, F32)))

    def walk_cond(st):
        return st[2] > 0

    def walk_body(st):
        m, _, _ = st

        def body(kb, carry):
            cnt, nxt = carry
            s = sc_s[kb]
            gt = s > m
            return cnt + jnp.where(gt, 1.0, 0.0), jnp.minimum(nxt, jnp.where(gt, s, inf))

        cnt, nxt = lax.fori_loop(0, nkb, body,
                                 (jnp.zeros((QB, QB), F32), jnp.full((QB, QB), inf, F32)))
        n_gt = colsum(cnt)
        done = n_gt < k_eff
        pending = jnp.max(jnp.where(done, 0, 1))
        return jnp.where(done, m, colmin(nxt)), n_gt, pending

    thr, n_gt, _ = lax.while_loop(
        walk_cond, walk_body, (min_ge(lo), jnp.zeros((1, QB), F32), jnp.int32(1)))
    need = k_eff - n_gt

    tr = lax.broadcasted_iota(jnp.int32, (QB, QB), 0)
    tcn = lax.broadcasted_iota(jnp.int32, (QB, QB), 1)
    tri = (tcn <= tr).astype(BF16)

    def mask_body(kb, carry):
        s = sc_s[kb]
        tie = jnp.where(s == thr, 1.0, 0.0)
        pre = _dot(tri, tie.astype(BF16)) + carry
        sel = (s > thr) | ((s == thr) & (pre <= need))
        bias_s[kb] = jnp.where(sel, 0.0, NEG_BIG)
        return carry + colsum(tie)

    lax.fori_loop(0, nkb, mask_body, jnp.zeros((1, QB), F32))

    scale = DSA_HEAD_DIM ** -0.5
    for h in range(DSA_HEADS):
        qa_t = (_dot(wk_ref[h], q_t[h * DSA_HEAD_DIM:(h + 1) * DSA_HEAD_DIM]) * scale).astype(BF16)

        def att_body(kb, mx):
            kv = ckv_ref[0, pl.ds(pl.multiple_of(kb * QB, QB), QB), :]
            att = _dot(kv, qa_t) + bias_s[kb]
            att_s[kb] = att
            return jnp.maximum(mx, att)

        mx = lax.fori_loop(0, nkb, att_body, jnp.full((QB, QB), NEG_BIG, F32))
        mx = jnp.max(mx, axis=0, keepdims=True)

        def pv_body(kb, carry):
            l, o = carry
            p = jnp.exp(att_s[kb] - mx)
            return l + p, o + _dot(ckvt_s[kb], p.astype(BF16))

        l, o = lax.fori_loop(0, nkb, pv_body,
                             (jnp.zeros((QB, QB), F32), jnp.zeros((KV_LORA, QB), F32)))
        o = o / colsum(l)
        yt_s[h * DSA_HEAD_DIM:(h + 1) * DSA_HEAD_DIM, :] = _dot(wvt_ref[h], o.astype(BF16))

    y_ref[0] = yt_s[...].T.astype(BF16)


def _dsa_call(cq, kw, ckv, wqt, wk, wqit, wvt):
    B, S, _ = cq.shape
    QB = Q_BLOCK
    nblk = S // QB
    top_k = min(TOPK_MAX, S // 4)
    c2 = lambda b, i: (0, 0)
    c3 = lambda b, i: (0, 0, 0)
    return pl.pallas_call(
        functools.partial(_dsa_kernel, top_k=top_k),
        grid=(B, nblk),
        in_specs=[
            pl.BlockSpec((1, QB, Q_LORA), lambda b, i: (b, i, 0)),
            pl.BlockSpec((1, QB, LANES), lambda b, i: (b, i, 0)),
            pl.BlockSpec((1, S, LANES), lambda b, i: (b, 0, 0)),
            pl.BlockSpec((1, S, KV_LORA), lambda b, i: (b, 0, 0)),
            pl.BlockSpec((DSA_WIDTH, Q_LORA), c2),
            pl.BlockSpec((DSA_HEADS, KV_LORA, DSA_HEAD_DIM), c3),
            pl.BlockSpec((IDX_HEADS, LANES, Q_LORA), c3),
            pl.BlockSpec((DSA_HEADS, DSA_HEAD_DIM, KV_LORA), c3),
        ],
        out_specs=pl.BlockSpec((1, QB, DSA_WIDTH), lambda b, i: (b, i, 0)),
        out_shape=jax.ShapeDtypeStruct((B, S, DSA_WIDTH), BF16),
        scratch_shapes=[
            pltpu.VMEM((nblk, QB, LANES), BF16),
            pltpu.VMEM((nblk, KV_LORA, QB), BF16),
            pltpu.VMEM((nblk, QB, QB), F32),
            pltpu.VMEM((nblk, QB, QB), F32),
            pltpu.VMEM((nblk, QB, QB), F32),
            pltpu.VMEM((DSA_WIDTH, QB), F32),
        ],
        compiler_params=pltpu.CompilerParams(
            dimension_semantics=("arbitrary", "arbitrary"), vmem_limit_bytes=VMEM_LIMIT),
    )(cq, kw, kw, ckv, wqt, wk, wqit, wvt)


def _out_ffn_kernel(x_ref, yd_ref, yr_ref, mod_ref, gpm_ref, gff_ref, gpf_ref,
                    wo_ref, wfc_ref, wdn_ref, o_ref):
    mod = mod_ref[0]
    y = _dot(yd_ref[0], wo_ref[0]) + _dot(yr_ref[0], wo_ref[1])
    x1 = x_ref[0] + mod[2:3] * _rms(y, gpm_ref[...])
    h = (_rms(x1, gff_ref[...]) * (1.0 + mod[4:5]) + mod[3:4]).astype(BF16)
    fc = wdn_ref.shape[1]
    acc = jnp.zeros_like(x1)
    for c in range(wfc_ref.shape[0]):
        gu = _dot(h, wfc_ref[c])
        gate, up = gu[:, :fc], gu[:, fc:]
        acc = acc + _dot((gate * _sigmoid(gate) * up).astype(BF16), wdn_ref[c])
    o_ref[0] = x1 + mod[5:6] * _rms(acc, gpf_ref[...])


def _out_ffn_call(x, yd, yr, mod, gpm, gff, gpf, wo, wfc, wdn, tm):
    B, S, D = x.shape
    nc, _, fc2 = wfc.shape
    tok = lambda b, j: (b, j, 0)
    c2 = lambda b, j: (0, 0)
    c3 = lambda b, j: (0, 0, 0)
    return pl.pallas_call(
        _out_ffn_kernel,
        grid=(B, S // tm),
        in_specs=[
            pl.BlockSpec((1, tm, D), tok),
            pl.BlockSpec((1, tm, DSA_WIDTH), tok),
            pl.BlockSpec((1, tm, RWKV_WIDTH), tok),
            pl.BlockSpec((1, 6, D), lambda b, j: (b, 0, 0)),
            pl.BlockSpec((1, D), c2),
            pl.BlockSpec((1, D), c2),
            pl.BlockSpec((1, D), c2),
            pl.BlockSpec((2, DSA_WIDTH, D), c3, pipeline_mode=pl.Buffered(1)),
            pl.BlockSpec((nc, D, fc2), c3, pipeline_mode=pl.Buffered(1)),
            pl.BlockSpec((nc, fc2 // 2, D), c3, pipeline_mode=pl.Buffered(1)),
        ],
        out_specs=pl.BlockSpec((1, tm, D), tok),
        out_shape=jax.ShapeDtypeStruct((B, S, D), F32),
        compiler_params=pltpu.CompilerParams(
            dimension_semantics=("arbitrary", "arbitrary"), vmem_limit_bytes=VMEM_LIMIT),
    )(x, yd, yr, mod, gpm, gff, gpf, wo, wfc, wdn)


def _pad_rows(w, rows, at):
    out = jnp.zeros((rows,) + w.shape[1:], w.dtype)
    return out.at[at:at + w.shape[0]].set(w)


def kernel(x, c, ada_w, ada_b, pre_g_mix, post_g_mix, pre_g_ffn, post_g_ffn, w_in, w_in_vres, mu_shift, mu_vres, w_out, q_norm_g, kv_norm_g, w_q_up, w_qi_up, w_k_up, w_v_up, kidx_ln_g, kidx_ln_b, w0, w2, a0, a2, g2, v0, v2, k_k, k_a, r_k, lnx_g, lnx_b, w_fc, w_down):
    B, S, D = x.shape
    L = w_in.shape[0]
    n_dsa = Q_LORA + KV_LORA + IDX_DIM + IDX_HEADS
    tm = min(512, S)
    ff_chunk = D_FF // 2

    mod_all = _mod_call(c, ada_w, ada_b).reshape(L, B, 6, D)

    row = lambda t: t.reshape(1, -1)
    lane_pad = lambda t: jnp.pad(t, (0, LANES - t.shape[0])).reshape(1, LANES)

    v_first_src = None
    for l in range(L):
        wl = w_in[l]
        dsa_w, rw_w = wl[:, :n_dsa], wl[:, n_dsa:]
        if l == 0:
            mv_w = jnp.zeros((D, MV_LORA), F32)
            mv_mu = jnp.zeros((MV_LORA,), F32)
        else:
            mv_w, mv_mu = w_in_vres[l - 1], mu_vres[l - 1]
        zpad = lambda n: jnp.zeros((D, n), F32)
        w_all = jnp.concatenate(
            [rw_w, mv_w, zpad(RW_PAD - rw_w.shape[1] - MV_LORA),
             dsa_w, zpad(P_PAD - COL_KW - IDX_DIM - IDX_HEADS)], axis=1).astype(BF16)
        mu_all = jnp.concatenate(
            [mu_shift[l], mv_mu, jnp.zeros((RW_PAD - rw_w.shape[1] - MV_LORA,), F32)]).reshape(1, RW_PAD)

        rw, cq, ckv, kw = _inproj_call(
            x, mod_all[l], row(pre_g_mix[l]), w_all, mu_all, row(q_norm_g[l]), row(kv_norm_g[l]),
            lane_pad(kidx_ln_g[l]), lane_pad(kidx_ln_b[l]), tm)

        vec = jnp.stack([w0[l], a0[l], v0[l - 1] if l else jnp.zeros_like(w0[l]), k_k[l], k_a[l],
                         r_k[l].reshape(-1), lnx_g[l], lnx_b[l]])
        w2p = _pad_rows(w2[l], LANES, 0)
        a2p = _pad_rows(a2[l], LANES, DECAY_LORA)
        g2p = _pad_rows(g2[l], 2 * LANES, 0)
        v2p = _pad_rows(v2[l - 1], 2 * LANES, GATE_LORA) if l else None
        y_rwkv = _rwkv_call(rw, v_first_src, vec, w2p, a2p, g2p, v2p)
        if l == 0:
            v_first_src = rw

        wqt = w_q_up[l].reshape(Q_LORA, DSA_WIDTH).T.astype(BF16)
        wk = jnp.transpose(w_k_up[l], (1, 0, 2)).astype(BF16)
        wqit = jnp.pad(jnp.transpose(w_qi_up[l], (1, 2, 0)),
                       ((0, 0), (0, LANES - IDX_DIM), (0, 0))).astype(BF16)
        wvt = jnp.transpose(w_v_up[l], (1, 2, 0)).astype(BF16)
        y_dsa = _dsa_call(cq, kw, ckv, wqt, wk, wqit, wvt)

        wo = w_out[l].reshape(2, DSA_WIDTH, D).astype(BF16)
        wfc = w_fc[l].reshape(D, 2, D_FF // ff_chunk, ff_chunk)
        wfc = jnp.transpose(wfc, (2, 0, 1, 3)).reshape(D_FF // ff_chunk, D, 2 * ff_chunk).astype(BF16)
        wdn = w_down[l].reshape(D_FF // ff_chunk, ff_chunk, D).astype(BF16)
        x = _out_ffn_call(x, y_dsa, y_rwkv, mod_all[l], row(post_g_mix[l]), row(pre_g_ffn[l]),
                          row(post_g_ffn[l]), wo, wfc, wdn, tm)
    return x
```

```python
import functools

import jax
import jax.numpy as jnp
from jax import lax
from jax.experimental import pallas as pl
from jax.experimental.pallas import tpu as pltpu

F32 = jnp.float32
BF16 = jnp.bfloat16

D_MODEL = 1024
CHUNK = 64
Q_BLOCK = 128
DSA_HEADS = 8
DSA_HEAD_DIM = 64
DSA_WIDTH = DSA_HEADS * DSA_HEAD_DIM
Q_LORA = 256
KV_LORA = 128
IDX_HEADS = 4
IDX_DIM = 64
TOPK_MAX = 256
RWKV_HEADS = 8
RWKV_HEAD_DIM = 64
RWKV_WIDTH = RWKV_HEADS * RWKV_HEAD_DIM
DECAY_LORA = 64
AAA_LORA = 64
MV_LORA = 32
GATE_LORA = 160
D_FF = ((8 * D_MODEL + 3 * 256 - 1) // (3 * 256)) * 256
NORM_EPS = 1e-6
LNX_EPS = 64e-5

LANES = 128
VMEM_LIMIT = 56 * 1024 * 1024

RW_R, RW_K, RW_V = 0, RWKV_WIDTH, 2 * RWKV_WIDTH
RW_WA = 3 * RWKV_WIDTH
RW_GM = RW_WA + DECAY_LORA + AAA_LORA
RW_PAD = RW_GM + 2 * LANES
COL_CQ = RW_PAD
COL_CKV = COL_CQ + Q_LORA
COL_KW = COL_CKV + KV_LORA
P_PAD = COL_KW + LANES

RW_CHUNK = 64
RW_TILE = 128
QUAD = 4 * RWKV_HEAD_DIM
N_BISECT = 22
KEY_BLOCK = 256
ATT_HEAD_GROUP = 8
PART_ROWS = 32
NEG_BIG = -1e30


def _dot(a, b):
    return jnp.dot(a, b, preferred_element_type=F32)


def _dot_nt(a, b):
    return lax.dot_general(a, b, (((1,), (1,)), ((), ())), preferred_element_type=F32)


def _split(a):
    hi = a.astype(BF16)
    lo = (a - hi.astype(F32)).astype(BF16)
    return hi, lo


def _dot3(a, b):
    ah, al = _split(a)
    bh, bl = _split(b)
    return _dot(ah, bh) + _dot(al, bh) + _dot(ah, bl)


def _rms(x, g):
    return x * lax.rsqrt(jnp.mean(x * x, axis=-1, keepdims=True) + NORM_EPS) * g


def _sigmoid(x):
    return 1.0 / (1.0 + jnp.exp(-x))


def _mod_kernel(c_ref, w_ref, b_ref, o_ref):
    c = c_ref[...]
    cond = c * _sigmoid(c)
    o_ref[0] = _dot3(cond, w_ref[0]) + b_ref[0]


def _mod_call(c, ada_w, ada_b):
    L, D, D6 = ada_w.shape
    B = c.shape[0]
    nj = D6 // D
    return pl.pallas_call(
        _mod_kernel,
        grid=(L, nj),
        in_specs=[
            pl.BlockSpec((B, D), lambda l, j: (0, 0)),
            pl.BlockSpec((1, D, D), lambda l, j: (l, 0, j)),
            pl.BlockSpec((1, 1, D), lambda l, j: (l, 0, j)),
        ],
        out_specs=pl.BlockSpec((1, B, D), lambda l, j: (l, 0, j)),
        out_shape=jax.ShapeDtypeStruct((L, B, D6), F32),
        compiler_params=pltpu.CompilerParams(vmem_limit_bytes=VMEM_LIMIT),
    )(c, ada_w, ada_b.reshape(L, 1, D6))


def _inproj_kernel(x_ref, mod_ref, g_ref, w_ref, mu_ref, qg_ref, kvg_ref, lng_ref, lnb_ref,
                   rw_ref, cq_ref, ckv_ref, kw_ref, carry_ref):
    j = pl.program_id(1)
    tm = x_ref.shape[1]
    mod = mod_ref[0]
    h = _rms(x_ref[0], g_ref[...]) * (1.0 + mod[1:2]) + mod[0:1]
    cols = _dot(h.astype(BF16), w_ref[...])

    @pl.when(j == 0)
    def _():
        carry_ref[...] = jnp.zeros_like(carry_ref)

    u = cols[:, :RW_PAD]
    row = lax.broadcasted_iota(jnp.int32, (tm, 1), 0)
    prev = jnp.where(row == 0, carry_ref[0:1, :], pltpu.roll(u, 1, axis=0))
    carry_ref[0:1, :] = u[tm - 1:tm, :]
    rw_ref[0] = u + (prev - u) * mu_ref[...]

    cq_ref[0] = _rms(cols[:, COL_CQ:COL_CKV], qg_ref[...]).astype(BF16)
    ckv_ref[0] = _rms(cols[:, COL_CKV:COL_KW], kvg_ref[...]).astype(BF16)

    kw = cols[:, COL_KW:P_PAD]
    lane = lax.broadcasted_iota(jnp.int32, (1, LANES), 1)
    is_k = lane < IDX_DIM
    mean = jnp.sum(jnp.where(is_k, kw, 0.0), axis=-1, keepdims=True) * (1.0 / IDX_DIM)
    cen = jnp.where(is_k, kw - mean, 0.0)
    var = jnp.sum(cen * cen, axis=-1, keepdims=True) * (1.0 / IDX_DIM)
    kn = cen * lax.rsqrt(var + NORM_EPS) * lng_ref[...] + lnb_ref[...]
    kw_ref[0] = jnp.where(is_k, kn, kw * ((IDX_HEADS ** -0.5) * (IDX_DIM ** -0.5)))


def _inproj_call(x, mod, g, w, mu, qg, kvg, lng, lnb, tm):
    B, S, D = x.shape
    const = lambda b, j: (0, 0)
    tok = lambda b, j: (b, j, 0)
    return pl.pallas_call(
        _inproj_kernel,
        grid=(B, S // tm),
        in_specs=[
            pl.BlockSpec((1, tm, D), tok),
            pl.BlockSpec((1, 6, D), lambda b, j: (b, 0, 0)),
            pl.BlockSpec((1, D), const),
            pl.BlockSpec((D, P_PAD), const, pipeline_mode=pl.Buffered(1)),
            pl.BlockSpec((1, RW_PAD), const),
            pl.BlockSpec((1, Q_LORA), const),
            pl.BlockSpec((1, KV_LORA), const),
            pl.BlockSpec((1, LANES), const),
            pl.BlockSpec((1, LANES), const),
        ],
        out_specs=[
            pl.BlockSpec((1, tm, RW_PAD), tok),
            pl.BlockSpec((1, tm, Q_LORA), tok),
            pl.BlockSpec((1, tm, KV_LORA), tok),
            pl.BlockSpec((1, tm, LANES), tok),
        ],
        out_shape=[
            jax.ShapeDtypeStruct((B, S, RW_PAD), F32),
            jax.ShapeDtypeStruct((B, S, Q_LORA), BF16),
            jax.ShapeDtypeStruct((B, S, KV_LORA), BF16),
            jax.ShapeDtypeStruct((B, S, LANES), F32),
        ],
        scratch_shapes=[pltpu.VMEM((8, RW_PAD), F32)],
        compiler_params=pltpu.CompilerParams(
            dimension_semantics=("arbitrary", "arbitrary"), vmem_limit_bytes=VMEM_LIMIT),
    )(x, mod, g, w, mu, qg, kvg, lng, lnb)


def _softplus(z):
    return jnp.maximum(z, 0.0) + jnp.log(1.0 + jnp.exp(-jnp.abs(z)))


def _rwkv_kernel(*refs, has_vres):
    if has_vres:
        rw_ref, vf_ref, vec_ref, w2_ref, a2_ref, g2_ref, v2_ref, y_ref, h_ref = refs
    else:
        rw_ref, vec_ref, w2_ref, a2_ref, g2_ref, y_ref, h_ref = refs
    W = RWKV_WIDTH
    N = RWKV_HEAD_DIM
    C = RW_CHUNK
    tc = rw_ref.shape[1]

    @pl.when(pl.program_id(1) == 0)
    def _():
        h_ref[...] = jnp.zeros_like(h_ref)

    vec = vec_ref[...]
    w0, a0, v0, k_k, k_a, r_k, lnx_g, lnx_b = [vec[i:i + 1] for i in range(8)]

    r = rw_ref[0, :, RW_R:RW_R + W]
    k = rw_ref[0, :, RW_K:RW_K + W]
    v = rw_ref[0, :, RW_V:RW_V + W]
    wa = rw_ref[0, :, RW_WA:RW_GM]
    gm = rw_ref[0, :, RW_GM:RW_PAD]

    hi_ = lax.broadcasted_iota(jnp.int32, (W, W), 0) // N
    hj_ = lax.broadcasted_iota(jnp.int32, (W, W), 1) // N
    head_ones = (hi_ == hj_).astype(BF16)

    def headsum(t):
        th = t.astype(BF16)
        t1 = t - th.astype(F32)
        tm_ = t1.astype(BF16)
        tl = (t1 - tm_.astype(F32)).astype(BF16)
        return _dot(th, head_ones) + _dot(tm_, head_ones) + _dot(tl, head_ones)

    w_log = -_softplus(-(w0 + _dot3(jnp.tanh(wa), w2_ref[...]))) - 0.5
    logw = -jnp.exp(w_log)
    a = _sigmoid(a0 + _dot3(wa, a2_ref[...]))
    g = _dot3(_sigmoid(gm), g2_ref[...])
    if has_vres:
        v = v + (vf_ref[0] - v) * _sigmoid(v0 + _dot3(gm, v2_ref[...]))
    kk = k * k_k
    kk = kk / jnp.maximum(jnp.sqrt(headsum(kk * kk)), 1e-12)
    k = k * (1.0 + (a - 1.0) * k_a)

    ti = lax.broadcasted_iota(jnp.int32, (tc, tc), 0)
    tj = lax.broadcasted_iota(jnp.int32, (tc, tc), 1)
    tri = ((ti // C == tj // C) & (tj <= ti)).astype(BF16)
    lh = logw.astype(BF16)
    l1 = logw - lh.astype(F32)
    lm = l1.astype(BF16)
    ll = (l1 - lm.astype(F32)).astype(BF16)
    cum = _dot(tri, lh) + _dot(tri, lm) + _dot(tri, ll)

    bt = kk * jnp.exp(cum - logw)
    at = -(a * kk) * jnp.exp(-cum)
    kt = k * jnp.exp(-cum)
    rt = r * jnp.exp(cum)

    lane_q = lax.broadcasted_iota(jnp.int32, (1, QUAD), 1) // N
    hmask = [lane_q == h for h in range(4)]
    gr = lax.broadcasted_iota(jnp.int32, (2 * C, 2 * QUAD), 0)
    gc = lax.broadcasted_iota(jnp.int32, (2 * C, 2 * QUAD), 1) % C
    gmask = ((gr < C) & (gc < gr)) | ((gr >= C) & (gc <= gr - C))
    pr = lax.broadcasted_iota(jnp.int32, (C, LANES), 0)
    pc = lax.broadcasted_iota(jnp.int32, (C, LANES), 1)
    eye_pad = (pr == pc).astype(F32)
    left = pc < C
    bd = (lax.broadcasted_iota(jnp.int32, (QUAD, QUAD), 0) // N
          == lax.broadcasted_iota(jnp.int32, (QUAD, QUAD), 1) // N)
    zeros_cq = jnp.zeros((C, QUAD), F32)

    n_ch = tc // C
    n_q = W // QUAD
    cells = [(c, q) for c in range(n_ch) for q in range(n_q)]

    lhs, gram, gram_b, v_masked, ak_t, dp_col, vq_all = {}, {}, {}, {}, {}, {}, {}
    for c, q in cells:
        rs = slice(c * C, (c + 1) * C)
        cs = slice(q * QUAD, (q + 1) * QUAD)
        tot = cum[c * C + C - 1:c * C + C, cs]
        tail = jnp.exp(tot - cum[rs, cs])
        atail = -(a[rs, cs] * kk[rs, cs]) * tail
        ktail = k[rs, cs] * tail
        ak_t[c, q] = jnp.concatenate([atail, ktail], axis=0).T.astype(BF16)
        dp_t = jnp.broadcast_to(jnp.exp(tot), (LANES, QUAD)).T
        dp_col[c, q] = jnp.concatenate([dp_t, dp_t], axis=1)
        vq_all[c, q] = v[rs, cs]
        lhs[c, q] = jnp.concatenate([bt[rs, cs], rt[rs, cs]], axis=0).astype(BF16)
        rhs_t = jnp.concatenate(
            [jnp.where(hmask[h], t, 0.0) for h in range(4) for t in (at[rs, cs], kt[rs, cs])],
            axis=0).astype(BF16)
        gram[c, q] = jnp.where(gmask, _dot_nt(lhs[c, q], rhs_t), 0.0)
        gram_b[c, q] = gram[c, q].astype(BF16)
        v_masked[c, q] = [jnp.where(hmask[h], vq_all[c, q], 0.0) for h in range(4)]

    heads = [(c, q, h) for c, q in cells for h in range(4)]
    pw, xk = {}, {}
    for key in heads:
        c, q, h = key
        a_pad = jnp.where(left, gram[c, q][0:C, h * LANES:(h + 1) * LANES], 0.0)
        xk[key] = eye_pad + a_pad
        pw[key] = a_pad
    for key in heads:
        pw[key] = _dot(pw[key][:, :C].astype(BF16), pw[key].astype(BF16))
    for lvl in range(4):
        for key in heads:
            both = _dot(pw[key][:, :C].astype(BF16),
                        jnp.concatenate([xk[key], pw[key]], axis=1).astype(BF16))
            xk[key] = xk[key] + both[:, :LANES]
            pw[key] = both[:, LANES:]
    for key in heads:
        xk[key] = xk[key] + _dot(pw[key][:, :C].astype(BF16), xk[key].astype(BF16))

    t_cat, akv = {}, {}
    for c, q in cells:
        t_cat[c, q] = jnp.concatenate([xk[c, q, h] for h in range(4)], axis=1).astype(BF16)
        vs0 = jnp.concatenate(
            [t for h in range(4) for t in (zeros_cq, v_masked[c, q][h])], axis=0).astype(BF16)
        akv[c, q] = _dot(gram_b[c, q][0:C], vs0)

    hq = [h_ref[q] for q in range(n_q)]
    y_rows = []
    for c in range(n_ch):
        y_quads = []
        for q in range(n_q):
            z = _dot(lhs[c, q], hq[q].astype(BF16))
            xs = z[0:C] + akv[c, q]
            xs_st = jnp.concatenate(
                [t for h in range(4) for t in (jnp.where(hmask[h], xs, 0.0), zeros_cq)],
                axis=0).astype(BF16)
            u = _dot(t_cat[c, q], xs_st)
            uv_st = jnp.concatenate(
                [t for h in range(4) for t in (jnp.where(hmask[h], u, 0.0), v_masked[c, q][h])],
                axis=0).astype(BF16)
            y_quads.append(z[C:2 * C] + _dot(gram_b[c, q][C:2 * C], uv_st))
            uv = jnp.concatenate([u, vq_all[c, q]], axis=0).astype(BF16)
            hq[q] = dp_col[c, q] * hq[q] + jnp.where(bd, _dot(ak_t[c, q], uv), 0.0)
        y_rows.append(jnp.concatenate(y_quads, axis=1))
    for q in range(n_q):
        h_ref[q] = hq[q]
    y = jnp.concatenate(y_rows, axis=0)

    mu = headsum(y) * (1.0 / N)
    yc = y - mu
    var = headsum(yc * yc) * (1.0 / N)
    yn = yc * lax.rsqrt(var + LNX_EPS) * lnx_g + lnx_b
    bonus = headsum(r * k * r_k) * v
    y_ref[0] = ((yn + bonus) * g).astype(BF16)


def _rwkv_call(rw, vfirst_src, vec, w2p, a2p, g2p, v2p):
    B, S, _ = rw.shape
    has_vres = vfirst_src is not None
    tc = RW_TILE
    const = lambda b, j: (0, 0)
    in_specs = [pl.BlockSpec((1, tc, RW_PAD), lambda b, j: (b, j, 0))]
    args = [rw]
    if has_vres:
        in_specs.append(pl.BlockSpec((1, tc, RWKV_WIDTH), lambda b, j: (b, j, RW_V // RWKV_WIDTH)))
        args.append(vfirst_src)
    in_specs += [pl.BlockSpec((8, RWKV_WIDTH), const),
                 pl.BlockSpec((LANES, RWKV_WIDTH), const),
                 pl.BlockSpec((LANES, RWKV_WIDTH), const),
                 pl.BlockSpec((2 * LANES, RWKV_WIDTH), const)]
    args += [vec, w2p, a2p, g2p]
    if has_vres:
        in_specs.append(pl.BlockSpec((2 * LANES, RWKV_WIDTH), const))
        args.append(v2p)
    return pl.pallas_call(
        functools.partial(_rwkv_kernel, has_vres=has_vres),
        grid=(B, S // tc),
        in_specs=in_specs,
        out_specs=pl.BlockSpec((1, tc, RWKV_WIDTH), lambda b, j: (b, j, 0)),
        out_shape=jax.ShapeDtypeStruct((B, S, RWKV_WIDTH), BF16),
        scratch_shapes=[pltpu.VMEM((RWKV_WIDTH // QUAD, QUAD, QUAD), F32)],
        compiler_params=pltpu.CompilerParams(
            dimension_semantics=("arbitrary", "arbitrary"), vmem_limit_bytes=VMEM_LIMIT),
    )(*args)


def _part(t, op):
    r, n = t.shape
    return op(t.reshape(r // PART_ROWS, PART_ROWS, n), axis=0)


def _dsa_kernel(cq_ref, kwq_ref, kw_ref, ckv_ref, wqt_ref, wk_ref, wqit_ref, wvt_ref, y_ref,
                kidx_s, ckvt_s, sc_s, bias_s, att_s, yt_s, *, top_k):
    i = pl.program_id(1)
    nblk = kidx_s.shape[0]
    QB = Q_BLOCK
    KB = KEY_BLOCK
    HG = ATT_HEAD_GROUP

    @pl.when(i == 0)
    def _():
        for b in range(nblk):
            kidx_s[b] = kw_ref[0, b * KB:(b + 1) * KB, :].astype(BF16)
            ckvt_s[b] = ckv_ref[0, b * KB:(b + 1) * KB, :].astype(F32).T.astype(BF16)

    cq_t = cq_ref[0].astype(F32).T.astype(BF16)
    q_t = _dot(wqt_ref[...], cq_t).astype(BF16)
    w_t = kwq_ref[0].T
    qi_all = jnp.concatenate([_dot(wqit_ref[h], cq_t) for h in range(IDX_HEADS)],
                             axis=1).astype(BF16)
    w_all = jnp.concatenate([w_t[IDX_DIM + h:IDX_DIM + h + 1, :] for h in range(IDX_HEADS)],
                            axis=1)

    nkb = (i * QB + QB + KB - 1) // KB
    q_chunk = (i * QB + lax.broadcasted_iota(jnp.int32, (1, QB), 1)) // CHUNK
    k_row = lax.broadcasted_iota(jnp.int32, (KB, 1), 0)
    k_eff = jnp.minimum(top_k, (q_chunk + 1) * CHUNK).astype(F32)
    inf = jnp.float32(jnp.inf)
    one = jnp.float32(1.0)
    zero = jnp.float32(0.0)

    def fin_min(t):
        return jnp.min(t, axis=0, keepdims=True)

    def fin_max(t):
        return jnp.max(t, axis=0, keepdims=True)

    def fin_sum(t):
        return jnp.sum(t, axis=0, keepdims=True)

    def score_body(kb, carry):
        lo, hi = carry
        lg = jnp.maximum(_dot(kidx_s[kb], qi_all), zero) * w_all
        acc = jnp.zeros((KB, QB), F32)
        for h in range(IDX_HEADS):
            acc = acc + lg[:, h * QB:(h + 1) * QB]
        adm = (kb * KB + k_row) // CHUNK <= q_chunk
        sc_s[kb] = jnp.where(adm, acc, -inf)
        lo = jnp.minimum(lo, _part(jnp.where(adm, acc, inf), jnp.min))
        hi = jnp.maximum(hi, _part(jnp.where(adm, acc, -inf), jnp.max))
        return lo, hi

    lo, hi = lax.fori_loop(0, nkb, score_body,
                           (jnp.full((PART_ROWS, QB), inf, F32), jnp.full((PART_ROWS, QB), -inf, F32)))
    lo = fin_min(lo)
    hi = fin_max(hi)

    def count_ge(t):
        def body(kb, acc):
            return acc + _part(jnp.where(sc_s[kb] >= t, one, zero), jnp.sum)
        return fin_sum(lax.fori_loop(0, nkb, body, jnp.zeros((PART_ROWS, QB), F32)))

    def bisect(_, carry):
        lo, hi = carry
        mid = 0.5 * lo + 0.5 * hi
        ge = count_ge(mid) >= k_eff
        return jnp.where(ge, mid, lo), jnp.where(ge, hi, mid)

    lo, hi = lax.fori_loop(0, N_BISECT, bisect, (lo, hi))

    def min_ge(t):
        def body(kb, acc):
            s = sc_s[kb]
            return jnp.minimum(acc, _part(jnp.where(s >= t, s, inf), jnp.min))
        return fin_min(lax.fori_loop(0, nkb, body, jnp.full((PART_ROWS, QB), inf, F32)))

    def walk_cond(st):
        return st[2] > 0

    def walk_body(st):
        m, _, _ = st

        def body(kb, carry):
            cnt, nxt = carry
            s = sc_s[kb]
            gt = s > m
            return (cnt + _part(jnp.where(gt, one, zero), jnp.sum),
                    jnp.minimum(nxt, _part(jnp.where(gt, s, inf), jnp.min)))

        cnt, nxt = lax.fori_loop(0, nkb, body,
                                 (jnp.zeros((PART_ROWS, QB), F32), jnp.full((PART_ROWS, QB), inf, F32)))
        n_gt = fin_sum(cnt)
        done = n_gt < k_eff
        pending = jnp.max(jnp.where(done, 0, 1))
        return jnp.where(done, m, fin_min(nxt)), n_gt, pending

    thr, n_gt, _ = lax.while_loop(
        walk_cond, walk_body, (min_ge(lo), jnp.zeros((1, QB), F32), jnp.int32(1)))
    need = k_eff - n_gt

    tri = (lax.broadcasted_iota(jnp.int32, (KB, KB), 1)
           <= lax.broadcasted_iota(jnp.int32, (KB, KB), 0)).astype(BF16)

    def mask_body(kb, carry):
        s = sc_s[kb]
        tie = jnp.where(s == thr, one, zero)
        pre = _dot(tri, tie.astype(BF16)) + carry
        sel = (s > thr) | ((s == thr) & (pre <= need))
        bias_s[kb] = jnp.where(sel, zero, jnp.float32(NEG_BIG))
        return carry + fin_sum(_part(tie, jnp.sum))

    lax.fori_loop(0, nkb, mask_body, jnp.zeros((1, QB), F32))

    scale = DSA_HEAD_DIM ** -0.5
    dh = DSA_HEAD_DIM
    for g in range(DSA_HEADS // HG):
        qa_all = jnp.concatenate(
            [_dot(wk_ref[g * HG + h], q_t[(g * HG + h) * dh:(g * HG + h + 1) * dh]) * scale
             for h in range(HG)], axis=1).astype(BF16)

        def att_body(kb, mx):
            kv = ckv_ref[0, pl.ds(pl.multiple_of(kb * KB, KB), KB), :]
            lg = _dot(kv, qa_all)
            b = bias_s[kb]
            parts = []
            for h in range(HG):
                att = lg[:, h * QB:(h + 1) * QB] + b
                att_s[kb, :, h * QB:(h + 1) * QB] = att
                parts.append(_part(att, jnp.max))
            return jnp.maximum(mx, jnp.concatenate(parts, axis=1))

        mx = fin_max(lax.fori_loop(0, nkb, att_body,
                                   jnp.full((PART_ROWS, HG * QB), NEG_BIG, F32)))

        def pv_body(kb, carry):
            l, o = carry
            p = jnp.exp(att_s[kb] - mx)
            return l + _part(p, jnp.sum), o + _dot(ckvt_s[kb], p.astype(BF16))

        l, o = lax.fori_loop(0, nkb, pv_body,
                             (jnp.zeros((PART_ROWS, HG * QB), F32), jnp.zeros((KV_LORA, HG * QB), F32)))
        o = (o / fin_sum(l)).astype(BF16)
        for h in range(HG):
            hh = g * HG + h
            yt_s[hh * dh:(hh + 1) * dh, :] = _dot(wvt_ref[hh], o[:, h * QB:(h + 1) * QB])

    y_ref[0] = yt_s[...].T.astype(BF16)


def _dsa_call(cq, kw, ckv, wqt, wk, wqit, wvt):
    B, S, _ = cq.shape
    QB = Q_BLOCK
    KB = KEY_BLOCK
    nblk = S // KB
    top_k = min(TOPK_MAX, S // 4)
    c2 = lambda b, i: (0, 0)
    c3 = lambda b, i: (0, 0, 0)
    return pl.pallas_call(
        functools.partial(_dsa_kernel, top_k=top_k),
        grid=(B, S // QB),
        in_specs=[
            pl.BlockSpec((1, QB, Q_LORA), lambda b, i: (b, i, 0)),
            pl.BlockSpec((1, QB, LANES), lambda b, i: (b, i, 0)),
            pl.BlockSpec((1, S, LANES), lambda b, i: (b, 0, 0)),
            pl.BlockSpec((1, S, KV_LORA), lambda b, i: (b, 0, 0)),
            pl.BlockSpec((DSA_WIDTH, Q_LORA), c2),
            pl.BlockSpec((DSA_HEADS, KV_LORA, DSA_HEAD_DIM), c3),
            pl.BlockSpec((IDX_HEADS, LANES, Q_LORA), c3),
            pl.BlockSpec((DSA_HEADS, DSA_HEAD_DIM, KV_LORA), c3),
        ],
        out_specs=pl.BlockSpec((1, QB, DSA_WIDTH), lambda b, i: (b, i, 0)),
        out_shape=jax.ShapeDtypeStruct((B, S, DSA_WIDTH), BF16),
        scratch_shapes=[
            pltpu.VMEM((nblk, KB, LANES), BF16),
            pltpu.VMEM((nblk, KV_LORA, KB), BF16),
            pltpu.VMEM((nblk, KB, QB), F32),
            pltpu.VMEM((nblk, KB, QB), F32),
            pltpu.VMEM((nblk, KB, ATT_HEAD_GROUP * QB), F32),
            pltpu.VMEM((DSA_WIDTH, QB), F32),
        ],
        compiler_params=pltpu.CompilerParams(
            dimension_semantics=("arbitrary", "arbitrary"), vmem_limit_bytes=VMEM_LIMIT),
    )(cq, kw, kw, ckv, wqt, wk, wqit, wvt)


def _out_ffn_kernel(x_ref, yd_ref, yr_ref, mod_ref, gpm_ref, gff_ref, gpf_ref,
                    wo_ref, wfc_ref, wdn_ref, o_ref):
    mod = mod_ref[0]
    y = _dot(yd_ref[0], wo_ref[0]) + _dot(yr_ref[0], wo_ref[1])
    x1 = x_ref[0] + mod[2:3] * _rms(y, gpm_ref[...])
    h = (_rms(x1, gff_ref[...]) * (1.0 + mod[4:5]) + mod[3:4]).astype(BF16)
    fc = wdn_ref.shape[1]
    acc = jnp.zeros_like(x1)
    for c in range(wfc_ref.shape[0]):
        gu = _dot(h, wfc_ref[c])
        gate, up = gu[:, :fc], gu[:, fc:]
        acc = acc + _dot((gate * _sigmoid(gate) * up).astype(BF16), wdn_ref[c])
    o_ref[0] = x1 + mod[5:6] * _rms(acc, gpf_ref[...])


def _out_ffn_call(x, yd, yr, mod, gpm, gff, gpf, wo, wfc, wdn, tm):
    B, S, D = x.shape
    nc, _, fc2 = wfc.shape
    tok = lambda b, j: (b, j, 0)
    c2 = lambda b, j: (0, 0)
    c3 = lambda b, j: (0, 0, 0)
    return pl.pallas_call(
        _out_ffn_kernel,
        grid=(B, S // tm),
        in_specs=[
            pl.BlockSpec((1, tm, D), tok),
            pl.BlockSpec((1, tm, DSA_WIDTH), tok),
            pl.BlockSpec((1, tm, RWKV_WIDTH), tok),
            pl.BlockSpec((1, 6, D), lambda b, j: (b, 0, 0)),
            pl.BlockSpec((1, D), c2),
            pl.BlockSpec((1, D), c2),
            pl.BlockSpec((1, D), c2),
            pl.BlockSpec((2, DSA_WIDTH, D), c3, pipeline_mode=pl.Buffered(1)),
            pl.BlockSpec((nc, D, fc2), c3, pipeline_mode=pl.Buffered(1)),
            pl.BlockSpec((nc, fc2 // 2, D), c3, pipeline_mode=pl.Buffered(1)),
        ],
        out_specs=pl.BlockSpec((1, tm, D), tok),
        out_shape=jax.ShapeDtypeStruct((B, S, D), F32),
        compiler_params=pltpu.CompilerParams(
            dimension_semantics=("arbitrary", "arbitrary"), vmem_limit_bytes=VMEM_LIMIT),
    )(x, yd, yr, mod, gpm, gff, gpf, wo, wfc, wdn)


def _pad_rows(w, rows, at):
    out = jnp.zeros((rows,) + w.shape[1:], w.dtype)
    return out.at[at:at + w.shape[0]].set(w)


def kernel(x, c, ada_w, ada_b, pre_g_mix, post_g_mix, pre_g_ffn, post_g_ffn, w_in, w_in_vres, mu_shift, mu_vres, w_out, q_norm_g, kv_norm_g, w_q_up, w_qi_up, w_k_up, w_v_up, kidx_ln_g, kidx_ln_b, w0, w2, a0, a2, g2, v0, v2, k_k, k_a, r_k, lnx_g, lnx_b, w_fc, w_down):
    B, S, D = x.shape
    L = w_in.shape[0]
    n_dsa = Q_LORA + KV_LORA + IDX_DIM + IDX_HEADS
    tm = min(512, S)
    ff_chunk = D_FF // 2

    mod_all = _mod_call(c, ada_w, ada_b).reshape(L, B, 6, D)

    row = lambda t: t.reshape(1, -1)
    lane_pad = lambda t: jnp.pad(t, (0, LANES - t.shape[0])).reshape(1, LANES)

    v_first_src = None
    for l in range(L):
        wl = w_in[l]
        dsa_w, rw_w = wl[:, :n_dsa], wl[:, n_dsa:]
        if l == 0:
            mv_w = jnp.zeros((D, MV_LORA), F32)
            mv_mu = jnp.zeros((MV_LORA,), F32)
        else:
            mv_w, mv_mu = w_in_vres[l - 1], mu_vres[l - 1]
        zpad = lambda n: jnp.zeros((D, n), F32)
        w_all = jnp.concatenate(
            [rw_w, mv_w, zpad(RW_PAD - rw_w.shape[1] - MV_LORA),
             dsa_w, zpad(P_PAD - COL_KW - IDX_DIM - IDX_HEADS)], axis=1).astype(BF16)
        mu_all = jnp.concatenate(
            [mu_shift[l], mv_mu, jnp.zeros((RW_PAD - rw_w.shape[1] - MV_LORA,), F32)]).reshape(1, RW_PAD)

        rw, cq, ckv, kw = _inproj_call(
            x, mod_all[l], row(pre_g_mix[l]), w_all, mu_all, row(q_norm_g[l]), row(kv_norm_g[l]),
            lane_pad(kidx_ln_g[l]), lane_pad(kidx_ln_b[l]), tm)

        vec = jnp.stack([w0[l], a0[l], v0[l - 1] if l else jnp.zeros_like(w0[l]), k_k[l], k_a[l],
                         r_k[l].reshape(-1), lnx_g[l], lnx_b[l]])
        w2p = _pad_rows(w2[l], LANES, 0)
        a2p = _pad_rows(a2[l], LANES, DECAY_LORA)
        g2p = _pad_rows(g2[l], 2 * LANES, 0)
        v2p = _pad_rows(v2[l - 1], 2 * LANES, GATE_LORA) if l else None
        y_rwkv = _rwkv_call(rw, v_first_src, vec, w2p, a2p, g2p, v2p)
        if l == 0:
            v_first_src = rw

        wqt = w_q_up[l].reshape(Q_LORA, DSA_WIDTH).T.astype(BF16)
        wk = jnp.transpose(w_k_up[l], (1, 0, 2)).astype(BF16)
        wqit = jnp.pad(jnp.transpose(w_qi_up[l], (1, 2, 0)),
                       ((0, 0), (0, LANES - IDX_DIM), (0, 0))).astype(BF16)
        wvt = jnp.transpose(w_v_up[l], (1, 2, 0)).astype(BF16)
        y_dsa = _dsa_call(cq, kw, ckv, wqt, wk, wqit, wvt)

        wo = w_out[l].reshape(2, DSA_WIDTH, D).astype(BF16)
        wfc = w_fc[l].reshape(D, 2, D_FF // ff_chunk, ff_chunk)
        wfc = jnp.transpose(wfc, (2, 0, 1, 3)).reshape(D_FF // ff_chunk, D, 2 * ff_chunk).astype(BF16)
        wdn = w_down[l].reshape(D_FF // ff_chunk, ff_chunk, D).astype(BF16)
        x = _out_ffn_call(x, y_dsa, y_rwkv, mod_all[l], row(post_g_mix[l]), row(pre_g_ffn[l]),
                          row(post_g_ffn[l]), wo, wfc, wdn, tm)
    return x
```

```python
import functools

import jax
import jax.numpy as jnp
from jax import lax
from jax.experimental import pallas as pl
from jax.experimental.pallas import tpu as pltpu

F32 = jnp.float32
BF16 = jnp.bfloat16

D_MODEL = 1024
CHUNK = 64
Q_BLOCK = 128
DSA_HEADS = 8
DSA_HEAD_DIM = 64
DSA_WIDTH = DSA_HEADS * DSA_HEAD_DIM
Q_LORA = 256
KV_LORA = 128
IDX_HEADS = 4
IDX_DIM = 64
TOPK_MAX = 256
RWKV_HEADS = 8
RWKV_HEAD_DIM = 64
RWKV_WIDTH = RWKV_HEADS * RWKV_HEAD_DIM
DECAY_LORA = 64
AAA_LORA = 64
MV_LORA = 32
GATE_LORA = 160
D_FF = ((8 * D_MODEL + 3 * 256 - 1) // (3 * 256)) * 256
NORM_EPS = 1e-6
LNX_EPS = 64e-5

LANES = 128
VMEM_LIMIT = 56 * 1024 * 1024

RW_R, RW_K, RW_V = 0, RWKV_WIDTH, 2 * RWKV_WIDTH
RW_WA = 3 * RWKV_WIDTH
RW_GM = RW_WA + DECAY_LORA + AAA_LORA
RW_PAD = RW_GM + 2 * LANES
COL_CQ = RW_PAD
COL_CKV = COL_CQ + Q_LORA
COL_KW = COL_CKV + KV_LORA
P_PAD = COL_KW + LANES

RW_CHUNK = 64
RW_TILE = 256
RW_SEQS = 4
QUAD = 4 * RWKV_HEAD_DIM
N_BISECT = 22
KEY_BLOCK = 256
ATT_HEAD_GROUP = 8
PART_ROWS = 32
ONES_ROWS = 16
LOG2_E = 1.4426950408889634
NEG_BIG = -1e30


def _dot(a, b):
    return jnp.dot(a, b, preferred_element_type=F32)


def _dot_nt(a, b):
    return lax.dot_general(a, b, (((1,), (1,)), ((), ())), preferred_element_type=F32)


def _split(a):
    hi = a.astype(BF16)
    lo = (a - hi.astype(F32)).astype(BF16)
    return hi, lo


def _dot3(a, b):
    ah, al = _split(a)
    bh, bl = _split(b)
    return _dot(ah, bh) + _dot(al, bh) + _dot(ah, bl)


def _rms(x, g):
    return x * lax.rsqrt(jnp.mean(x * x, axis=-1, keepdims=True) + NORM_EPS) * g


def _sigmoid(x):
    return 1.0 / (1.0 + jnp.exp(-x))


def _mod_kernel(c_ref, w_ref, b_ref, o_ref):
    c = c_ref[...]
    cond = c * _sigmoid(c)
    o_ref[0] = _dot3(cond, w_ref[0]) + b_ref[0]


def _mod_call(c, ada_w, ada_b):
    L, D, D6 = ada_w.shape
    B = c.shape[0]
    nj = D6 // D
    return pl.pallas_call(
        _mod_kernel,
        grid=(L, nj),
        in_specs=[
            pl.BlockSpec((B, D), lambda l, j: (0, 0)),
            pl.BlockSpec((1, D, D), lambda l, j: (l, 0, j)),
            pl.BlockSpec((1, 1, D), lambda l, j: (l, 0, j)),
        ],
        out_specs=pl.BlockSpec((1, B, D), lambda l, j: (l, 0, j)),
        out_shape=jax.ShapeDtypeStruct((L, B, D6), F32),
        compiler_params=pltpu.CompilerParams(vmem_limit_bytes=VMEM_LIMIT),
    )(c, ada_w, ada_b.reshape(L, 1, D6))


def _inproj_kernel(x_ref, mod_ref, g_ref, w_ref, mu_ref, qg_ref, kvg_ref, lng_ref, lnb_ref,
                   rw_ref, cq_ref, ckv_ref, kw_ref, carry_ref):
    j = pl.program_id(1)
    tm = x_ref.shape[1]
    mod = mod_ref[0]
    h = _rms(x_ref[0], g_ref[...]) * (1.0 + mod[1:2]) + mod[0:1]
    cols = _dot(h.astype(BF16), w_ref[...])

    @pl.when(j == 0)
    def _():
        carry_ref[...] = jnp.zeros_like(carry_ref)

    u = cols[:, :RW_PAD]
    row = lax.broadcasted_iota(jnp.int32, (tm, 1), 0)
    prev = jnp.where(row == 0, carry_ref[0:1, :], pltpu.roll(u, 1, axis=0))
    carry_ref[0:1, :] = u[tm - 1:tm, :]
    rw_ref[0] = u + (prev - u) * mu_ref[...]

    cq_ref[0] = _rms(cols[:, COL_CQ:COL_CKV], qg_ref[...]).astype(BF16)
    ckv_ref[0] = _rms(cols[:, COL_CKV:COL_KW], kvg_ref[...]).astype(BF16)

    kw = cols[:, COL_KW:P_PAD]
    lane = lax.broadcasted_iota(jnp.int32, (1, LANES), 1)
    is_k = lane < IDX_DIM
    mean = jnp.sum(jnp.where(is_k, kw, 0.0), axis=-1, keepdims=True) * (1.0 / IDX_DIM)
    cen = jnp.where(is_k, kw - mean, 0.0)
    var = jnp.sum(cen * cen, axis=-1, keepdims=True) * (1.0 / IDX_DIM)
    kn = cen * lax.rsqrt(var + NORM_EPS) * lng_ref[...] + lnb_ref[...]
    kw_ref[0] = jnp.where(is_k, kn, kw * ((IDX_HEADS ** -0.5) * (IDX_DIM ** -0.5)))


def _inproj_call(x, mod, g, w, mu, qg, kvg, lng, lnb, tm):
    B, S, D = x.shape
    const = lambda b, j: (0, 0)
    tok = lambda b, j: (b, j, 0)
    return pl.pallas_call(
        _inproj_kernel,
        grid=(B, S // tm),
        in_specs=[
            pl.BlockSpec((1, tm, D), tok),
            pl.BlockSpec((1, 6, D), lambda b, j: (b, 0, 0)),
            pl.BlockSpec((1, D), const),
            pl.BlockSpec((D, P_PAD), const, pipeline_mode=pl.Buffered(1)),
            pl.BlockSpec((1, RW_PAD), const),
            pl.BlockSpec((1, Q_LORA), const),
            pl.BlockSpec((1, KV_LORA), const),
            pl.BlockSpec((1, LANES), const),
            pl.BlockSpec((1, LANES), const),
        ],
        out_specs=[
            pl.BlockSpec((1, tm, RW_PAD), tok),
            pl.BlockSpec((1, tm, Q_LORA), tok),
            pl.BlockSpec((1, tm, KV_LORA), tok),
            pl.BlockSpec((1, tm, LANES), tok),
        ],
        out_shape=[
            jax.ShapeDtypeStruct((B, S, RW_PAD), F32),
            jax.ShapeDtypeStruct((B, S, Q_LORA), BF16),
            jax.ShapeDtypeStruct((B, S, KV_LORA), BF16),
            jax.ShapeDtypeStruct((B, S, LANES), F32),
        ],
        scratch_shapes=[pltpu.VMEM((8, RW_PAD), F32)],
        compiler_params=pltpu.CompilerParams(
            dimension_semantics=("arbitrary", "arbitrary"), vmem_limit_bytes=VMEM_LIMIT),
    )(x, mod, g, w, mu, qg, kvg, lng, lnb)


def _softplus(z):
    return jnp.maximum(z, 0.0) + jnp.log(1.0 + jnp.exp(-jnp.abs(z)))


def _rwkv_kernel(*refs, has_vres):
    if has_vres:
        rw_ref, vf_ref, vec_ref, w2_ref, a2_ref, g2_ref, v2_ref, y_ref, h_ref = refs
    else:
        rw_ref, vec_ref, w2_ref, a2_ref, g2_ref, y_ref, h_ref = refs
    W = RWKV_WIDTH
    N = RWKV_HEAD_DIM
    C = RW_CHUNK
    nb, tcb = rw_ref.shape[0], rw_ref.shape[1]
    tc = nb * tcb

    @pl.when(pl.program_id(1) == 0)
    def _():
        h_ref[...] = jnp.zeros_like(h_ref)

    vec = vec_ref[...]
    w0, a0, v0, k_k, k_a, r_k, lnx_g, lnx_b = [vec[i:i + 1] for i in range(8)]

    def rows(ref, lo_, hi_):
        return ref[:, :, lo_:hi_].reshape(tc, hi_ - lo_)

    r = rows(rw_ref, RW_R, RW_R + W)
    k = rows(rw_ref, RW_K, RW_K + W)
    v = rows(rw_ref, RW_V, RW_V + W)
    wa = rows(rw_ref, RW_WA, RW_GM)
    gm = rows(rw_ref, RW_GM, RW_PAD)

    bd = (lax.broadcasted_iota(jnp.int32, (QUAD, QUAD), 0) // N
          == lax.broadcasted_iota(jnp.int32, (QUAD, QUAD), 1) // N)
    head_ones = bd.astype(BF16)

    def headsum(t):
        th = t.astype(BF16)
        tl = (t - th.astype(F32)).astype(BF16)
        return jnp.concatenate(
            [_dot(th[:, q * QUAD:(q + 1) * QUAD], head_ones)
             + _dot(tl[:, q * QUAD:(q + 1) * QUAD], head_ones) for q in range(W // QUAD)], axis=1)

    w_log = -_softplus(-(w0 + _dot3(jnp.tanh(wa), w2_ref[...]))) - 0.5
    logw = -jnp.exp(w_log)
    a = _sigmoid(a0 + _dot(wa.astype(BF16), a2_ref[...]))
    g = _dot(_sigmoid(gm).astype(BF16), g2_ref[...])
    if has_vres:
        v = v + (rows(vf_ref, 0, W) - v) * _sigmoid(v0 + _dot(gm.astype(BF16), v2_ref[...]))
    kk = k * k_k
    kk = kk / jnp.maximum(jnp.sqrt(headsum(kk * kk)), 1e-12)
    k = k * (1.0 + (a - 1.0) * k_a)

    ti = lax.broadcasted_iota(jnp.int32, (tc, tc), 0)
    tj = lax.broadcasted_iota(jnp.int32, (tc, tc), 1)
    tri = ((ti // C == tj // C) & (tj <= ti)).astype(BF16)
    lh = logw.astype(BF16)
    l1 = logw - lh.astype(F32)
    lm = l1.astype(BF16)
    ll = (l1 - lm.astype(F32)).astype(BF16)
    cum = _dot(tri, lh) + _dot(tri, lm) + _dot(tri, ll)

    bt = kk * jnp.exp(cum - logw)
    at = -(a * kk) * jnp.exp(-cum)
    kt = k * jnp.exp(-cum)
    rt = r * jnp.exp(cum)

    lane_q = lax.broadcasted_iota(jnp.int32, (1, QUAD), 1) // N
    hmask = [lane_q == h for h in range(4)]
    gr = lax.broadcasted_iota(jnp.int32, (2 * C, 2 * QUAD), 0)
    gc = lax.broadcasted_iota(jnp.int32, (2 * C, 2 * QUAD), 1) % C
    gmask = ((gr < C) & (gc < gr)) | ((gr >= C) & (gc <= gr - C))
    pr = lax.broadcasted_iota(jnp.int32, (C, LANES), 0)
    pc = lax.broadcasted_iota(jnp.int32, (C, LANES), 1)
    eye_pad = (pr == pc).astype(F32)
    left = pc < C
    zeros_cq = jnp.zeros((C, QUAD), F32)
    n_ch = tc // C
    n_q = W // QUAD

    cells = [(c, q) for c in range(n_ch) for q in range(n_q)]

    cell = {}
    for c, q in cells:
        rs = slice(c * C, (c + 1) * C)
        cs = slice(q * QUAD, (q + 1) * QUAD)
        tot = cum[c * C + C - 1:c * C + C, cs]
        tail = jnp.exp(tot - cum[rs, cs])
        atail = -(a[rs, cs] * kk[rs, cs]) * tail
        ktail = k[rs, cs] * tail
        dp_t = jnp.broadcast_to(jnp.exp(tot), (LANES, QUAD)).T
        vq = v[rs, cs]
        lhs = jnp.concatenate([bt[rs, cs], rt[rs, cs]], axis=0).astype(BF16)
        rhs_t = jnp.concatenate(
            [jnp.where(hmask[h], t, 0.0) for h in range(4) for t in (kt[rs, cs], at[rs, cs])],
            axis=0).astype(BF16)
        gram = jnp.where(gmask, _dot_nt(lhs, rhs_t), 0.0)
        cell[c, q] = dict(
            ak_t=jnp.concatenate([atail, ktail], axis=0).T.astype(BF16),
            dp_col=jnp.concatenate([dp_t, dp_t], axis=1), vq=vq, lhs=lhs, gram=gram,
            gram_b=gram.astype(BF16),
            v_masked=[jnp.where(hmask[h], vq, 0.0) for h in range(4)])

    pairs = [(c, q, j) for c, q in cells for j in range(2)]
    st = {}
    for c, q, j in pairs:
        for h in (2 * j, 2 * j + 1):
            st[c, q, h] = jnp.where(left, eye_pad, cell[c, q]["gram"][0:C, h * LANES:(h + 1) * LANES])
    zeros_pad = jnp.zeros((C, LANES), F32)
    for _ in range(6):
        for c, q, j in pairs:
            s0, s1 = st[c, q, 2 * j], st[c, q, 2 * j + 1]
            lhs2 = jnp.where(left, pltpu.roll(s0, C, axis=1), s1).astype(BF16)
            rhs2 = jnp.concatenate(
                [jnp.concatenate([s0, zeros_pad], axis=1),
                 jnp.concatenate([zeros_pad, s1], axis=1)], axis=0).astype(BF16)
            res = _dot(lhs2, rhs2)
            st[c, q, 2 * j] = jnp.where(left, s0 + res[:, :LANES], res[:, :LANES])
            st[c, q, 2 * j + 1] = jnp.where(left, s1 + res[:, LANES:], res[:, LANES:])

    for c, q in cells:
        o = cell[c, q]
        o["t_cat"] = jnp.concatenate(
            [jnp.where(left, st[c, q, h], 0.0) for h in range(4)], axis=1).astype(BF16)
        vs0 = jnp.concatenate(
            [t for h in range(4) for t in (o["v_masked"][h], zeros_cq)], axis=0).astype(BF16)
        o["akv"] = _dot(o["gram_b"][0:C], vs0)

    def recur(chunks, hq):
        keys = [(b, q) for b in range(nb) for q in range(n_q)]
        z, u, y = {}, {}, {}
        for b, q in keys:
            z[b, q] = _dot(cell[chunks[b], q]["lhs"], hq[b][q].astype(BF16))
        for b, q in keys:
            o = cell[chunks[b], q]
            xs = z[b, q][0:C] + o["akv"]
            xs_st = jnp.concatenate(
                [t for h in range(4) for t in (jnp.where(hmask[h], xs, 0.0), zeros_cq)],
                axis=0).astype(BF16)
            u[b, q] = _dot(o["t_cat"], xs_st)
        for b, q in keys:
            o = cell[chunks[b], q]
            uv = jnp.concatenate([u[b, q], o["vq"]], axis=0).astype(BF16)
            hq[b][q] = o["dp_col"] * hq[b][q] + jnp.where(bd, _dot(o["ak_t"], uv), 0.0)
        for b, q in keys:
            o = cell[chunks[b], q]
            uv_st = jnp.concatenate(
                [t for h in range(4) for t in (o["v_masked"][h], jnp.where(hmask[h], u[b, q], 0.0))],
                axis=0).astype(BF16)
            y[b, q] = z[b, q][C:2 * C] + _dot(o["gram_b"][C:2 * C], uv_st)
        return [jnp.concatenate([y[b, q] for q in range(n_q)], axis=1) for b in range(nb)]

    per_seq = tcb // C
    hq = [[h_ref[b * n_q + q] for q in range(n_q)] for b in range(nb)]
    y_rows = [None] * n_ch
    for s_ in range(per_seq):
        ys = recur([b * per_seq + s_ for b in range(nb)], hq)
        for b in range(nb):
            y_rows[b * per_seq + s_] = ys[b]
    for b in range(nb):
        for q in range(n_q):
            h_ref[b * n_q + q] = hq[b][q]
    y = jnp.concatenate(y_rows, axis=0)

    mu = headsum(y) * (1.0 / N)
    yc = y - mu
    var = headsum(yc * yc) * (1.0 / N)
    yn = yc * lax.rsqrt(var + LNX_EPS) * lnx_g + lnx_b
    bonus = headsum(r * k * r_k) * v
    y_ref[...] = ((yn + bonus) * g).astype(BF16).reshape(nb, tcb, W)


def _rwkv_call(rw, vfirst_src, vec, w2p, a2p, g2p, v2p):
    B, S, _ = rw.shape
    has_vres = vfirst_src is not None
    nb = min(RW_SEQS, B)
    tc = RW_TILE // nb
    const = lambda b, j: (0, 0)
    in_specs = [pl.BlockSpec((nb, tc, RW_PAD), lambda b, j: (b, j, 0))]
    args = [rw]
    if has_vres:
        in_specs.append(pl.BlockSpec((nb, tc, RWKV_WIDTH), lambda b, j: (b, j, RW_V // RWKV_WIDTH)))
        args.append(vfirst_src)
    in_specs += [pl.BlockSpec((8, RWKV_WIDTH), const),
                 pl.BlockSpec((LANES, RWKV_WIDTH), const),
                 pl.BlockSpec((LANES, RWKV_WIDTH), const),
                 pl.BlockSpec((2 * LANES, RWKV_WIDTH), const)]
    args += [vec, w2p, a2p, g2p]
    if has_vres:
        in_specs.append(pl.BlockSpec((2 * LANES, RWKV_WIDTH), const))
        args.append(v2p)
    return pl.pallas_call(
        functools.partial(_rwkv_kernel, has_vres=has_vres),
        grid=(B // nb, S // tc),
        in_specs=in_specs,
        out_specs=pl.BlockSpec((nb, tc, RWKV_WIDTH), lambda b, j: (b, j, 0)),
        out_shape=jax.ShapeDtypeStruct((B, S, RWKV_WIDTH), BF16),
        scratch_shapes=[pltpu.VMEM((nb * (RWKV_WIDTH // QUAD), QUAD, QUAD), F32)],
        compiler_params=pltpu.CompilerParams(
            dimension_semantics=("arbitrary", "arbitrary"), vmem_limit_bytes=VMEM_LIMIT),
    )(*args)


def _part(t, op):
    r, n = t.shape
    return op(t.reshape(r // PART_ROWS, PART_ROWS, n), axis=0)


def _sweep(n, body, init):
    def pair(j, c):
        return body(2 * j + 1, body(2 * j, c))
    c = lax.fori_loop(0, n // 2, pair, init)
    return lax.cond(n % 2 == 1, lambda c: body(n - 1, c), lambda c: c, c)


def _dsa_kernel(cq_ref, kwq_ref, kw_ref, ckv_ref, wqt_ref, wk_ref, wqit_ref, wvt_ref, y_ref,
                kidx_s, ckvt_s, sc_s, bias_s, att_s, yt_s, *, top_k):
    i = pl.program_id(1)
    nblk = kidx_s.shape[0]
    QB = Q_BLOCK
    KB = KEY_BLOCK
    HG = ATT_HEAD_GROUP

    @pl.when(i == 0)
    def _():
        for b in range(nblk):
            kidx_s[b] = kw_ref[0, b * KB:(b + 1) * KB, :].astype(BF16)
            ckvt_s[b, :KV_LORA, :] = ckv_ref[0, b * KB:(b + 1) * KB, :].astype(F32).T.astype(BF16)
            ckvt_s[b, KV_LORA:, :] = jnp.ones((ONES_ROWS, KB), BF16)

    cq_t = cq_ref[0].astype(F32).T.astype(BF16)
    q_t = _dot(wqt_ref[...], cq_t).astype(BF16)
    w_t = kwq_ref[0].T
    qi_all = jnp.concatenate([_dot(wqit_ref[h], cq_t) for h in range(IDX_HEADS)],
                             axis=1).astype(BF16)
    w_all = jnp.concatenate([w_t[IDX_DIM + h:IDX_DIM + h + 1, :] for h in range(IDX_HEADS)],
                            axis=1)

    nkb = (i * QB + QB + KB - 1) // KB
    q_chunk = (i * QB + lax.broadcasted_iota(jnp.int32, (1, QB), 1)) // CHUNK
    k_row = lax.broadcasted_iota(jnp.int32, (KB, 1), 0)
    k_eff = jnp.minimum(top_k, (q_chunk + 1) * CHUNK).astype(F32)
    inf = jnp.float32(jnp.inf)
    one = jnp.float32(1.0)
    zero = jnp.float32(0.0)

    def fin_min(t):
        return jnp.min(t, axis=0, keepdims=True)

    def fin_max(t):
        return jnp.max(t, axis=0, keepdims=True)

    def fin_sum(t):
        return jnp.sum(t, axis=0, keepdims=True)

    def score_body(kb, carry):
        lo, hi = carry
        lg = jnp.maximum(_dot(kidx_s[kb], qi_all), zero) * w_all
        acc = jnp.zeros((KB, QB), F32)
        for h in range(IDX_HEADS):
            acc = acc + lg[:, h * QB:(h + 1) * QB]
        adm = (kb * KB + k_row) // CHUNK <= q_chunk
        sc_s[kb] = jnp.where(adm, acc, -inf)
        lo = jnp.minimum(lo, _part(jnp.where(adm, acc, inf), jnp.min))
        hi = jnp.maximum(hi, _part(jnp.where(adm, acc, -inf), jnp.max))
        return lo, hi

    lo, hi = _sweep(nkb,score_body,
                           (jnp.full((PART_ROWS, QB), inf, F32), jnp.full((PART_ROWS, QB), -inf, F32)))
    lo = fin_min(lo)
    hi = fin_max(hi)

    def count_ge(t):
        def body(kb, acc):
            return acc + _part(jnp.where(sc_s[kb] >= t, one, zero), jnp.sum)
        return fin_sum(_sweep(nkb,body, jnp.zeros((PART_ROWS, QB), F32)))

    def bisect(_, carry):
        lo, hi = carry
        mid = 0.5 * lo + 0.5 * hi
        ge = count_ge(mid) >= k_eff
        return jnp.where(ge, mid, lo), jnp.where(ge, hi, mid)

    lo, hi = lax.fori_loop(0, N_BISECT, bisect, (lo, hi))

    def min_ge(t):
        def body(kb, acc):
            s = sc_s[kb]
            return jnp.minimum(acc, _part(jnp.where(s >= t, s, inf), jnp.min))
        return fin_min(_sweep(nkb,body, jnp.full((PART_ROWS, QB), inf, F32)))

    def walk_cond(st):
        return st[2] > 0

    def walk_body(st):
        m, _, _ = st

        def body(kb, carry):
            cnt, nxt = carry
            s = sc_s[kb]
            gt = s > m
            return (cnt + _part(jnp.where(gt, one, zero), jnp.sum),
                    jnp.minimum(nxt, _part(jnp.where(gt, s, inf), jnp.min)))

        cnt, nxt = _sweep(nkb,body,
                                 (jnp.zeros((PART_ROWS, QB), F32), jnp.full((PART_ROWS, QB), inf, F32)))
        n_gt = fin_sum(cnt)
        done = n_gt < k_eff
        pending = jnp.max(jnp.where(done, 0, 1))
        return jnp.where(done, m, fin_min(nxt)), n_gt, pending

    thr, n_gt, _ = lax.while_loop(
        walk_cond, walk_body, (min_ge(lo), jnp.zeros((1, QB), F32), jnp.int32(1)))
    need = k_eff - n_gt

    tri = (lax.broadcasted_iota(jnp.int32, (KB, KB), 1)
           <= lax.broadcasted_iota(jnp.int32, (KB, KB), 0)).astype(BF16)

    def mask_body(kb, carry):
        s = sc_s[kb]
        tie = jnp.where(s == thr, one, zero)
        pre = _dot(tri, tie.astype(BF16)) + carry
        sel = (s > thr) | ((s == thr) & (pre <= need))
        bias_s[kb] = jnp.where(sel, zero, jnp.float32(NEG_BIG))
        return carry + fin_sum(_part(tie, jnp.sum))

    _sweep(nkb,mask_body, jnp.zeros((1, QB), F32))

    scale = DSA_HEAD_DIM ** -0.5 * LOG2_E
    dh = DSA_HEAD_DIM
    for g in range(DSA_HEADS // HG):
        qa_all = jnp.concatenate(
            [_dot(wk_ref[g * HG + h], q_t[(g * HG + h) * dh:(g * HG + h + 1) * dh]) * scale
             for h in range(HG)], axis=1).astype(BF16)

        def att_body(kb, mx):
            kv = ckv_ref[0, pl.ds(pl.multiple_of(kb * KB, KB), KB), :]
            lg = _dot(kv, qa_all)
            b = bias_s[kb]
            parts = []
            for h in range(HG):
                att = lg[:, h * QB:(h + 1) * QB] + b
                att_s[kb, :, h * QB:(h + 1) * QB] = att
                parts.append(_part(att, jnp.max))
            return jnp.maximum(mx, jnp.concatenate(parts, axis=1))

        mx = fin_max(_sweep(nkb, att_body, jnp.full((PART_ROWS, HG * QB), NEG_BIG, F32)))

        def pv_body(kb, o):
            return o + _dot(ckvt_s[kb], jnp.exp2(att_s[kb] - mx).astype(BF16))

        o = _sweep(nkb, pv_body, jnp.zeros((KV_LORA + ONES_ROWS, HG * QB), F32))
        o = (o[:KV_LORA] / o[KV_LORA:KV_LORA + 1]).astype(BF16)
        for h in range(HG):
            hh = g * HG + h
            yt_s[hh * dh:(hh + 1) * dh, :] = _dot(wvt_ref[hh], o[:, h * QB:(h + 1) * QB])

    y_ref[0] = yt_s[...].T.astype(BF16)


def _dsa_call(cq, kw, ckv, wqt, wk, wqit, wvt):
    B, S, _ = cq.shape
    QB = Q_BLOCK
    KB = KEY_BLOCK
    nblk = S // KB
    top_k = min(TOPK_MAX, S // 4)
    c2 = lambda b, i: (0, 0)
    c3 = lambda b, i: (0, 0, 0)
    return pl.pallas_call(
        functools.partial(_dsa_kernel, top_k=top_k),
        grid=(B, S // QB),
        in_specs=[
            pl.BlockSpec((1, QB, Q_LORA), lambda b, i: (b, i, 0)),
            pl.BlockSpec((1, QB, LANES), lambda b, i: (b, i, 0)),
            pl.BlockSpec((1, S, LANES), lambda b, i: (b, 0, 0)),
            pl.BlockSpec((1, S, KV_LORA), lambda b, i: (b, 0, 0)),
            pl.BlockSpec((DSA_WIDTH, Q_LORA), c2),
            pl.BlockSpec((DSA_HEADS, KV_LORA, DSA_HEAD_DIM), c3),
            pl.BlockSpec((IDX_HEADS, LANES, Q_LORA), c3),
            pl.BlockSpec((DSA_HEADS, DSA_HEAD_DIM, KV_LORA), c3),
        ],
        out_specs=pl.BlockSpec((1, QB, DSA_WIDTH), lambda b, i: (b, i, 0)),
        out_shape=jax.ShapeDtypeStruct((B, S, DSA_WIDTH), BF16),
        scratch_shapes=[
            pltpu.VMEM((nblk, KB, LANES), BF16),
            pltpu.VMEM((nblk, KV_LORA + ONES_ROWS, KB), BF16),
            pltpu.VMEM((nblk, KB, QB), F32),
            pltpu.VMEM((nblk, KB, QB), F32),
            pltpu.VMEM((nblk, KB, ATT_HEAD_GROUP * QB), F32),
            pltpu.VMEM((DSA_WIDTH, QB), F32),
        ],
        compiler_params=pltpu.CompilerParams(
            dimension_semantics=("arbitrary", "arbitrary"), vmem_limit_bytes=VMEM_LIMIT),
    )(cq, kw, kw, ckv, wqt, wk, wqit, wvt)


def _out_ffn_kernel(x_ref, yd_ref, yr_ref, mod_ref, gpm_ref, gff_ref, gpf_ref,
                    wo_ref, wfc_ref, wdn_ref, o_ref):
    mod = mod_ref[0]
    y = _dot(yd_ref[0], wo_ref[0]) + _dot(yr_ref[0], wo_ref[1])
    x1 = x_ref[0] + mod[2:3] * _rms(y, gpm_ref[...])
    h = (_rms(x1, gff_ref[...]) * (1.0 + mod[4:5]) + mod[3:4]).astype(BF16)
    nc, fc, _ = wdn_ref.shape
    acc = jnp.zeros_like(x1)
    for c in range(nc):
        gate = _dot(h, wfc_ref[:, c * fc:(c + 1) * fc])
        up = _dot(h, wfc_ref[:, (nc + c) * fc:(nc + c + 1) * fc])
        acc = acc + _dot((gate * _sigmoid(gate) * up).astype(BF16), wdn_ref[c])
    o_ref[0] = x1 + mod[5:6] * _rms(acc, gpf_ref[...])


def _out_ffn_call(x, yd, yr, mod, gpm, gff, gpf, wo, wfc, wdn, tm):
    B, S, D = x.shape
    nc, fc, _ = wdn.shape
    tok = lambda b, j: (b, j, 0)
    c2 = lambda b, j: (0, 0)
    c3 = lambda b, j: (0, 0, 0)
    return pl.pallas_call(
        _out_ffn_kernel,
        grid=(B, S // tm),
        in_specs=[
            pl.BlockSpec((1, tm, D), tok),
            pl.BlockSpec((1, tm, DSA_WIDTH), tok),
            pl.BlockSpec((1, tm, RWKV_WIDTH), tok),
            pl.BlockSpec((1, 6, D), lambda b, j: (b, 0, 0)),
            pl.BlockSpec((1, D), c2),
            pl.BlockSpec((1, D), c2),
            pl.BlockSpec((1, D), c2),
            pl.BlockSpec((2, DSA_WIDTH, D), c3, pipeline_mode=pl.Buffered(1)),
            pl.BlockSpec((D, 2 * nc * fc), c2, pipeline_mode=pl.Buffered(1)),
            pl.BlockSpec((nc, fc, D), c3, pipeline_mode=pl.Buffered(1)),
        ],
        out_specs=pl.BlockSpec((1, tm, D), tok),
        out_shape=jax.ShapeDtypeStruct((B, S, D), F32),
        compiler_params=pltpu.CompilerParams(
            dimension_semantics=("arbitrary", "arbitrary"), vmem_limit_bytes=VMEM_LIMIT),
    )(x, yd, yr, mod, gpm, gff, gpf, wo, wfc, wdn)


def _pad_rows(w, rows, at):
    out = jnp.zeros((rows,) + w.shape[1:], w.dtype)
    return out.at[at:at + w.shape[0]].set(w)


def kernel(x, c, ada_w, ada_b, pre_g_mix, post_g_mix, pre_g_ffn, post_g_ffn, w_in, w_in_vres, mu_shift, mu_vres, w_out, q_norm_g, kv_norm_g, w_q_up, w_qi_up, w_k_up, w_v_up, kidx_ln_g, kidx_ln_b, w0, w2, a0, a2, g2, v0, v2, k_k, k_a, r_k, lnx_g, lnx_b, w_fc, w_down):
    B, S, D = x.shape
    L = w_in.shape[0]
    n_dsa = Q_LORA + KV_LORA + IDX_DIM + IDX_HEADS
    tm = min(512, S)
    ff_chunk = D_FF // 2

    mod_all = _mod_call(c, ada_w, ada_b).reshape(L, B, 6, D)

    row = lambda t: t.reshape(1, -1)
    lane_pad = lambda t: jnp.pad(t, (0, LANES - t.shape[0])).reshape(1, LANES)

    v_first_src = None
    for l in range(L):
        wl = w_in[l]
        dsa_w, rw_w = wl[:, :n_dsa], wl[:, n_dsa:]
        if l == 0:
            mv_w = jnp.zeros((D, MV_LORA), F32)
            mv_mu = jnp.zeros((MV_LORA,), F32)
        else:
            mv_w, mv_mu = w_in_vres[l - 1], mu_vres[l - 1]
        zpad = lambda n: jnp.zeros((D, n), F32)
        w_all = jnp.concatenate(
            [rw_w, mv_w, zpad(RW_PAD - rw_w.shape[1] - MV_LORA),
             dsa_w, zpad(P_PAD - COL_KW - IDX_DIM - IDX_HEADS)], axis=1).astype(BF16)
        mu_all = jnp.concatenate(
            [mu_shift[l], mv_mu, jnp.zeros((RW_PAD - rw_w.shape[1] - MV_LORA,), F32)]).reshape(1, RW_PAD)

        rw, cq, ckv, kw = _inproj_call(
            x, mod_all[l], row(pre_g_mix[l]), w_all, mu_all, row(q_norm_g[l]), row(kv_norm_g[l]),
            lane_pad(kidx_ln_g[l]), lane_pad(kidx_ln_b[l]), tm)

        vec = jnp.stack([w0[l], a0[l], v0[l - 1] if l else jnp.zeros_like(w0[l]), k_k[l], k_a[l],
                         r_k[l].reshape(-1), lnx_g[l], lnx_b[l]])
        w2p = _pad_rows(w2[l], LANES, 0)
        a2p = _pad_rows(a2[l], LANES, DECAY_LORA).astype(BF16)
        g2p = _pad_rows(g2[l], 2 * LANES, 0).astype(BF16)
        v2p = _pad_rows(v2[l - 1], 2 * LANES, GATE_LORA).astype(BF16) if l else None
        y_rwkv = _rwkv_call(rw, v_first_src, vec, w2p, a2p, g2p, v2p)
        if l == 0:
            v_first_src = rw

        wqt = w_q_up[l].reshape(Q_LORA, DSA_WIDTH).T.astype(BF16)
        wk = jnp.transpose(w_k_up[l], (1, 0, 2)).astype(BF16)
        wqit = jnp.pad(jnp.transpose(w_qi_up[l], (1, 2, 0)),
                       ((0, 0), (0, LANES - IDX_DIM), (0, 0))).astype(BF16)
        wvt = jnp.transpose(w_v_up[l], (1, 2, 0)).astype(BF16)
        y_dsa = _dsa_call(cq, kw, ckv, wqt, wk, wqit, wvt)

        wo = w_out[l].reshape(2, DSA_WIDTH, D).astype(BF16)
        wfc = w_fc[l].astype(BF16)
        wdn = w_down[l].reshape(D_FF // ff_chunk, ff_chunk, D).astype(BF16)
        x = _out_ffn_call(x, y_dsa, y_rwkv, mod_all[l], row(post_g_mix[l]), row(pre_g_ffn[l]),
                          row(post_g_ffn[l]), wo, wfc, wdn, tm)
    return x
```

```python
import functools

import jax
import jax.numpy as jnp
from jax import lax
from jax.experimental import pallas as pl
from jax.experimental.pallas import tpu as pltpu

F32 = jnp.float32
BF16 = jnp.bfloat16

D_MODEL = 1024
CHUNK = 64
Q_BLOCK = 128
DSA_HEADS = 8
DSA_HEAD_DIM = 64
DSA_WIDTH = DSA_HEADS * DSA_HEAD_DIM
Q_LORA = 256
KV_LORA = 128
IDX_HEADS = 4
IDX_DIM = 64
TOPK_MAX = 256
RWKV_HEADS = 8
RWKV_HEAD_DIM = 64
RWKV_WIDTH = RWKV_HEADS * RWKV_HEAD_DIM
DECAY_LORA = 64
AAA_LORA = 64
MV_LORA = 32
GATE_LORA = 160
D_FF = ((8 * D_MODEL + 3 * 256 - 1) // (3 * 256)) * 256
NORM_EPS = 1e-6
LNX_EPS = 64e-5

LANES = 128
VMEM_LIMIT = 56 * 1024 * 1024

RW_R, RW_K, RW_V = 0, RWKV_WIDTH, 2 * RWKV_WIDTH
RW_WA = 3 * RWKV_WIDTH
RW_GM = RW_WA + DECAY_LORA + AAA_LORA
RW_PAD = RW_GM + 2 * LANES
COL_CQ = RW_PAD
COL_CKV = COL_CQ + Q_LORA
COL_KW = COL_CKV + KV_LORA
P_PAD = COL_KW + LANES

RW_CHUNK = 64
RW_TILE = 256
RW_SEQS = 4
QUAD = 4 * RWKV_HEAD_DIM
N_BISECT = 22
KEY_BLOCK = 256
DSA_Q_TILE = 256
ATT_HEAD_GROUP = 8
PART_ROWS = 32
ONES_ROWS = 16
LOG2_E = 1.4426950408889634
NEG_BIG = -1e30


def _dot(a, b):
    return jnp.dot(a, b, preferred_element_type=F32)


def _dot_nt(a, b):
    return lax.dot_general(a, b, (((1,), (1,)), ((), ())), preferred_element_type=F32)


def _split(a):
    hi = a.astype(BF16)
    lo = (a - hi.astype(F32)).astype(BF16)
    return hi, lo


def _dot3(a, b):
    ah, al = _split(a)
    bh, bl = _split(b)
    return _dot(ah, bh) + _dot(al, bh) + _dot(ah, bl)


def _rms(x, g):
    return x * lax.rsqrt(jnp.mean(x * x, axis=-1, keepdims=True) + NORM_EPS) * g


def _sigmoid(x):
    return 1.0 / (1.0 + jnp.exp(-x))


def _mod_kernel(c_ref, w_ref, b_ref, o_ref):
    c = c_ref[...]
    cond = c * _sigmoid(c)
    o_ref[0] = _dot3(cond, w_ref[0]) + b_ref[0]


def _mod_call(c, ada_w, ada_b):
    L, D, D6 = ada_w.shape
    B = c.shape[0]
    nj = D6 // D
    return pl.pallas_call(
        _mod_kernel,
        grid=(L, nj),
        in_specs=[
            pl.BlockSpec((B, D), lambda l, j: (0, 0)),
            pl.BlockSpec((1, D, D), lambda l, j: (l, 0, j)),
            pl.BlockSpec((1, 1, D), lambda l, j: (l, 0, j)),
        ],
        out_specs=pl.BlockSpec((1, B, D), lambda l, j: (l, 0, j)),
        out_shape=jax.ShapeDtypeStruct((L, B, D6), F32),
        compiler_params=pltpu.CompilerParams(vmem_limit_bytes=VMEM_LIMIT),
    )(c, ada_w, ada_b.reshape(L, 1, D6))


def _inproj_kernel(x_ref, mod_ref, g_ref, w_ref, mu_ref, qg_ref, kvg_ref, lng_ref, lnb_ref,
                   rw_ref, cq_ref, ckv_ref, kw_ref, carry_ref):
    j = pl.program_id(1)
    tm = x_ref.shape[1]
    mod = mod_ref[0]
    h = _rms(x_ref[0], g_ref[...]) * (1.0 + mod[1:2]) + mod[0:1]
    cols = _dot(h.astype(BF16), w_ref[...])

    @pl.when(j == 0)
    def _():
        carry_ref[...] = jnp.zeros_like(carry_ref)

    u = cols[:, :RW_PAD]
    row = lax.broadcasted_iota(jnp.int32, (tm, 1), 0)
    prev = jnp.where(row == 0, carry_ref[0:1, :], pltpu.roll(u, 1, axis=0))
    carry_ref[0:1, :] = u[tm - 1:tm, :]
    rw_ref[0] = u + (prev - u) * mu_ref[...]

    cq_ref[0] = _rms(cols[:, COL_CQ:COL_CKV], qg_ref[...]).astype(BF16)
    ckv_ref[0] = _rms(cols[:, COL_CKV:COL_KW], kvg_ref[...]).astype(BF16)

    kw = cols[:, COL_KW:P_PAD]
    lane = lax.broadcasted_iota(jnp.int32, (1, LANES), 1)
    is_k = lane < IDX_DIM
    mean = jnp.sum(jnp.where(is_k, kw, 0.0), axis=-1, keepdims=True) * (1.0 / IDX_DIM)
    cen = jnp.where(is_k, kw - mean, 0.0)
    var = jnp.sum(cen * cen, axis=-1, keepdims=True) * (1.0 / IDX_DIM)
    kn = cen * lax.rsqrt(var + NORM_EPS) * lng_ref[...] + lnb_ref[...]
    kw_ref[0] = jnp.where(is_k, kn, kw * ((IDX_HEADS ** -0.5) * (IDX_DIM ** -0.5)))


def _inproj_call(x, mod, g, w, mu, qg, kvg, lng, lnb, tm):
    B, S, D = x.shape
    const = lambda b, j: (0, 0)
    tok = lambda b, j: (b, j, 0)
    return pl.pallas_call(
        _inproj_kernel,
        grid=(B, S // tm),
        in_specs=[
            pl.BlockSpec((1, tm, D), tok),
            pl.BlockSpec((1, 6, D), lambda b, j: (b, 0, 0)),
            pl.BlockSpec((1, D), const),
            pl.BlockSpec((D, P_PAD), const, pipeline_mode=pl.Buffered(1)),
            pl.BlockSpec((1, RW_PAD), const),
            pl.BlockSpec((1, Q_LORA), const),
            pl.BlockSpec((1, KV_LORA), const),
            pl.BlockSpec((1, LANES), const),
            pl.BlockSpec((1, LANES), const),
        ],
        out_specs=[
            pl.BlockSpec((1, tm, RW_PAD), tok),
            pl.BlockSpec((1, tm, Q_LORA), tok),
            pl.BlockSpec((1, tm, KV_LORA), tok),
            pl.BlockSpec((1, tm, LANES), tok),
        ],
        out_shape=[
            jax.ShapeDtypeStruct((B, S, RW_PAD), F32),
            jax.ShapeDtypeStruct((B, S, Q_LORA), BF16),
            jax.ShapeDtypeStruct((B, S, KV_LORA), BF16),
            jax.ShapeDtypeStruct((B, S, LANES), F32),
        ],
        scratch_shapes=[pltpu.VMEM((8, RW_PAD), F32)],
        compiler_params=pltpu.CompilerParams(
            dimension_semantics=("arbitrary", "arbitrary"), vmem_limit_bytes=VMEM_LIMIT),
    )(x, mod, g, w, mu, qg, kvg, lng, lnb)


def _softplus(z):
    return jnp.maximum(z, 0.0) + jnp.log(1.0 + jnp.exp(-jnp.abs(z)))


def _rwkv_kernel(*refs, has_vres):
    if has_vres:
        rw_ref, vf_ref, vec_ref, w2_ref, a2_ref, g2_ref, v2_ref, y_ref, h_ref = refs
    else:
        rw_ref, vec_ref, w2_ref, a2_ref, g2_ref, y_ref, h_ref = refs
    W = RWKV_WIDTH
    N = RWKV_HEAD_DIM
    C = RW_CHUNK
    nb, tcb = rw_ref.shape[0], rw_ref.shape[1]
    tc = nb * tcb

    @pl.when(pl.program_id(1) == 0)
    def _():
        h_ref[...] = jnp.zeros_like(h_ref)

    vec = vec_ref[...]
    w0, a0, v0, k_k, k_a, r_k, lnx_g, lnx_b = [vec[i:i + 1] for i in range(8)]

    def rows(ref, lo_, hi_):
        return ref[:, :, lo_:hi_].reshape(tc, hi_ - lo_)

    r = rows(rw_ref, RW_R, RW_R + W)
    k = rows(rw_ref, RW_K, RW_K + W)
    v = rows(rw_ref, RW_V, RW_V + W)
    wa = rows(rw_ref, RW_WA, RW_GM)
    gm = rows(rw_ref, RW_GM, RW_PAD)

    bd = (lax.broadcasted_iota(jnp.int32, (QUAD, QUAD), 0) // N
          == lax.broadcasted_iota(jnp.int32, (QUAD, QUAD), 1) // N)
    head_ones = bd.astype(BF16)

    def headsum(t):
        th = t.astype(BF16)
        tl = (t - th.astype(F32)).astype(BF16)
        return jnp.concatenate(
            [_dot(th[:, q * QUAD:(q + 1) * QUAD], head_ones)
             + _dot(tl[:, q * QUAD:(q + 1) * QUAD], head_ones) for q in range(W // QUAD)], axis=1)

    w_log = -_softplus(-(w0 + _dot3(jnp.tanh(wa), w2_ref[...]))) - 0.5
    logw = -jnp.exp(w_log)
    a = _sigmoid(a0 + _dot(wa.astype(BF16), a2_ref[...]))
    g = _dot(_sigmoid(gm).astype(BF16), g2_ref[...])
    if has_vres:
        v = v + (rows(vf_ref, 0, W) - v) * _sigmoid(v0 + _dot(gm.astype(BF16), v2_ref[...]))
    kk = k * k_k
    kk = kk / jnp.maximum(jnp.sqrt(headsum(kk * kk)), 1e-12)
    k = k * (1.0 + (a - 1.0) * k_a)

    ti = lax.broadcasted_iota(jnp.int32, (tc, tc), 0)
    tj = lax.broadcasted_iota(jnp.int32, (tc, tc), 1)
    tri = ((ti // C == tj // C) & (tj <= ti)).astype(BF16)
    lh = logw.astype(BF16)
    l1 = logw - lh.astype(F32)
    lm = l1.astype(BF16)
    ll = (l1 - lm.astype(F32)).astype(BF16)
    cum = _dot(tri, lh) + _dot(tri, lm) + _dot(tri, ll)

    bt = kk * jnp.exp(cum - logw)
    at = -(a * kk) * jnp.exp(-cum)
    kt = k * jnp.exp(-cum)
    rt = r * jnp.exp(cum)

    lane_q = lax.broadcasted_iota(jnp.int32, (1, QUAD), 1) // N
    hmask = [lane_q == h for h in range(4)]
    gr = lax.broadcasted_iota(jnp.int32, (2 * C, 2 * QUAD), 0)
    gc = lax.broadcasted_iota(jnp.int32, (2 * C, 2 * QUAD), 1) % C
    gmask = ((gr < C) & (gc < gr)) | ((gr >= C) & (gc <= gr - C))
    pr = lax.broadcasted_iota(jnp.int32, (C, LANES), 0)
    pc = lax.broadcasted_iota(jnp.int32, (C, LANES), 1)
    eye_pad = (pr == pc).astype(F32)
    left = pc < C
    zeros_cq = jnp.zeros((C, QUAD), F32)
    n_ch = tc // C
    n_q = W // QUAD

    cells = [(c, q) for c in range(n_ch) for q in range(n_q)]

    cell = {}
    for c, q in cells:
        rs = slice(c * C, (c + 1) * C)
        cs = slice(q * QUAD, (q + 1) * QUAD)
        tot = cum[c * C + C - 1:c * C + C, cs]
        tail = jnp.exp(tot - cum[rs, cs])
        atail = -(a[rs, cs] * kk[rs, cs]) * tail
        ktail = k[rs, cs] * tail
        dp_t = jnp.broadcast_to(jnp.exp(tot), (LANES, QUAD)).T
        vq = v[rs, cs]
        lhs = jnp.concatenate([bt[rs, cs], rt[rs, cs]], axis=0).astype(BF16)
        rhs_t = jnp.concatenate(
            [jnp.where(hmask[h], t, 0.0) for h in range(4) for t in (kt[rs, cs], at[rs, cs])],
            axis=0).astype(BF16)
        gram = jnp.where(gmask, _dot_nt(lhs, rhs_t), 0.0)
        cell[c, q] = dict(
            ak_t=jnp.concatenate([atail, ktail], axis=0).T.astype(BF16),
            dp_col=jnp.concatenate([dp_t, dp_t], axis=1), vq=vq, lhs=lhs, gram=gram,
            gram_b=gram.astype(BF16),
            v_masked=[jnp.where(hmask[h], vq, 0.0) for h in range(4)])

    pairs = [(c, q, j) for c, q in cells for j in range(2)]
    st = {}
    for c, q, j in pairs:
        for h in (2 * j, 2 * j + 1):
            st[c, q, h] = jnp.where(left, eye_pad, cell[c, q]["gram"][0:C, h * LANES:(h + 1) * LANES])
    zeros_pad = jnp.zeros((C, LANES), F32)
    for _ in range(6):
        for c, q, j in pairs:
            s0, s1 = st[c, q, 2 * j], st[c, q, 2 * j + 1]
            lhs2 = jnp.where(left, pltpu.roll(s0, C, axis=1), s1).astype(BF16)
            rhs2 = jnp.concatenate(
                [jnp.concatenate([s0, zeros_pad], axis=1),
                 jnp.concatenate([zeros_pad, s1], axis=1)], axis=0).astype(BF16)
            res = _dot(lhs2, rhs2)
            st[c, q, 2 * j] = jnp.where(left, s0 + res[:, :LANES], res[:, :LANES])
            st[c, q, 2 * j + 1] = jnp.where(left, s1 + res[:, LANES:], res[:, LANES:])

    for c, q in cells:
        o = cell[c, q]
        o["t_cat"] = jnp.concatenate(
            [jnp.where(left, st[c, q, h], 0.0) for h in range(4)], axis=1).astype(BF16)
        vs0 = jnp.concatenate(
            [t for h in range(4) for t in (o["v_masked"][h], zeros_cq)], axis=0).astype(BF16)
        o["akv"] = _dot(o["gram_b"][0:C], vs0)

    def recur(chunks, hq):
        keys = [(b, q) for b in range(nb) for q in range(n_q)]
        z, u, y = {}, {}, {}
        for b, q in keys:
            z[b, q] = _dot(cell[chunks[b], q]["lhs"], hq[b][q].astype(BF16))
        for b, q in keys:
            o = cell[chunks[b], q]
            xs = z[b, q][0:C] + o["akv"]
            xs_st = jnp.concatenate(
                [t for h in range(4) for t in (jnp.where(hmask[h], xs, 0.0), zeros_cq)],
                axis=0).astype(BF16)
            u[b, q] = _dot(o["t_cat"], xs_st)
        for b, q in keys:
            o = cell[chunks[b], q]
            uv = jnp.concatenate([u[b, q], o["vq"]], axis=0).astype(BF16)
            hq[b][q] = o["dp_col"] * hq[b][q] + jnp.where(bd, _dot(o["ak_t"], uv), 0.0)
        for b, q in keys:
            o = cell[chunks[b], q]
            uv_st = jnp.concatenate(
                [t for h in range(4) for t in (o["v_masked"][h], jnp.where(hmask[h], u[b, q], 0.0))],
                axis=0).astype(BF16)
            y[b, q] = z[b, q][C:2 * C] + _dot(o["gram_b"][C:2 * C], uv_st)
        return [jnp.concatenate([y[b, q] for q in range(n_q)], axis=1) for b in range(nb)]

    per_seq = tcb // C
    hq = [[h_ref[b * n_q + q] for q in range(n_q)] for b in range(nb)]
    y_rows = [None] * n_ch
    for s_ in range(per_seq):
        ys = recur([b * per_seq + s_ for b in range(nb)], hq)
        for b in range(nb):
            y_rows[b * per_seq + s_] = ys[b]
    for b in range(nb):
        for q in range(n_q):
            h_ref[b * n_q + q] = hq[b][q]
    y = jnp.concatenate(y_rows, axis=0)

    mu = headsum(y) * (1.0 / N)
    yc = y - mu
    var = headsum(yc * yc) * (1.0 / N)
    yn = yc * lax.rsqrt(var + LNX_EPS) * lnx_g + lnx_b
    bonus = headsum(r * k * r_k) * v
    y_ref[...] = ((yn + bonus) * g).astype(BF16).reshape(nb, tcb, W)


def _rwkv_call(rw, vfirst_src, vec, w2p, a2p, g2p, v2p):
    B, S, _ = rw.shape
    has_vres = vfirst_src is not None
    nb = min(RW_SEQS, B)
    tc = RW_TILE // nb
    const = lambda b, j: (0, 0)
    in_specs = [pl.BlockSpec((nb, tc, RW_PAD), lambda b, j: (b, j, 0))]
    args = [rw]
    if has_vres:
        in_specs.append(pl.BlockSpec((nb, tc, RWKV_WIDTH), lambda b, j: (b, j, RW_V // RWKV_WIDTH)))
        args.append(vfirst_src)
    in_specs += [pl.BlockSpec((8, RWKV_WIDTH), const),
                 pl.BlockSpec((LANES, RWKV_WIDTH), const),
                 pl.BlockSpec((LANES, RWKV_WIDTH), const),
                 pl.BlockSpec((2 * LANES, RWKV_WIDTH), const)]
    args += [vec, w2p, a2p, g2p]
    if has_vres:
        in_specs.append(pl.BlockSpec((2 * LANES, RWKV_WIDTH), const))
        args.append(v2p)
    return pl.pallas_call(
        functools.partial(_rwkv_kernel, has_vres=has_vres),
        grid=(B // nb, S // tc),
        in_specs=in_specs,
        out_specs=pl.BlockSpec((nb, tc, RWKV_WIDTH), lambda b, j: (b, j, 0)),
        out_shape=jax.ShapeDtypeStruct((B, S, RWKV_WIDTH), BF16),
        scratch_shapes=[pltpu.VMEM((nb * (RWKV_WIDTH // QUAD), QUAD, QUAD), F32)],
        compiler_params=pltpu.CompilerParams(
            dimension_semantics=("arbitrary", "arbitrary"), vmem_limit_bytes=VMEM_LIMIT),
    )(*args)


def _part(t, op):
    r, n = t.shape
    return op(t.reshape(r // PART_ROWS, PART_ROWS, n), axis=0)


def _sweep(n, body, init, pair=None):
    if pair is None:
        def pair(j, c):
            return body(2 * j + 1, body(2 * j, c))
    c = lax.fori_loop(0, n // 2, pair, init)
    return lax.cond(n % 2 == 1, lambda c: body(n - 1, c), lambda c: c, c)


def _dsa_kernel(cq_ref, kwq_ref, kw_ref, ckv_ref, wqt_ref, wk_ref, wqit_ref, wvt_ref, y_ref,
                kidx_s, ckvt_s, sc_s, bias_s, att_s, yt_s, *, top_k):
    i = pl.program_id(1)
    nblk = kidx_s.shape[0]
    QB = DSA_Q_TILE
    KB = KEY_BLOCK
    HG = ATT_HEAD_GROUP

    @pl.when(i == 0)
    def _():
        for b in range(nblk):
            kidx_s[b] = kw_ref[0, b * KB:(b + 1) * KB, :].astype(BF16)
            ckvt_s[b, :KV_LORA, :] = ckv_ref[0, b * KB:(b + 1) * KB, :].astype(F32).T.astype(BF16)
            ckvt_s[b, KV_LORA:, :] = jnp.ones((ONES_ROWS, KB), BF16)

    cq_t = cq_ref[0].astype(F32).T.astype(BF16)
    q_t = _dot(wqt_ref[...], cq_t).astype(BF16)
    w_t = kwq_ref[0].T
    qi_all = jnp.concatenate([_dot(wqit_ref[h], cq_t) for h in range(IDX_HEADS)],
                             axis=1).astype(BF16)
    w_all = jnp.concatenate([w_t[IDX_DIM + h:IDX_DIM + h + 1, :] for h in range(IDX_HEADS)],
                            axis=1)

    nkb = (i * QB + QB + KB - 1) // KB
    q_chunk = (i * QB + lax.broadcasted_iota(jnp.int32, (1, QB), 1)) // CHUNK
    k_row = lax.broadcasted_iota(jnp.int32, (KB, 1), 0)
    k_eff = jnp.minimum(top_k, (q_chunk + 1) * CHUNK).astype(F32)
    inf = jnp.float32(jnp.inf)
    one = jnp.float32(1.0)
    zero = jnp.float32(0.0)

    def fin_min(t):
        return jnp.min(t, axis=0, keepdims=True)

    def fin_max(t):
        return jnp.max(t, axis=0, keepdims=True)

    def fin_sum(t):
        return jnp.sum(t, axis=0, keepdims=True)

    def score_body(kb, carry):
        lo, hi = carry
        lg = jnp.maximum(_dot(kidx_s[kb], qi_all), zero) * w_all
        acc = jnp.zeros((KB, QB), F32)
        for h in range(IDX_HEADS):
            acc = acc + lg[:, h * QB:(h + 1) * QB]
        adm = (kb * KB + k_row) // CHUNK <= q_chunk
        sc_s[kb] = jnp.where(adm, acc, -inf)
        lo = jnp.minimum(lo, _part(jnp.where(adm, acc, inf), jnp.min))
        hi = jnp.maximum(hi, _part(jnp.where(adm, acc, -inf), jnp.max))
        return lo, hi

    lo, hi = _sweep(nkb,score_body,
                           (jnp.full((PART_ROWS, QB), inf, F32), jnp.full((PART_ROWS, QB), -inf, F32)))
    lo = fin_min(lo)
    hi = fin_max(hi)

    def count_ge(t):
        def body(kb, acc):
            return acc + _part(jnp.where(sc_s[kb] >= t, one, zero), jnp.sum)
        return fin_sum(_sweep(nkb,body, jnp.zeros((PART_ROWS, QB), F32)))

    def bisect(_, carry):
        lo, hi = carry
        mid = 0.5 * lo + 0.5 * hi
        ge = count_ge(mid) >= k_eff
        return jnp.where(ge, mid, lo), jnp.where(ge, hi, mid)

    lo, hi = lax.fori_loop(0, N_BISECT, bisect, (lo, hi))

    def min_ge(t):
        def body(kb, acc):
            s = sc_s[kb]
            return jnp.minimum(acc, _part(jnp.where(s >= t, s, inf), jnp.min))
        return fin_min(_sweep(nkb,body, jnp.full((PART_ROWS, QB), inf, F32)))

    def walk_cond(st):
        return st[2] > 0

    def walk_body(st):
        m, _, _ = st

        def body(kb, carry):
            cnt, nxt = carry
            s = sc_s[kb]
            gt = s > m
            return (cnt + _part(jnp.where(gt, one, zero), jnp.sum),
                    jnp.minimum(nxt, _part(jnp.where(gt, s, inf), jnp.min)))

        cnt, nxt = _sweep(nkb,body,
                                 (jnp.zeros((PART_ROWS, QB), F32), jnp.full((PART_ROWS, QB), inf, F32)))
        n_gt = fin_sum(cnt)
        done = n_gt < k_eff
        pending = jnp.max(jnp.where(done, 0, 1))
        return jnp.where(done, m, fin_min(nxt)), n_gt, pending

    thr, n_gt, _ = lax.while_loop(
        walk_cond, walk_body, (min_ge(lo), jnp.zeros((1, QB), F32), jnp.int32(1)))
    need = k_eff - n_gt

    tri = (lax.broadcasted_iota(jnp.int32, (KB, KB), 1)
           <= lax.broadcasted_iota(jnp.int32, (KB, KB), 0)).astype(BF16)

    def mask_body(kb, carry):
        s = sc_s[kb]
        tie = jnp.where(s == thr, one, zero)
        pre = _dot(tri, tie.astype(BF16)) + carry
        sel = (s > thr) | ((s == thr) & (pre <= need))
        bias_s[kb] = jnp.where(sel, zero, jnp.float32(NEG_BIG))
        return carry + fin_sum(_part(tie, jnp.sum))

    _sweep(nkb,mask_body, jnp.zeros((1, QB), F32))

    scale = DSA_HEAD_DIM ** -0.5 * LOG2_E
    dh = DSA_HEAD_DIM
    for g in range(DSA_HEADS // HG):
        qa_all = jnp.concatenate(
            [_dot(wk_ref[g * HG + h], q_t[(g * HG + h) * dh:(g * HG + h + 1) * dh]) * scale
             for h in range(HG)], axis=1).astype(BF16)

        def att_body(kb, mx):
            kv = ckv_ref[0, pl.ds(pl.multiple_of(kb * KB, KB), KB), :]
            lg = _dot(kv, qa_all)
            b = bias_s[kb]
            parts = []
            for h in range(HG):
                att = lg[:, h * QB:(h + 1) * QB] + b
                att_s[kb, :, h * QB:(h + 1) * QB] = att
                parts.append(_part(att, jnp.max))
            return jnp.maximum(mx, jnp.concatenate(parts, axis=1))

        mx = fin_max(_sweep(nkb, att_body, jnp.full((PART_ROWS, HG * QB), NEG_BIG, F32)))

        def prob(kb):
            return jnp.exp2(att_s[kb] - mx).astype(BF16)

        def pv_body(kb, o):
            return o + _dot(ckvt_s[kb], prob(kb))

        def pv_pair(j, o):
            return o + _dot(jnp.concatenate([ckvt_s[2 * j], ckvt_s[2 * j + 1]], axis=1),
                            jnp.concatenate([prob(2 * j), prob(2 * j + 1)], axis=0))

        o = _sweep(nkb, pv_body, jnp.zeros((KV_LORA + ONES_ROWS, HG * QB), F32), pv_pair)
        o = (o[:KV_LORA] / o[KV_LORA:KV_LORA + 1]).astype(BF16)
        for h in range(HG):
            hh = g * HG + h
            yt_s[hh * dh:(hh + 1) * dh, :] = _dot(wvt_ref[hh], o[:, h * QB:(h + 1) * QB])

    y_ref[0] = yt_s[...].T.astype(BF16)


def _dsa_call(cq, kw, ckv, wqt, wk, wqit, wvt):
    B, S, _ = cq.shape
    QB = DSA_Q_TILE
    KB = KEY_BLOCK
    nblk = S // KB
    top_k = min(TOPK_MAX, S // 4)
    c2 = lambda b, i: (0, 0)
    c3 = lambda b, i: (0, 0, 0)
    return pl.pallas_call(
        functools.partial(_dsa_kernel, top_k=top_k),
        grid=(B, S // QB),
        in_specs=[
            pl.BlockSpec((1, QB, Q_LORA), lambda b, i: (b, i, 0)),
            pl.BlockSpec((1, QB, LANES), lambda b, i: (b, i, 0)),
            pl.BlockSpec((1, S, LANES), lambda b, i: (b, 0, 0)),
            pl.BlockSpec((1, S, KV_LORA), lambda b, i: (b, 0, 0)),
            pl.BlockSpec((DSA_WIDTH, Q_LORA), c2),
            pl.BlockSpec((DSA_HEADS, KV_LORA, DSA_HEAD_DIM), c3),
            pl.BlockSpec((IDX_HEADS, LANES, Q_LORA), c3),
            pl.BlockSpec((DSA_HEADS, DSA_HEAD_DIM, KV_LORA), c3),
        ],
        out_specs=pl.BlockSpec((1, QB, DSA_WIDTH), lambda b, i: (b, i, 0)),
        out_shape=jax.ShapeDtypeStruct((B, S, DSA_WIDTH), BF16),
        scratch_shapes=[
            pltpu.VMEM((nblk, KB, LANES), BF16),
            pltpu.VMEM((nblk, KV_LORA + ONES_ROWS, KB), BF16),
            pltpu.VMEM((nblk, KB, QB), F32),
            pltpu.VMEM((nblk, KB, QB), F32),
            pltpu.VMEM((nblk, KB, ATT_HEAD_GROUP * QB), F32),
            pltpu.VMEM((DSA_WIDTH, QB), F32),
        ],
        compiler_params=pltpu.CompilerParams(
            dimension_semantics=("arbitrary", "arbitrary"), vmem_limit_bytes=VMEM_LIMIT),
    )(cq, kw, kw, ckv, wqt, wk, wqit, wvt)


def _out_ffn_kernel(x_ref, yd_ref, yr_ref, mod_ref, gpm_ref, gff_ref, gpf_ref,
                    wo_ref, wfc_ref, wdn_ref, o_ref):
    mod = mod_ref[0]
    y = _dot(yd_ref[0], wo_ref[0]) + _dot(yr_ref[0], wo_ref[1])
    x1 = x_ref[0] + mod[2:3] * _rms(y, gpm_ref[...])
    h = (_rms(x1, gff_ref[...]) * (1.0 + mod[4:5]) + mod[3:4]).astype(BF16)
    nc, fc, _ = wdn_ref.shape
    acc = jnp.zeros_like(x1)
    for c in range(nc):
        gate = _dot(h, wfc_ref[:, c * fc:(c + 1) * fc])
        up = _dot(h, wfc_ref[:, (nc + c) * fc:(nc + c + 1) * fc])
        acc = acc + _dot((gate * _sigmoid(gate) * up).astype(BF16), wdn_ref[c])
    o_ref[0] = x1 + mod[5:6] * _rms(acc, gpf_ref[...])


def _out_ffn_call(x, yd, yr, mod, gpm, gff, gpf, wo, wfc, wdn, tm):
    B, S, D = x.shape
    nc, fc, _ = wdn.shape
    tok = lambda b, j: (b, j, 0)
    c2 = lambda b, j: (0, 0)
    c3 = lambda b, j: (0, 0, 0)
    return pl.pallas_call(
        _out_ffn_kernel,
        grid=(B, S // tm),
        in_specs=[
            pl.BlockSpec((1, tm, D), tok),
            pl.BlockSpec((1, tm, DSA_WIDTH), tok),
            pl.BlockSpec((1, tm, RWKV_WIDTH), tok),
            pl.BlockSpec((1, 6, D), lambda b, j: (b, 0, 0)),
            pl.BlockSpec((1, D), c2),
            pl.BlockSpec((1, D), c2),
            pl.BlockSpec((1, D), c2),
            pl.BlockSpec((2, DSA_WIDTH, D), c3, pipeline_mode=pl.Buffered(1)),
            pl.BlockSpec((D, 2 * nc * fc), c2, pipeline_mode=pl.Buffered(1)),
            pl.BlockSpec((nc, fc, D), c3, pipeline_mode=pl.Buffered(1)),
        ],
        out_specs=pl.BlockSpec((1, tm, D), tok),
        out_shape=jax.ShapeDtypeStruct((B, S, D), F32),
        compiler_params=pltpu.CompilerParams(
            dimension_semantics=("arbitrary", "arbitrary"), vmem_limit_bytes=VMEM_LIMIT),
    )(x, yd, yr, mod, gpm, gff, gpf, wo, wfc, wdn)


def _pad_rows(w, rows, at):
    out = jnp.zeros((rows,) + w.shape[1:], w.dtype)
    return out.at[at:at + w.shape[0]].set(w)


def kernel(x, c, ada_w, ada_b, pre_g_mix, post_g_mix, pre_g_ffn, post_g_ffn, w_in, w_in_vres, mu_shift, mu_vres, w_out, q_norm_g, kv_norm_g, w_q_up, w_qi_up, w_k_up, w_v_up, kidx_ln_g, kidx_ln_b, w0, w2, a0, a2, g2, v0, v2, k_k, k_a, r_k, lnx_g, lnx_b, w_fc, w_down):
    B, S, D = x.shape
    L = w_in.shape[0]
    n_dsa = Q_LORA + KV_LORA + IDX_DIM + IDX_HEADS
    tm = min(512, S)
    ff_chunk = D_FF // 2

    mod_all = _mod_call(c, ada_w, ada_b).reshape(L, B, 6, D)

    row = lambda t: t.reshape(1, -1)
    lane_pad = lambda t: jnp.pad(t, (0, LANES - t.shape[0])).reshape(1, LANES)

    v_first_src = None
    for l in range(L):
        wl = w_in[l]
        dsa_w, rw_w = wl[:, :n_dsa], wl[:, n_dsa:]
        if l == 0:
            mv_w = jnp.zeros((D, MV_LORA), F32)
            mv_mu = jnp.zeros((MV_LORA,), F32)
        else:
            mv_w, mv_mu = w_in_vres[l - 1], mu_vres[l - 1]
        zpad = lambda n: jnp.zeros((D, n), F32)
        w_all = jnp.concatenate(
            [rw_w, mv_w, zpad(RW_PAD - rw_w.shape[1] - MV_LORA),
             dsa_w, zpad(P_PAD - COL_KW - IDX_DIM - IDX_HEADS)], axis=1).astype(BF16)
        mu_all = jnp.concatenate(
            [mu_shift[l], mv_mu, jnp.zeros((RW_PAD - rw_w.shape[1] - MV_LORA,), F32)]).reshape(1, RW_PAD)

        rw, cq, ckv, kw = _inproj_call(
            x, mod_all[l], row(pre_g_mix[l]), w_all, mu_all, row(q_norm_g[l]), row(kv_norm_g[l]),
            lane_pad(kidx_ln_g[l]), lane_pad(kidx_ln_b[l]), tm)

        vec = jnp.stack([w0[l], a0[l], v0[l - 1] if l else jnp.zeros_like(w0[l]), k_k[l], k_a[l],
                         r_k[l].reshape(-1), lnx_g[l], lnx_b[l]])
        w2p = _pad_rows(w2[l], LANES, 0)
        a2p = _pad_rows(a2[l], LANES, DECAY_LORA).astype(BF16)
        g2p = _pad_rows(g2[l], 2 * LANES, 0).astype(BF16)
        v2p = _pad_rows(v2[l - 1], 2 * LANES, GATE_LORA).astype(BF16) if l else None
        y_rwkv = _rwkv_call(rw, v_first_src, vec, w2p, a2p, g2p, v2p)
        if l == 0:
            v_first_src = rw

        wqt = w_q_up[l].reshape(Q_LORA, DSA_WIDTH).T.astype(BF16)
        wk = jnp.transpose(w_k_up[l], (1, 0, 2)).astype(BF16)
        wqit = jnp.pad(jnp.transpose(w_qi_up[l], (1, 2, 0)),
                       ((0, 0), (0, LANES - IDX_DIM), (0, 0))).astype(BF16)
        wvt = jnp.transpose(w_v_up[l], (1, 2, 0)).astype(BF16)
        y_dsa = _dsa_call(cq, kw, ckv, wqt, wk, wqit, wvt)

        wo = w_out[l].reshape(2, DSA_WIDTH, D).astype(BF16)
        wfc = w_fc[l].astype(BF16)
        wdn = w_down[l].reshape(D_FF // ff_chunk, ff_chunk, D).astype(BF16)
        x = _out_ffn_call(x, y_dsa, y_rwkv, mod_all[l], row(post_g_mix[l]), row(pre_g_ffn[l]),
                          row(post_g_ffn[l]), wo, wfc, wdn, tm)
    return x
```

```python
import functools

import jax
import jax.numpy as jnp
from jax import lax
from jax.experimental import pallas as pl
from jax.experimental.pallas import tpu as pltpu

F32 = jnp.float32
BF16 = jnp.bfloat16

D_MODEL = 1024
CHUNK = 64
Q_BLOCK = 128
DSA_HEADS = 8
DSA_HEAD_DIM = 64
DSA_WIDTH = DSA_HEADS * DSA_HEAD_DIM
Q_LORA = 256
KV_LORA = 128
IDX_HEADS = 4
IDX_DIM = 64
TOPK_MAX = 256
RWKV_HEADS = 8
RWKV_HEAD_DIM = 64
RWKV_WIDTH = RWKV_HEADS * RWKV_HEAD_DIM
DECAY_LORA = 64
AAA_LORA = 64
MV_LORA = 32
GATE_LORA = 160
D_FF = ((8 * D_MODEL + 3 * 256 - 1) // (3 * 256)) * 256
NORM_EPS = 1e-6
LNX_EPS = 64e-5

LANES = 128
VMEM_LIMIT = 56 * 1024 * 1024

RW_R, RW_K, RW_V = 0, RWKV_WIDTH, 2 * RWKV_WIDTH
RW_WA = 3 * RWKV_WIDTH
RW_GM = RW_WA + DECAY_LORA + AAA_LORA
RW_PAD = RW_GM + 2 * LANES
COL_CQ = RW_PAD
COL_CKV = COL_CQ + Q_LORA
COL_KW = COL_CKV + KV_LORA
P_PAD = COL_KW + LANES

INPROJ_COL_BLOCK = 512
INPROJ_ROW_BLOCK = 512
INPROJ_TILE = 1024
RW_CHUNK = 64
RW_TILE = 256
RW_SEQS = 4
QUAD = 4 * RWKV_HEAD_DIM
N_BISECT = 18
KEY_BLOCK = 256
DSA_Q_TILE = 256
ATT_HEAD_GROUP = 8
PART_ROWS = 32
ONES_ROWS = 16
LOG2_E = 1.4426950408889634
NEG_BIG = -1e30


def _dot(a, b):
    return jnp.dot(a, b, preferred_element_type=F32)


def _dot_nt(a, b):
    return lax.dot_general(a, b, (((1,), (1,)), ((), ())), preferred_element_type=F32)


def _split(a):
    hi = a.astype(BF16)
    lo = (a - hi.astype(F32)).astype(BF16)
    return hi, lo


def _dot3(a, b):
    ah, al = _split(a)
    bh, bl = _split(b)
    return _dot(ah, bh) + _dot(al, bh) + _dot(ah, bl)


def _rms(x, g):
    return x * lax.rsqrt(jnp.mean(x * x, axis=-1, keepdims=True) + NORM_EPS) * g


def _sigmoid(x):
    return 1.0 / (1.0 + jnp.exp(-x))


def _mod_kernel(c_ref, w_ref, b_ref, o_ref):
    c = c_ref[...]
    cond = c * _sigmoid(c)
    o_ref[0] = _dot3(cond, w_ref[0]) + b_ref[0]


def _mod_call(c, ada_w, ada_b):
    L, D, D6 = ada_w.shape
    B = c.shape[0]
    nj = D6 // D
    return pl.pallas_call(
        _mod_kernel,
        grid=(L, nj),
        in_specs=[
            pl.BlockSpec((B, D), lambda l, j: (0, 0)),
            pl.BlockSpec((1, D, D), lambda l, j: (l, 0, j)),
            pl.BlockSpec((1, 1, D), lambda l, j: (l, 0, j)),
        ],
        out_specs=pl.BlockSpec((1, B, D), lambda l, j: (l, 0, j)),
        out_shape=jax.ShapeDtypeStruct((L, B, D6), F32),
        compiler_params=pltpu.CompilerParams(vmem_limit_bytes=VMEM_LIMIT),
    )(c, ada_w, ada_b.reshape(L, 1, D6))


def _inproj_kernel(x_ref, mod_ref, g_ref, w_ref, mu_ref, qg_ref, kvg_ref, lng_ref, lnb_ref,
                   rw_ref, cq_ref, ckv_ref, kw_ref, carry_ref):
    j = pl.program_id(1)
    tm = x_ref.shape[1]
    rb = min(INPROJ_ROW_BLOCK, tm)
    mod = mod_ref[0]

    @pl.when(j == 0)
    def _():
        carry_ref[...] = jnp.zeros_like(carry_ref)

    row = lax.broadcasted_iota(jnp.int32, (rb, 1), 0)
    lane = lax.broadcasted_iota(jnp.int32, (1, LANES), 1)
    is_k = lane < IDX_DIM

    for r0 in range(0, tm, rb):
        rs = slice(r0, r0 + rb)
        h = (_rms(x_ref[0, rs, :], g_ref[...]) * (1.0 + mod[1:2]) + mod[0:1]).astype(BF16)

        def proj(lo_, hi_):
            return _dot(h, w_ref[:, lo_:hi_])

        for lo_ in range(0, RW_PAD, INPROJ_COL_BLOCK):
            hi_ = min(lo_ + INPROJ_COL_BLOCK, RW_PAD)
            u = proj(lo_, hi_)
            prev = jnp.where(row == 0, carry_ref[0:1, lo_:hi_], pltpu.roll(u, 1, axis=0))
            carry_ref[0:1, lo_:hi_] = u[rb - 1:rb, :]
            rw_ref[0, rs, lo_:hi_] = u + (prev - u) * mu_ref[:, lo_:hi_]

        cq_ref[0, rs, :] = _rms(proj(COL_CQ, COL_CKV), qg_ref[...]).astype(BF16)
        ckv_ref[0, rs, :] = _rms(proj(COL_CKV, COL_KW), kvg_ref[...]).astype(BF16)

        kw = proj(COL_KW, P_PAD)
        mean = jnp.sum(jnp.where(is_k, kw, 0.0), axis=-1, keepdims=True) * (1.0 / IDX_DIM)
        cen = jnp.where(is_k, kw - mean, 0.0)
        var = jnp.sum(cen * cen, axis=-1, keepdims=True) * (1.0 / IDX_DIM)
        kn = cen * lax.rsqrt(var + NORM_EPS) * lng_ref[...] + lnb_ref[...]
        kw_ref[0, rs, :] = jnp.where(is_k, kn, kw * ((IDX_HEADS ** -0.5) * (IDX_DIM ** -0.5)))


def _inproj_call(x, mod, g, w, mu, qg, kvg, lng, lnb, tm):
    B, S, D = x.shape
    const = lambda b, j: (0, 0)
    tok = lambda b, j: (b, j, 0)
    return pl.pallas_call(
        _inproj_kernel,
        grid=(B, S // tm),
        in_specs=[
            pl.BlockSpec((1, tm, D), tok),
            pl.BlockSpec((1, 6, D), lambda b, j: (b, 0, 0)),
            pl.BlockSpec((1, D), const),
            pl.BlockSpec((D, P_PAD), const, pipeline_mode=pl.Buffered(1)),
            pl.BlockSpec((1, RW_PAD), const),
            pl.BlockSpec((1, Q_LORA), const),
            pl.BlockSpec((1, KV_LORA), const),
            pl.BlockSpec((1, LANES), const),
            pl.BlockSpec((1, LANES), const),
        ],
        out_specs=[
            pl.BlockSpec((1, tm, RW_PAD), tok),
            pl.BlockSpec((1, tm, Q_LORA), tok),
            pl.BlockSpec((1, tm, KV_LORA), tok),
            pl.BlockSpec((1, tm, LANES), tok),
        ],
        out_shape=[
            jax.ShapeDtypeStruct((B, S, RW_PAD), F32),
            jax.ShapeDtypeStruct((B, S, Q_LORA), BF16),
            jax.ShapeDtypeStruct((B, S, KV_LORA), BF16),
            jax.ShapeDtypeStruct((B, S, LANES), F32),
        ],
        scratch_shapes=[pltpu.VMEM((8, RW_PAD), F32)],
        compiler_params=pltpu.CompilerParams(
            dimension_semantics=("arbitrary", "arbitrary"), vmem_limit_bytes=VMEM_LIMIT),
    )(x, mod, g, w, mu, qg, kvg, lng, lnb)


def _softplus(z):
    return jnp.maximum(z, 0.0) + jnp.log(1.0 + jnp.exp(-jnp.abs(z)))


def _rwkv_kernel(*refs, has_vres):
    if has_vres:
        rw_ref, vf_ref, vec_ref, w2_ref, a2_ref, g2_ref, v2_ref, y_ref, h_ref = refs
    else:
        rw_ref, vec_ref, w2_ref, a2_ref, g2_ref, y_ref, h_ref = refs
    W = RWKV_WIDTH
    N = RWKV_HEAD_DIM
    C = RW_CHUNK
    nb, tcb = rw_ref.shape[0], rw_ref.shape[1]
    tc = nb * tcb

    @pl.when(pl.program_id(1) == 0)
    def _():
        h_ref[...] = jnp.zeros_like(h_ref)

    vec = vec_ref[...]
    w0, a0, v0, k_k, k_a, r_k, lnx_g, lnx_b = [vec[i:i + 1] for i in range(8)]

    def rows(ref, lo_, hi_):
        return ref[:, :, lo_:hi_].reshape(tc, hi_ - lo_)

    r = rows(rw_ref, RW_R, RW_R + W)
    k = rows(rw_ref, RW_K, RW_K + W)
    v = rows(rw_ref, RW_V, RW_V + W)
    wa = rows(rw_ref, RW_WA, RW_GM)
    gm = rows(rw_ref, RW_GM, RW_PAD)

    bd = (lax.broadcasted_iota(jnp.int32, (QUAD, QUAD), 0) // N
          == lax.broadcasted_iota(jnp.int32, (QUAD, QUAD), 1) // N)
    head_ones = bd.astype(BF16)

    def headsum(t, passes=2):
        parts = [t.astype(BF16)]
        if passes == 2:
            parts.append((t - parts[0].astype(F32)).astype(BF16))
        return jnp.concatenate(
            [sum(_dot(p[:, q * QUAD:(q + 1) * QUAD], head_ones) for p in parts)
             for q in range(W // QUAD)], axis=1)

    w_log = -_softplus(-(w0 + _dot3(jnp.tanh(wa), w2_ref[...]))) - 0.5
    logw = -jnp.exp(w_log)
    a = _sigmoid(a0 + _dot(wa.astype(BF16), a2_ref[...]))
    g = _dot(_sigmoid(gm).astype(BF16), g2_ref[...])
    if has_vres:
        v = v + (rows(vf_ref, 0, W) - v) * _sigmoid(v0 + _dot(gm.astype(BF16), v2_ref[...]))
    kk = k * k_k
    kk = kk * lax.rsqrt(jnp.maximum(headsum(kk * kk, passes=1), 1e-24))
    k = k * (1.0 + (a - 1.0) * k_a)

    ti = lax.broadcasted_iota(jnp.int32, (tc, tc), 0)
    tj = lax.broadcasted_iota(jnp.int32, (tc, tc), 1)
    tri = ((ti // C == tj // C) & (tj <= ti)).astype(BF16)
    lh = logw.astype(BF16)
    l1 = logw - lh.astype(F32)
    lm = l1.astype(BF16)
    ll = (l1 - lm.astype(F32)).astype(BF16)
    cum = _dot(tri, lh) + _dot(tri, lm) + _dot(tri, ll)

    bt = kk * jnp.exp(cum - logw)
    at = -(a * kk) * jnp.exp(-cum)
    kt = k * jnp.exp(-cum)
    rt = r * jnp.exp(cum)

    lane_q = lax.broadcasted_iota(jnp.int32, (1, QUAD), 1) // N
    hmask = [lane_q == h for h in range(4)]
    gr = lax.broadcasted_iota(jnp.int32, (2 * C, 2 * QUAD), 0)
    gc = lax.broadcasted_iota(jnp.int32, (2 * C, 2 * QUAD), 1) % C
    gmask = ((gr < C) & (gc < gr)) | ((gr >= C) & (gc <= gr - C))
    pr = lax.broadcasted_iota(jnp.int32, (C, LANES), 0)
    pc = lax.broadcasted_iota(jnp.int32, (C, LANES), 1)
    eye_pad = (pr == pc).astype(F32)
    left = pc < C
    zeros_cq = jnp.zeros((C, QUAD), F32)
    n_ch = tc // C
    n_q = W // QUAD

    cells = [(c, q) for c in range(n_ch) for q in range(n_q)]

    cell = {}
    for c, q in cells:
        rs = slice(c * C, (c + 1) * C)
        cs = slice(q * QUAD, (q + 1) * QUAD)
        tot = cum[c * C + C - 1:c * C + C, cs]
        tail = jnp.exp(tot - cum[rs, cs])
        atail = -(a[rs, cs] * kk[rs, cs]) * tail
        ktail = k[rs, cs] * tail
        dp_t = jnp.broadcast_to(jnp.exp(tot), (LANES, QUAD)).T
        vq = v[rs, cs]
        lhs = jnp.concatenate([bt[rs, cs], rt[rs, cs]], axis=0).astype(BF16)
        rhs_t = jnp.concatenate(
            [jnp.where(hmask[h], t, 0.0) for h in range(4) for t in (kt[rs, cs], at[rs, cs])],
            axis=0).astype(BF16)
        gram = jnp.where(gmask, _dot_nt(lhs, rhs_t), 0.0)
        cell[c, q] = dict(
            ak_t=jnp.concatenate([atail, ktail], axis=0).T.astype(BF16),
            dp_col=jnp.concatenate([dp_t, dp_t], axis=1), vq=vq, lhs=lhs, gram=gram,
            gram_b=gram.astype(BF16),
            v_masked=[jnp.where(hmask[h], vq, 0.0) for h in range(4)])

    pairs = [(c, q, j) for c, q in cells for j in range(2)]
    st = {}
    for c, q, j in pairs:
        for h in (2 * j, 2 * j + 1):
            st[c, q, h] = jnp.where(left, eye_pad, cell[c, q]["gram"][0:C, h * LANES:(h + 1) * LANES])
    zeros_pad = jnp.zeros((C, LANES), F32)
    for _ in range(6):
        for c, q, j in pairs:
            s0, s1 = st[c, q, 2 * j], st[c, q, 2 * j + 1]
            lhs2 = jnp.where(left, pltpu.roll(s0, C, axis=1), s1).astype(BF16)
            rhs2 = jnp.concatenate(
                [jnp.concatenate([s0, zeros_pad], axis=1),
                 jnp.concatenate([zeros_pad, s1], axis=1)], axis=0).astype(BF16)
            res = _dot(lhs2, rhs2)
            st[c, q, 2 * j] = jnp.where(left, s0 + res[:, :LANES], res[:, :LANES])
            st[c, q, 2 * j + 1] = jnp.where(left, s1 + res[:, LANES:], res[:, LANES:])

    for c, q in cells:
        o = cell[c, q]
        o["t_cat"] = jnp.concatenate(
            [jnp.where(left, st[c, q, h], 0.0) for h in range(4)], axis=1).astype(BF16)
        vs0 = jnp.concatenate(
            [t for h in range(4) for t in (o["v_masked"][h], zeros_cq)], axis=0).astype(BF16)
        o["akv"] = _dot(o["gram_b"][0:C], vs0)

    def recur(chunks, hq):
        keys = [(b, q) for b in range(nb) for q in range(n_q)]
        z, u, y = {}, {}, {}
        for b, q in keys:
            z[b, q] = _dot(cell[chunks[b], q]["lhs"], hq[b][q].astype(BF16))
        for b, q in keys:
            o = cell[chunks[b], q]
            xs = z[b, q][0:C] + o["akv"]
            xs_st = jnp.concatenate(
                [t for h in range(4) for t in (jnp.where(hmask[h], xs, 0.0), zeros_cq)],
                axis=0).astype(BF16)
            u[b, q] = _dot(o["t_cat"], xs_st)
        for b, q in keys:
            o = cell[chunks[b], q]
            uv = jnp.concatenate([u[b, q], o["vq"]], axis=0).astype(BF16)
            hq[b][q] = o["dp_col"] * hq[b][q] + jnp.where(bd, _dot(o["ak_t"], uv), 0.0)
        for b, q in keys:
            o = cell[chunks[b], q]
            uv_st = jnp.concatenate(
                [t for h in range(4) for t in (o["v_masked"][h], jnp.where(hmask[h], u[b, q], 0.0))],
                axis=0).astype(BF16)
            y[b, q] = z[b, q][C:2 * C] + _dot(o["gram_b"][C:2 * C], uv_st)
        return [jnp.concatenate([y[b, q] for q in range(n_q)], axis=1) for b in range(nb)]

    per_seq = tcb // C
    hq = [[h_ref[b * n_q + q] for q in range(n_q)] for b in range(nb)]
    y_rows = [None] * n_ch
    for s_ in range(per_seq):
        ys = recur([b * per_seq + s_ for b in range(nb)], hq)
        for b in range(nb):
            y_rows[b * per_seq + s_] = ys[b]
    for b in range(nb):
        for q in range(n_q):
            h_ref[b * n_q + q] = hq[b][q]
    y = jnp.concatenate(y_rows, axis=0)

    mu = headsum(y) * (1.0 / N)
    yc = y - mu
    var = headsum(yc * yc) * (1.0 / N)
    yn = yc * lax.rsqrt(var + LNX_EPS) * lnx_g + lnx_b
    bonus = headsum(r * k * r_k, passes=1) * v
    y_ref[...] = ((yn + bonus) * g).astype(BF16).reshape(nb, tcb, W)


def _rwkv_call(rw, vfirst_src, vec, w2p, a2p, g2p, v2p):
    B, S, _ = rw.shape
    has_vres = vfirst_src is not None
    nb = min(RW_SEQS, B)
    tc = RW_TILE // nb
    const = lambda b, j: (0, 0)
    in_specs = [pl.BlockSpec((nb, tc, RW_PAD), lambda b, j: (b, j, 0))]
    args = [rw]
    if has_vres:
        in_specs.append(pl.BlockSpec((nb, tc, RWKV_WIDTH), lambda b, j: (b, j, RW_V // RWKV_WIDTH)))
        args.append(vfirst_src)
    in_specs += [pl.BlockSpec((8, RWKV_WIDTH), const),
                 pl.BlockSpec((LANES, RWKV_WIDTH), const),
                 pl.BlockSpec((LANES, RWKV_WIDTH), const),
                 pl.BlockSpec((2 * LANES, RWKV_WIDTH), const)]
    args += [vec, w2p, a2p, g2p]
    if has_vres:
        in_specs.append(pl.BlockSpec((2 * LANES, RWKV_WIDTH), const))
        args.append(v2p)
    return pl.pallas_call(
        functools.partial(_rwkv_kernel, has_vres=has_vres),
        grid=(B // nb, S // tc),
        in_specs=in_specs,
        out_specs=pl.BlockSpec((nb, tc, RWKV_WIDTH), lambda b, j: (b, j, 0)),
        out_shape=jax.ShapeDtypeStruct((B, S, RWKV_WIDTH), BF16),
        scratch_shapes=[pltpu.VMEM((nb * (RWKV_WIDTH // QUAD), QUAD, QUAD), F32)],
        compiler_params=pltpu.CompilerParams(
            dimension_semantics=("arbitrary", "arbitrary"), vmem_limit_bytes=VMEM_LIMIT),
    )(*args)


def _part(t, op):
    r, n = t.shape
    return op(t.reshape(r // PART_ROWS, PART_ROWS, n), axis=0)


def _sweep(n, body, init, pair=None):
    if pair is None:
        def pair(j, c):
            return body(2 * j + 1, body(2 * j, c))
    c = lax.fori_loop(0, n // 2, pair, init)
    return lax.cond(n % 2 == 1, lambda c: body(n - 1, c), lambda c: c, c)


def _dsa_kernel(cq_ref, kwq_ref, kw_ref, ckv_ref, wqt_ref, wk_ref, wqit_ref, wvt_ref, y_ref,
                kidx_s, ckvt_s, sc_s, bias_s, att_s, yt_s, *, top_k):
    i = pl.program_id(1)
    nblk = kidx_s.shape[0]
    QB = DSA_Q_TILE
    KB = KEY_BLOCK
    HG = ATT_HEAD_GROUP

    @pl.when(i == 0)
    def _():
        for b in range(nblk):
            kidx_s[b] = kw_ref[0, b * KB:(b + 1) * KB, :].astype(BF16)
            ckvt_s[b, :KV_LORA, :] = ckv_ref[0, b * KB:(b + 1) * KB, :].astype(F32).T.astype(BF16)
            ckvt_s[b, KV_LORA:, :] = jnp.ones((ONES_ROWS, KB), BF16)

    cq_t = cq_ref[0].astype(F32).T.astype(BF16)
    q_t = _dot(wqt_ref[...], cq_t).astype(BF16)
    w_t = kwq_ref[0].T
    qi_all = jnp.concatenate([_dot(wqit_ref[h], cq_t) for h in range(IDX_HEADS)],
                             axis=1).astype(BF16)
    w_all = jnp.concatenate([w_t[IDX_DIM + h:IDX_DIM + h + 1, :] for h in range(IDX_HEADS)],
                            axis=1)

    nkb = (i * QB + QB + KB - 1) // KB
    q_chunk = (i * QB + lax.broadcasted_iota(jnp.int32, (1, QB), 1)) // CHUNK
    k_row = lax.broadcasted_iota(jnp.int32, (KB, 1), 0)
    k_eff = jnp.minimum(top_k, (q_chunk + 1) * CHUNK).astype(F32)
    inf = jnp.float32(jnp.inf)
    one = jnp.float32(1.0)
    zero = jnp.float32(0.0)

    def fin_min(t):
        return jnp.min(t, axis=0, keepdims=True)

    def fin_max(t):
        return jnp.max(t, axis=0, keepdims=True)

    def fin_sum(t):
        return jnp.sum(t, axis=0, keepdims=True)

    def score_body(kb, carry):
        lo, hi = carry
        lg = jnp.maximum(_dot(kidx_s[kb], qi_all), zero) * w_all
        acc = jnp.zeros((KB, QB), F32)
        for h in range(IDX_HEADS):
            acc = acc + lg[:, h * QB:(h + 1) * QB]
        adm = (kb * KB + k_row) // CHUNK <= q_chunk
        sc_s[kb] = jnp.where(adm, acc, -inf)
        lo = jnp.minimum(lo, _part(jnp.where(adm, acc, inf), jnp.min))
        hi = jnp.maximum(hi, _part(jnp.where(adm, acc, -inf), jnp.max))
        return lo, hi

    lo, hi = _sweep(nkb,score_body,
                           (jnp.full((PART_ROWS, QB), inf, F32), jnp.full((PART_ROWS, QB), -inf, F32)))
    lo = fin_min(lo)
    hi = fin_max(hi)

    def count_ge(t):
        def body(kb, acc):
            return acc + _part(jnp.where(sc_s[kb] >= t, one, zero), jnp.sum)
        return fin_sum(_sweep(nkb,body, jnp.zeros((PART_ROWS, QB), F32)))

    def bisect(_, carry):
        lo, hi = carry
        mid = 0.5 * lo + 0.5 * hi
        ge = count_ge(mid) >= k_eff
        return jnp.where(ge, mid, lo), jnp.where(ge, hi, mid)

    lo, hi = lax.fori_loop(0, N_BISECT, bisect, (lo, hi))

    def min_ge(t):
        def body(kb, acc):
            s = sc_s[kb]
            return jnp.minimum(acc, _part(jnp.where(s >= t, s, inf), jnp.min))
        return fin_min(_sweep(nkb, body, jnp.full((PART_ROWS, QB), inf, F32)))

    def walk_cond(st):
        return st[2] > 0

    def walk_body(st):
        m, _, _ = st

        def body(kb, carry):
            cnt, nxt = carry
            s = sc_s[kb]
            gt = s > m
            return (cnt + _part(jnp.where(gt, one, zero), jnp.sum),
                    jnp.minimum(nxt, _part(jnp.where(gt, s, inf), jnp.min)))

        cnt, nxt = _sweep(nkb,body,
                                 (jnp.zeros((PART_ROWS, QB), F32), jnp.full((PART_ROWS, QB), inf, F32)))
        n_gt = fin_sum(cnt)
        done = n_gt < k_eff
        pending = jnp.max(jnp.where(done, 0, 1))
        return jnp.where(done, m, fin_min(nxt)), n_gt, pending

    thr, n_gt, _ = lax.while_loop(
        walk_cond, walk_body, (min_ge(lo), jnp.zeros((1, QB), F32), jnp.int32(1)))
    need = k_eff - n_gt

    tri = (lax.broadcasted_iota(jnp.int32, (KB, KB), 1)
           <= lax.broadcasted_iota(jnp.int32, (KB, KB), 0)).astype(BF16)

    def mask_body(kb, carry):
        s = sc_s[kb]
        tie = jnp.where(s == thr, one, zero)
        pre = _dot(tri, tie.astype(BF16)) + carry
        sel = (s > thr) | ((s == thr) & (pre <= need))
        bias_s[kb] = jnp.where(sel, zero, jnp.float32(NEG_BIG))
        return carry + fin_sum(_part(tie, jnp.sum))

    _sweep(nkb, mask_body, jnp.zeros((1, QB), F32))

    scale = DSA_HEAD_DIM ** -0.5 * LOG2_E
    dh = DSA_HEAD_DIM
    for g in range(DSA_HEADS // HG):
        qa_all = jnp.concatenate(
            [_dot(wk_ref[g * HG + h], q_t[(g * HG + h) * dh:(g * HG + h + 1) * dh]) * scale
             for h in range(HG)], axis=1).astype(BF16)

        def att_body(kb, mx):
            kv = ckv_ref[0, pl.ds(pl.multiple_of(kb * KB, KB), KB), :]
            lg = _dot(kv, qa_all)
            b = bias_s[kb]
            parts = []
            for h in range(HG):
                att = lg[:, h * QB:(h + 1) * QB] + b
                att_s[kb, :, h * QB:(h + 1) * QB] = att
                parts.append(_part(att, jnp.max))
            return jnp.maximum(mx, jnp.concatenate(parts, axis=1))

        mx = fin_max(_sweep(nkb, att_body, jnp.full((PART_ROWS, HG * QB), NEG_BIG, F32)))

        def prob(kb):
            return jnp.exp2(att_s[kb] - mx).astype(BF16)

        def pv_body(kb, o):
            return o + _dot(ckvt_s[kb], prob(kb))

        def pv_pair(j, o):
            return o + _dot(jnp.concatenate([ckvt_s[2 * j], ckvt_s[2 * j + 1]], axis=1),
                            jnp.concatenate([prob(2 * j), prob(2 * j + 1)], axis=0))

        o = _sweep(nkb, pv_body, jnp.zeros((KV_LORA + ONES_ROWS, HG * QB), F32), pv_pair)
        o = (o[:KV_LORA] / o[KV_LORA:KV_LORA + 1]).astype(BF16)
        for h in range(HG):
            hh = g * HG + h
            yt_s[hh * dh:(hh + 1) * dh, :] = _dot(wvt_ref[hh], o[:, h * QB:(h + 1) * QB])

    y_ref[0] = yt_s[...].T.astype(BF16)


def _dsa_call(cq, kw, ckv, wqt, wk, wqit, wvt):
    B, S, _ = cq.shape
    QB = DSA_Q_TILE
    KB = KEY_BLOCK
    nblk = S // KB
    top_k = min(TOPK_MAX, S // 4)
    c2 = lambda b, i: (0, 0)
    c3 = lambda b, i: (0, 0, 0)
    return pl.pallas_call(
        functools.partial(_dsa_kernel, top_k=top_k),
        grid=(B, S // QB),
        in_specs=[
            pl.BlockSpec((1, QB, Q_LORA), lambda b, i: (b, i, 0)),
            pl.BlockSpec((1, QB, LANES), lambda b, i: (b, i, 0)),
            pl.BlockSpec((1, S, LANES), lambda b, i: (b, 0, 0)),
            pl.BlockSpec((1, S, KV_LORA), lambda b, i: (b, 0, 0)),
            pl.BlockSpec((DSA_WIDTH, Q_LORA), c2),
            pl.BlockSpec((DSA_HEADS, KV_LORA, DSA_HEAD_DIM), c3),
            pl.BlockSpec((IDX_HEADS, LANES, Q_LORA), c3),
            pl.BlockSpec((DSA_HEADS, DSA_HEAD_DIM, KV_LORA), c3),
        ],
        out_specs=pl.BlockSpec((1, QB, DSA_WIDTH), lambda b, i: (b, i, 0)),
        out_shape=jax.ShapeDtypeStruct((B, S, DSA_WIDTH), BF16),
        scratch_shapes=[
            pltpu.VMEM((nblk, KB, LANES), BF16),
            pltpu.VMEM((nblk, KV_LORA + ONES_ROWS, KB), BF16),
            pltpu.VMEM((nblk, KB, QB), F32),
            pltpu.VMEM((nblk, KB, QB), F32),
            pltpu.VMEM((nblk, KB, ATT_HEAD_GROUP * QB), F32),
            pltpu.VMEM((DSA_WIDTH, QB), F32),
        ],
        compiler_params=pltpu.CompilerParams(
            dimension_semantics=("arbitrary", "arbitrary"), vmem_limit_bytes=VMEM_LIMIT),
    )(cq, kw, kw, ckv, wqt, wk, wqit, wvt)


def _out_ffn_kernel(x_ref, yd_ref, yr_ref, mod_ref, gpm_ref, gff_ref, gpf_ref,
                    wo_ref, wfc_ref, wdn_ref, o_ref):
    mod = mod_ref[0]
    y = _dot(yd_ref[0], wo_ref[0]) + _dot(yr_ref[0], wo_ref[1])
    x1 = x_ref[0] + mod[2:3] * _rms(y, gpm_ref[...])
    h = (_rms(x1, gff_ref[...]) * (1.0 + mod[4:5]) + mod[3:4]).astype(BF16)
    nc, fc, _ = wdn_ref.shape
    acc = jnp.zeros_like(x1)
    for c in range(nc):
        gate = _dot(h, wfc_ref[:, c * fc:(c + 1) * fc])
        up = _dot(h, wfc_ref[:, (nc + c) * fc:(nc + c + 1) * fc])
        acc = acc + _dot((gate * _sigmoid(gate) * up).astype(BF16), wdn_ref[c])
    o_ref[0] = x1 + mod[5:6] * _rms(acc, gpf_ref[...])


def _out_ffn_call(x, yd, yr, mod, gpm, gff, gpf, wo, wfc, wdn, tm):
    B, S, D = x.shape
    nc, fc, _ = wdn.shape
    tok = lambda b, j: (b, j, 0)
    c2 = lambda b, j: (0, 0)
    c3 = lambda b, j: (0, 0, 0)
    return pl.pallas_call(
        _out_ffn_kernel,
        grid=(B, S // tm),
        in_specs=[
            pl.BlockSpec((1, tm, D), tok),
            pl.BlockSpec((1, tm, DSA_WIDTH), tok),
            pl.BlockSpec((1, tm, RWKV_WIDTH), tok),
            pl.BlockSpec((1, 6, D), lambda b, j: (b, 0, 0)),
            pl.BlockSpec((1, D), c2),
            pl.BlockSpec((1, D), c2),
            pl.BlockSpec((1, D), c2),
            pl.BlockSpec((2, DSA_WIDTH, D), c3, pipeline_mode=pl.Buffered(1)),
            pl.BlockSpec((D, 2 * nc * fc), c2, pipeline_mode=pl.Buffered(1)),
            pl.BlockSpec((nc, fc, D), c3, pipeline_mode=pl.Buffered(1)),
        ],
        out_specs=pl.BlockSpec((1, tm, D), tok),
        out_shape=jax.ShapeDtypeStruct((B, S, D), F32),
        compiler_params=pltpu.CompilerParams(
            dimension_semantics=("arbitrary", "arbitrary"), vmem_limit_bytes=VMEM_LIMIT),
    )(x, yd, yr, mod, gpm, gff, gpf, wo, wfc, wdn)


def _pad_rows(w, rows, at):
    out = jnp.zeros((rows,) + w.shape[1:], w.dtype)
    return out.at[at:at + w.shape[0]].set(w)


def kernel(x, c, ada_w, ada_b, pre_g_mix, post_g_mix, pre_g_ffn, post_g_ffn, w_in, w_in_vres, mu_shift, mu_vres, w_out, q_norm_g, kv_norm_g, w_q_up, w_qi_up, w_k_up, w_v_up, kidx_ln_g, kidx_ln_b, w0, w2, a0, a2, g2, v0, v2, k_k, k_a, r_k, lnx_g, lnx_b, w_fc, w_down):
    B, S, D = x.shape
    L = w_in.shape[0]
    n_dsa = Q_LORA + KV_LORA + IDX_DIM + IDX_HEADS
    tm = min(512, S)
    ff_chunk = D_FF // 2

    mod_all = _mod_call(c, ada_w, ada_b).reshape(L, B, 6, D)

    row = lambda t: t.reshape(1, -1)
    lane_pad = lambda t: jnp.pad(t, (0, LANES - t.shape[0])).reshape(1, LANES)

    v_first_src = None
    for l in range(L):
        wl = w_in[l]
        dsa_w, rw_w = wl[:, :n_dsa], wl[:, n_dsa:]
        if l == 0:
            mv_w = jnp.zeros((D, MV_LORA), F32)
            mv_mu = jnp.zeros((MV_LORA,), F32)
        else:
            mv_w, mv_mu = w_in_vres[l - 1], mu_vres[l - 1]
        zpad = lambda n: jnp.zeros((D, n), F32)
        w_all = jnp.concatenate(
            [rw_w, mv_w, zpad(RW_PAD - rw_w.shape[1] - MV_LORA),
             dsa_w, zpad(P_PAD - COL_KW - IDX_DIM - IDX_HEADS)], axis=1).astype(BF16)
        mu_all = jnp.concatenate(
            [mu_shift[l], mv_mu, jnp.zeros((RW_PAD - rw_w.shape[1] - MV_LORA,), F32)]).reshape(1, RW_PAD)

        rw, cq, ckv, kw = _inproj_call(
            x, mod_all[l], row(pre_g_mix[l]), w_all, mu_all, row(q_norm_g[l]), row(kv_norm_g[l]),
            lane_pad(kidx_ln_g[l]), lane_pad(kidx_ln_b[l]), min(INPROJ_TILE, S))

        vec = jnp.stack([w0[l], a0[l], v0[l - 1] if l else jnp.zeros_like(w0[l]), k_k[l], k_a[l],
                         r_k[l].reshape(-1), lnx_g[l], lnx_b[l]])
        w2p = _pad_rows(w2[l], LANES, 0)
        a2p = _pad_rows(a2[l], LANES, DECAY_LORA).astype(BF16)
        g2p = _pad_rows(g2[l], 2 * LANES, 0).astype(BF16)
        v2p = _pad_rows(v2[l - 1], 2 * LANES, GATE_LORA).astype(BF16) if l else None
        y_rwkv = _rwkv_call(rw, v_first_src, vec, w2p, a2p, g2p, v2p)
        if l == 0:
            v_first_src = rw

        wqt = w_q_up[l].reshape(Q_LORA, DSA_WIDTH).T.astype(BF16)
        wk = jnp.transpose(w_k_up[l], (1, 0, 2)).astype(BF16)
        wqit = jnp.pad(jnp.transpose(w_qi_up[l], (1, 2, 0)),
                       ((0, 0), (0, LANES - IDX_DIM), (0, 0))).astype(BF16)
        wvt = jnp.transpose(w_v_up[l], (1, 2, 0)).astype(BF16)
        y_dsa = _dsa_call(cq, kw, ckv, wqt, wk, wqit, wvt)

        wo = w_out[l].reshape(2, DSA_WIDTH, D).astype(BF16)
        wfc = w_fc[l].astype(BF16)
        wdn = w_down[l].reshape(D_FF // ff_chunk, ff_chunk, D).astype(BF16)
        x = _out_ffn_call(x, y_dsa, y_rwkv, mod_all[l], row(post_g_mix[l]), row(pre_g_ffn[l]),
                          row(post_g_ffn[l]), wo, wfc, wdn, tm)
    return x
```

```python
import functools

import jax
import jax.numpy as jnp
from jax import lax
from jax.experimental import pallas as pl
from jax.experimental.pallas import tpu as pltpu

F32 = jnp.float32
BF16 = jnp.bfloat16

D_MODEL = 1024
CHUNK = 64
Q_BLOCK = 128
DSA_HEADS = 8
DSA_HEAD_DIM = 64
DSA_WIDTH = DSA_HEADS * DSA_HEAD_DIM
Q_LORA = 256
KV_LORA = 128
IDX_HEADS = 4
IDX_DIM = 64
TOPK_MAX = 256
RWKV_HEADS = 8
RWKV_HEAD_DIM = 64
RWKV_WIDTH = RWKV_HEADS * RWKV_HEAD_DIM
DECAY_LORA = 64
AAA_LORA = 64
MV_LORA = 32
GATE_LORA = 160
D_FF = ((8 * D_MODEL + 3 * 256 - 1) // (3 * 256)) * 256
NORM_EPS = 1e-6
LNX_EPS = 64e-5

LANES = 128
VMEM_LIMIT = 56 * 1024 * 1024

RW_R, RW_K, RW_V = 0, RWKV_WIDTH, 2 * RWKV_WIDTH
RW_WA = 3 * RWKV_WIDTH
RW_GM = RW_WA + DECAY_LORA + AAA_LORA
RW_PAD = RW_GM + 2 * LANES
COL_CQ = RW_PAD
COL_CKV = COL_CQ + Q_LORA
COL_KW = COL_CKV + KV_LORA
P_PAD = COL_KW + LANES

FF_CHUNK = 256
INPROJ_COL_BLOCK = 512
INPROJ_ROW_BLOCK = 512
INPROJ_TILE = 1024
RW_CHUNK = 64
RW_TILE = 256
RW_SEQS = 4
QUAD = 4 * RWKV_HEAD_DIM
N_BISECT = 18
KEY_BLOCK = 256
DSA_Q_TILE = 256
ATT_HEAD_GROUP = 8
PART_ROWS = 32
ONES_ROWS = 16
LOG2_E = 1.4426950408889634
NEG_BIG = -1e30


def _dot(a, b):
    return jnp.dot(a, b, preferred_element_type=F32)


def _dot_nt(a, b):
    return lax.dot_general(a, b, (((1,), (1,)), ((), ())), preferred_element_type=F32)


def _split(a):
    hi = a.astype(BF16)
    lo = (a - hi.astype(F32)).astype(BF16)
    return hi, lo


def _dot3(a, b):
    ah, al = _split(a)
    bh, bl = _split(b)
    return _dot(ah, bh) + _dot(al, bh) + _dot(ah, bl)


def _rms(x, g):
    return x * lax.rsqrt(jnp.mean(x * x, axis=-1, keepdims=True) + NORM_EPS) * g


def _sigmoid(x):
    return 1.0 / (1.0 + jnp.exp(-x))


def _mod_kernel(c_ref, w_ref, b_ref, o_ref):
    c = c_ref[...]
    cond = c * _sigmoid(c)
    o_ref[0] = _dot3(cond, w_ref[0]) + b_ref[0]


def _mod_call(c, ada_w, ada_b):
    L, D, D6 = ada_w.shape
    B = c.shape[0]
    nj = D6 // D
    return pl.pallas_call(
        _mod_kernel,
        grid=(L, nj),
        in_specs=[
            pl.BlockSpec((B, D), lambda l, j: (0, 0)),
            pl.BlockSpec((1, D, D), lambda l, j: (l, 0, j)),
            pl.BlockSpec((1, 1, D), lambda l, j: (l, 0, j)),
        ],
        out_specs=pl.BlockSpec((1, B, D), lambda l, j: (l, 0, j)),
        out_shape=jax.ShapeDtypeStruct((L, B, D6), F32),
        compiler_params=pltpu.CompilerParams(vmem_limit_bytes=VMEM_LIMIT),
    )(c, ada_w, ada_b.reshape(L, 1, D6))


def _inproj_kernel(x_ref, mod_ref, g_ref, w_ref, mu_ref, qg_ref, kvg_ref, lng_ref, lnb_ref,
                   rw_ref, cq_ref, ckv_ref, kw_ref, carry_ref):
    j = pl.program_id(1)
    tm = x_ref.shape[1]
    rb = min(INPROJ_ROW_BLOCK, tm)
    mod = mod_ref[0]

    @pl.when(j == 0)
    def _():
        carry_ref[...] = jnp.zeros_like(carry_ref)

    row = lax.broadcasted_iota(jnp.int32, (rb, 1), 0)
    lane = lax.broadcasted_iota(jnp.int32, (1, LANES), 1)
    is_k = lane < IDX_DIM

    for r0 in range(0, tm, rb):
        rs = slice(r0, r0 + rb)
        h = (_rms(x_ref[0, rs, :], g_ref[...]) * (1.0 + mod[1:2]) + mod[0:1]).astype(BF16)

        def proj(lo_, hi_):
            return _dot(h, w_ref[:, lo_:hi_])

        for lo_ in range(0, RW_PAD, INPROJ_COL_BLOCK):
            hi_ = min(lo_ + INPROJ_COL_BLOCK, RW_PAD)
            u = proj(lo_, hi_)
            prev = jnp.where(row == 0, carry_ref[0:1, lo_:hi_], pltpu.roll(u, 1, axis=0))
            carry_ref[0:1, lo_:hi_] = u[rb - 1:rb, :]
            rw_ref[0, rs, lo_:hi_] = u + (prev - u) * mu_ref[:, lo_:hi_]

        cq_ref[0, rs, :] = _rms(proj(COL_CQ, COL_CKV), qg_ref[...]).astype(BF16)
        ckv_ref[0, rs, :] = _rms(proj(COL_CKV, COL_KW), kvg_ref[...]).astype(BF16)

        kw = proj(COL_KW, P_PAD)
        mean = jnp.sum(jnp.where(is_k, kw, 0.0), axis=-1, keepdims=True) * (1.0 / IDX_DIM)
        cen = jnp.where(is_k, kw - mean, 0.0)
        var = jnp.sum(cen * cen, axis=-1, keepdims=True) * (1.0 / IDX_DIM)
        kn = cen * lax.rsqrt(var + NORM_EPS) * lng_ref[...] + lnb_ref[...]
        kw_ref[0, rs, :] = jnp.where(is_k, kn, kw * ((IDX_HEADS ** -0.5) * (IDX_DIM ** -0.5)))


def _inproj_call(x, mod, g, w, mu, qg, kvg, lng, lnb, tm):
    B, S, D = x.shape
    const = lambda b, j: (0, 0)
    tok = lambda b, j: (b, j, 0)
    return pl.pallas_call(
        _inproj_kernel,
        grid=(B, S // tm),
        in_specs=[
            pl.BlockSpec((1, tm, D), tok),
            pl.BlockSpec((1, 6, D), lambda b, j: (b, 0, 0)),
            pl.BlockSpec((1, D), const),
            pl.BlockSpec((D, P_PAD), const, pipeline_mode=pl.Buffered(1)),
            pl.BlockSpec((1, RW_PAD), const),
            pl.BlockSpec((1, Q_LORA), const),
            pl.BlockSpec((1, KV_LORA), const),
            pl.BlockSpec((1, LANES), const),
            pl.BlockSpec((1, LANES), const),
        ],
        out_specs=[
            pl.BlockSpec((1, tm, RW_PAD), tok),
            pl.BlockSpec((1, tm, Q_LORA), tok),
            pl.BlockSpec((1, tm, KV_LORA), tok),
            pl.BlockSpec((1, tm, LANES), tok),
        ],
        out_shape=[
            jax.ShapeDtypeStruct((B, S, RW_PAD), F32),
            jax.ShapeDtypeStruct((B, S, Q_LORA), BF16),
            jax.ShapeDtypeStruct((B, S, KV_LORA), BF16),
            jax.ShapeDtypeStruct((B, S, LANES), F32),
        ],
        scratch_shapes=[pltpu.VMEM((8, RW_PAD), F32)],
        compiler_params=pltpu.CompilerParams(
            dimension_semantics=("arbitrary", "arbitrary"), vmem_limit_bytes=VMEM_LIMIT),
    )(x, mod, g, w, mu, qg, kvg, lng, lnb)


def _softplus(z):
    return jnp.maximum(z, 0.0) + jnp.log(1.0 + jnp.exp(-jnp.abs(z)))


def _rwkv_kernel(*refs, has_vres):
    if has_vres:
        rw_ref, vf_ref, vec_ref, w2_ref, a2_ref, g2_ref, v2_ref, y_ref, h_ref = refs
    else:
        rw_ref, vec_ref, w2_ref, a2_ref, g2_ref, y_ref, h_ref = refs
    W = RWKV_WIDTH
    N = RWKV_HEAD_DIM
    C = RW_CHUNK
    nb, tcb = rw_ref.shape[0], rw_ref.shape[1]
    tc = nb * tcb

    @pl.when(pl.program_id(1) == 0)
    def _():
        h_ref[...] = jnp.zeros_like(h_ref)

    vec = vec_ref[...]
    w0, a0, v0, k_k, k_a, r_k, lnx_g, lnx_b = [vec[i:i + 1] for i in range(8)]

    def rows(ref, lo_, hi_):
        return ref[:, :, lo_:hi_].reshape(tc, hi_ - lo_)

    r = rows(rw_ref, RW_R, RW_R + W)
    k = rows(rw_ref, RW_K, RW_K + W)
    v = rows(rw_ref, RW_V, RW_V + W)
    wa = rows(rw_ref, RW_WA, RW_GM)
    gm = rows(rw_ref, RW_GM, RW_PAD)

    bd = (lax.broadcasted_iota(jnp.int32, (QUAD, QUAD), 0) // N
          == lax.broadcasted_iota(jnp.int32, (QUAD, QUAD), 1) // N)
    head_ones = bd.astype(BF16)

    def headsum(t, passes=2):
        parts = [t.astype(BF16)]
        if passes == 2:
            parts.append((t - parts[0].astype(F32)).astype(BF16))
        return jnp.concatenate(
            [sum(_dot(p[:, q * QUAD:(q + 1) * QUAD], head_ones) for p in parts)
             for q in range(W // QUAD)], axis=1)

    w_log = -_softplus(-(w0 + _dot3(jnp.tanh(wa), w2_ref[...]))) - 0.5
    logw = -jnp.exp(w_log)
    a = _sigmoid(a0 + _dot(wa.astype(BF16), a2_ref[...]))
    g = _dot(_sigmoid(gm).astype(BF16), g2_ref[...])
    if has_vres:
        v = v + (rows(vf_ref, 0, W) - v) * _sigmoid(v0 + _dot(gm.astype(BF16), v2_ref[...]))
    kk = k * k_k
    kk = kk * lax.rsqrt(jnp.maximum(headsum(kk * kk, passes=1), 1e-24))
    k = k * (1.0 + (a - 1.0) * k_a)

    ti = lax.broadcasted_iota(jnp.int32, (tc, tc), 0)
    tj = lax.broadcasted_iota(jnp.int32, (tc, tc), 1)
    tri = ((ti // C == tj // C) & (tj <= ti)).astype(BF16)
    lh = logw.astype(BF16)
    l1 = logw - lh.astype(F32)
    lm = l1.astype(BF16)
    ll = (l1 - lm.astype(F32)).astype(BF16)
    cum = _dot(tri, lh) + _dot(tri, lm) + _dot(tri, ll)

    bt = kk * jnp.exp(cum - logw)
    at = -(a * kk) * jnp.exp(-cum)
    kt = k * jnp.exp(-cum)
    rt = r * jnp.exp(cum)

    lane_q = lax.broadcasted_iota(jnp.int32, (1, QUAD), 1) // N
    hmask = [lane_q == h for h in range(4)]
    gr = lax.broadcasted_iota(jnp.int32, (2 * C, 2 * QUAD), 0)
    gc = lax.broadcasted_iota(jnp.int32, (2 * C, 2 * QUAD), 1) % C
    gmask = ((gr < C) & (gc < gr)) | ((gr >= C) & (gc <= gr - C))
    pr = lax.broadcasted_iota(jnp.int32, (C, LANES), 0)
    pc = lax.broadcasted_iota(jnp.int32, (C, LANES), 1)
    eye_pad = (pr == pc).astype(F32)
    left = pc < C
    zeros_cq = jnp.zeros((C, QUAD), F32)
    n_ch = tc // C
    n_q = W // QUAD

    cells = [(c, q) for c in range(n_ch) for q in range(n_q)]

    cell = {}
    for c, q in cells:
        rs = slice(c * C, (c + 1) * C)
        cs = slice(q * QUAD, (q + 1) * QUAD)
        tot = cum[c * C + C - 1:c * C + C, cs]
        tail = jnp.exp(tot - cum[rs, cs])
        atail = -(a[rs, cs] * kk[rs, cs]) * tail
        ktail = k[rs, cs] * tail
        dp_t = jnp.broadcast_to(jnp.exp(tot), (LANES, QUAD)).T
        vq = v[rs, cs]
        lhs = jnp.concatenate([bt[rs, cs], rt[rs, cs]], axis=0).astype(BF16)
        rhs_t = jnp.concatenate(
            [jnp.where(hmask[h], t, 0.0) for h in range(4) for t in (kt[rs, cs], at[rs, cs])],
            axis=0).astype(BF16)
        gram = jnp.where(gmask, _dot_nt(lhs, rhs_t), 0.0)
        cell[c, q] = dict(
            ak_t=jnp.concatenate([atail, ktail], axis=0).T.astype(BF16),
            dp_col=jnp.concatenate([dp_t, dp_t], axis=1), vq=vq, lhs=lhs, gram=gram,
            gram_b=gram.astype(BF16),
            v_masked=[jnp.where(hmask[h], vq, 0.0) for h in range(4)])

    pairs = [(c, q, j) for c, q in cells for j in range(2)]
    st = {}
    for c, q, j in pairs:
        for h in (2 * j, 2 * j + 1):
            st[c, q, h] = jnp.where(left, eye_pad, cell[c, q]["gram"][0:C, h * LANES:(h + 1) * LANES])
    zeros_pad = jnp.zeros((C, LANES), F32)
    for _ in range(6):
        for c, q, j in pairs:
            s0, s1 = st[c, q, 2 * j], st[c, q, 2 * j + 1]
            lhs2 = jnp.where(left, pltpu.roll(s0, C, axis=1), s1).astype(BF16)
            rhs2 = jnp.concatenate(
                [jnp.concatenate([s0, zeros_pad], axis=1),
                 jnp.concatenate([zeros_pad, s1], axis=1)], axis=0).astype(BF16)
            res = _dot(lhs2, rhs2)
            st[c, q, 2 * j] = jnp.where(left, s0 + res[:, :LANES], res[:, :LANES])
            st[c, q, 2 * j + 1] = jnp.where(left, s1 + res[:, LANES:], res[:, LANES:])

    for c, q in cells:
        o = cell[c, q]
        o["t_cat"] = jnp.concatenate(
            [jnp.where(left, st[c, q, h], 0.0) for h in range(4)], axis=1).astype(BF16)
        vs0 = jnp.concatenate(
            [t for h in range(4) for t in (o["v_masked"][h], zeros_cq)], axis=0).astype(BF16)
        o["akv"] = _dot(o["gram_b"][0:C], vs0)

    def recur(chunks, hq):
        keys = [(b, q) for b in range(nb) for q in range(n_q)]
        z, u, y = {}, {}, {}
        for b, q in keys:
            z[b, q] = _dot(cell[chunks[b], q]["lhs"], hq[b][q].astype(BF16))
        for b, q in keys:
            o = cell[chunks[b], q]
            xs = z[b, q][0:C] + o["akv"]
            xs_st = jnp.concatenate(
                [t for h in range(4) for t in (jnp.where(hmask[h], xs, 0.0), zeros_cq)],
                axis=0).astype(BF16)
            u[b, q] = _dot(o["t_cat"], xs_st)
        for b, q in keys:
            o = cell[chunks[b], q]
            uv = jnp.concatenate([u[b, q], o["vq"]], axis=0).astype(BF16)
            hq[b][q] = o["dp_col"] * hq[b][q] + jnp.where(bd, _dot(o["ak_t"], uv), 0.0)
        for b, q in keys:
            o = cell[chunks[b], q]
            uv_st = jnp.concatenate(
                [t for h in range(4) for t in (o["v_masked"][h], jnp.where(hmask[h], u[b, q], 0.0))],
                axis=0).astype(BF16)
            y[b, q] = z[b, q][C:2 * C] + _dot(o["gram_b"][C:2 * C], uv_st)
        return [jnp.concatenate([y[b, q] for q in range(n_q)], axis=1) for b in range(nb)]

    per_seq = tcb // C
    hq = [[h_ref[b * n_q + q] for q in range(n_q)] for b in range(nb)]
    y_rows = [None] * n_ch
    for s_ in range(per_seq):
        ys = recur([b * per_seq + s_ for b in range(nb)], hq)
        for b in range(nb):
            y_rows[b * per_seq + s_] = ys[b]
    for b in range(nb):
        for q in range(n_q):
            h_ref[b * n_q + q] = hq[b][q]
    y = jnp.concatenate(y_rows, axis=0)

    mu = headsum(y) * (1.0 / N)
    yc = y - mu
    var = headsum(yc * yc) * (1.0 / N)
    yn = yc * lax.rsqrt(var + LNX_EPS) * lnx_g + lnx_b
    bonus = headsum(r * k * r_k, passes=1) * v
    y_ref[...] = ((yn + bonus) * g).astype(BF16).reshape(nb, tcb, W)


def _rwkv_call(rw, vfirst_src, vec, w2p, a2p, g2p, v2p):
    B, S, _ = rw.shape
    has_vres = vfirst_src is not None
    nb = min(RW_SEQS, B)
    tc = RW_TILE // nb
    const = lambda b, j: (0, 0)
    in_specs = [pl.BlockSpec((nb, tc, RW_PAD), lambda b, j: (b, j, 0))]
    args = [rw]
    if has_vres:
        in_specs.append(pl.BlockSpec((nb, tc, RWKV_WIDTH), lambda b, j: (b, j, RW_V // RWKV_WIDTH)))
        args.append(vfirst_src)
    in_specs += [pl.BlockSpec((8, RWKV_WIDTH), const),
                 pl.BlockSpec((LANES, RWKV_WIDTH), const),
                 pl.BlockSpec((LANES, RWKV_WIDTH), const),
                 pl.BlockSpec((2 * LANES, RWKV_WIDTH), const)]
    args += [vec, w2p, a2p, g2p]
    if has_vres:
        in_specs.append(pl.BlockSpec((2 * LANES, RWKV_WIDTH), const))
        args.append(v2p)
    return pl.pallas_call(
        functools.partial(_rwkv_kernel, has_vres=has_vres),
        grid=(B // nb, S // tc),
        in_specs=in_specs,
        out_specs=pl.BlockSpec((nb, tc, RWKV_WIDTH), lambda b, j: (b, j, 0)),
        out_shape=jax.ShapeDtypeStruct((B, S, RWKV_WIDTH), BF16),
        scratch_shapes=[pltpu.VMEM((nb * (RWKV_WIDTH // QUAD), QUAD, QUAD), F32)],
        compiler_params=pltpu.CompilerParams(
            dimension_semantics=("arbitrary", "arbitrary"), vmem_limit_bytes=VMEM_LIMIT),
    )(*args)


def _part(t, op):
    r, n = t.shape
    return op(t.reshape(r // PART_ROWS, PART_ROWS, n), axis=0)


def _sweep(n, body, init, pair=None):
    if pair is None:
        def pair(j, c):
            return body(2 * j + 1, body(2 * j, c))
    c = lax.fori_loop(0, n // 2, pair, init)
    return lax.cond(n % 2 == 1, lambda c: body(n - 1, c), lambda c: c, c)


def _dsa_kernel(cq_ref, kwq_ref, kw_ref, ckv_ref, wqt_ref, wk_ref, wqit_ref, wvt_ref, y_ref,
                kidx_s, ckvt_s, sc_s, bias_s, att_s, yt_s, *, top_k):
    i = pl.program_id(1)
    nblk = kidx_s.shape[0]
    QB = DSA_Q_TILE
    KB = KEY_BLOCK
    HG = ATT_HEAD_GROUP

    @pl.when(i == 0)
    def _():
        for b in range(nblk):
            kidx_s[b] = kw_ref[0, b * KB:(b + 1) * KB, :].astype(BF16)
            ckvt_s[b, :KV_LORA, :] = ckv_ref[0, b * KB:(b + 1) * KB, :].astype(F32).T.astype(BF16)
            ckvt_s[b, KV_LORA:, :] = jnp.ones((ONES_ROWS, KB), BF16)

    cq_t = cq_ref[0].astype(F32).T.astype(BF16)
    q_t = _dot(wqt_ref[...], cq_t).astype(BF16)
    w_t = kwq_ref[0].T
    qi_all = jnp.concatenate([_dot(wqit_ref[h], cq_t) for h in range(IDX_HEADS)],
                             axis=1).astype(BF16)
    w_all = jnp.concatenate([w_t[IDX_DIM + h:IDX_DIM + h + 1, :] for h in range(IDX_HEADS)],
                            axis=1)
    scale = DSA_HEAD_DIM ** -0.5 * LOG2_E
    dh = DSA_HEAD_DIM
    qa_groups = [
        jnp.concatenate(
            [_dot(wk_ref[g * HG + h], q_t[(g * HG + h) * dh:(g * HG + h + 1) * dh]) * scale
             for h in range(HG)], axis=1).astype(BF16)
        for g in range(DSA_HEADS // HG)]

    nkb = (i * QB + QB + KB - 1) // KB
    q_chunk = (i * QB + lax.broadcasted_iota(jnp.int32, (1, QB), 1)) // CHUNK
    k_row = lax.broadcasted_iota(jnp.int32, (KB, 1), 0)
    k_eff = jnp.minimum(top_k, (q_chunk + 1) * CHUNK).astype(F32)
    inf = jnp.float32(jnp.inf)
    one = jnp.float32(1.0)
    zero = jnp.float32(0.0)

    def fin_min(t):
        return jnp.min(t, axis=0, keepdims=True)

    def fin_max(t):
        return jnp.max(t, axis=0, keepdims=True)

    def fin_sum(t):
        return jnp.sum(t, axis=0, keepdims=True)

    def score_body(kb, carry):
        lo, hi = carry
        lg = jnp.maximum(_dot(kidx_s[kb], qi_all), zero) * w_all
        acc = jnp.zeros((KB, QB), F32)
        for h in range(IDX_HEADS):
            acc = acc + lg[:, h * QB:(h + 1) * QB]
        adm = (kb * KB + k_row) // CHUNK <= q_chunk
        sc_s[kb] = jnp.where(adm, acc, -inf)
        lo = jnp.minimum(lo, _part(jnp.where(adm, acc, inf), jnp.min))
        hi = jnp.maximum(hi, _part(jnp.where(adm, acc, -inf), jnp.max))
        return lo, hi

    lo, hi = _sweep(nkb,score_body,
                           (jnp.full((PART_ROWS, QB), inf, F32), jnp.full((PART_ROWS, QB), -inf, F32)))
    lo = fin_min(lo)
    hi = fin_max(hi)

    def count_ge(t):
        def body(kb, acc):
            return acc + _part(jnp.where(sc_s[kb] >= t, one, zero), jnp.sum)
        return fin_sum(_sweep(nkb,body, jnp.zeros((PART_ROWS, QB), F32)))

    def bisect(_, carry):
        lo, hi = carry
        mid = 0.5 * lo + 0.5 * hi
        ge = count_ge(mid) >= k_eff
        return jnp.where(ge, mid, lo), jnp.where(ge, hi, mid)

    lo, hi = lax.fori_loop(0, N_BISECT, bisect, (lo, hi))

    def min_ge(t):
        def body(kb, acc):
            s = sc_s[kb]
            return jnp.minimum(acc, _part(jnp.where(s >= t, s, inf), jnp.min))
        return fin_min(_sweep(nkb, body, jnp.full((PART_ROWS, QB), inf, F32)))

    def walk_cond(st):
        return st[2] > 0

    def walk_body(st):
        m, _, _ = st

        def body(kb, carry):
            cnt, nxt = carry
            s = sc_s[kb]
            gt = s > m
            return (cnt + _part(jnp.where(gt, one, zero), jnp.sum),
                    jnp.minimum(nxt, _part(jnp.where(gt, s, inf), jnp.min)))

        cnt, nxt = _sweep(nkb,body,
                                 (jnp.zeros((PART_ROWS, QB), F32), jnp.full((PART_ROWS, QB), inf, F32)))
        n_gt = fin_sum(cnt)
        done = n_gt < k_eff
        pending = jnp.max(jnp.where(done, 0, 1))
        return jnp.where(done, m, fin_min(nxt)), n_gt, pending

    thr, n_gt, _ = lax.while_loop(
        walk_cond, walk_body, (min_ge(lo), jnp.zeros((1, QB), F32), jnp.int32(1)))
    need = k_eff - n_gt

    tri = (lax.broadcasted_iota(jnp.int32, (KB, KB), 1)
           <= lax.broadcasted_iota(jnp.int32, (KB, KB), 0)).astype(BF16)

    def mask_body(kb, carry):
        s = sc_s[kb]
        tie = jnp.where(s == thr, one, zero)
        pre = _dot(tri, tie.astype(BF16)) + carry
        sel = (s > thr) | ((s == thr) & (pre <= need))
        bias_s[kb] = jnp.where(sel, zero, jnp.float32(NEG_BIG))
        return carry + fin_sum(_part(tie, jnp.sum))

    _sweep(nkb, mask_body, jnp.zeros((1, QB), F32))

    for g in range(DSA_HEADS // HG):
        qa_all = qa_groups[g]

        def att_body(kb, mx):
            kv = ckv_ref[0, pl.ds(pl.multiple_of(kb * KB, KB), KB), :]
            lg = _dot(kv, qa_all)
            b = bias_s[kb]
            parts = []
            for h in range(HG):
                att = lg[:, h * QB:(h + 1) * QB] + b
                att_s[kb, :, h * QB:(h + 1) * QB] = att
                parts.append(_part(att, jnp.max))
            return jnp.maximum(mx, jnp.concatenate(parts, axis=1))

        mx = fin_max(_sweep(nkb, att_body, jnp.full((PART_ROWS, HG * QB), NEG_BIG, F32)))

        def prob(kb):
            return jnp.exp2(att_s[kb] - mx).astype(BF16)

        def pv_body(kb, o):
            return o + _dot(ckvt_s[kb], prob(kb))

        def pv_pair(j, o):
            return o + _dot(jnp.concatenate([ckvt_s[2 * j], ckvt_s[2 * j + 1]], axis=1),
                            jnp.concatenate([prob(2 * j), prob(2 * j + 1)], axis=0))

        o = _sweep(nkb, pv_body, jnp.zeros((KV_LORA + ONES_ROWS, HG * QB), F32), pv_pair)
        o = (o[:KV_LORA] / o[KV_LORA:KV_LORA + 1]).astype(BF16)
        for h in range(HG):
            hh = g * HG + h
            yt_s[hh * dh:(hh + 1) * dh, :] = _dot(wvt_ref[hh], o[:, h * QB:(h + 1) * QB])

    y_ref[0] = yt_s[...].T.astype(BF16)


def _dsa_call(cq, kw, ckv, wqt, wk, wqit, wvt):
    B, S, _ = cq.shape
    QB = DSA_Q_TILE
    KB = KEY_BLOCK
    nblk = S // KB
    top_k = min(TOPK_MAX, S // 4)
    c2 = lambda b, i: (0, 0)
    c3 = lambda b, i: (0, 0, 0)
    return pl.pallas_call(
        functools.partial(_dsa_kernel, top_k=top_k),
        grid=(B, S // QB),
        in_specs=[
            pl.BlockSpec((1, QB, Q_LORA), lambda b, i: (b, i, 0)),
            pl.BlockSpec((1, QB, LANES), lambda b, i: (b, i, 0)),
            pl.BlockSpec((1, S, LANES), lambda b, i: (b, 0, 0)),
            pl.BlockSpec((1, S, KV_LORA), lambda b, i: (b, 0, 0)),
            pl.BlockSpec((DSA_WIDTH, Q_LORA), c2),
            pl.BlockSpec((DSA_HEADS, KV_LORA, DSA_HEAD_DIM), c3),
            pl.BlockSpec((IDX_HEADS, LANES, Q_LORA), c3),
            pl.BlockSpec((DSA_HEADS, DSA_HEAD_DIM, KV_LORA), c3),
        ],
        out_specs=pl.BlockSpec((1, QB, DSA_WIDTH), lambda b, i: (b, i, 0)),
        out_shape=jax.ShapeDtypeStruct((B, S, DSA_WIDTH), BF16),
        scratch_shapes=[
            pltpu.VMEM((nblk, KB, LANES), BF16),
            pltpu.VMEM((nblk, KV_LORA + ONES_ROWS, KB), BF16),
            pltpu.VMEM((nblk, KB, QB), F32),
            pltpu.VMEM((nblk, KB, QB), F32),
            pltpu.VMEM((nblk, KB, ATT_HEAD_GROUP * QB), F32),
            pltpu.VMEM((DSA_WIDTH, QB), F32),
        ],
        compiler_params=pltpu.CompilerParams(
            dimension_semantics=("arbitrary", "arbitrary"), vmem_limit_bytes=VMEM_LIMIT),
    )(cq, kw, kw, ckv, wqt, wk, wqit, wvt)


def _out_ffn_kernel(x_ref, yd_ref, yr_ref, mod_ref, gpm_ref, gff_ref, gpf_ref,
                    wo_ref, wfc_ref, wdn_ref, o_ref):
    mod = mod_ref[0]
    y = _dot(yd_ref[0], wo_ref[0]) + _dot(yr_ref[0], wo_ref[1])
    x1 = x_ref[0] + mod[2:3] * _rms(y, gpm_ref[...])
    h = (_rms(x1, gff_ref[...]) * (1.0 + mod[4:5]) + mod[3:4]).astype(BF16)
    nc, fc, _ = wdn_ref.shape
    acc = jnp.zeros_like(x1)
    for c in range(nc):
        gate = _dot(h, wfc_ref[:, c * fc:(c + 1) * fc])
        up = _dot(h, wfc_ref[:, (nc + c) * fc:(nc + c + 1) * fc])
        acc = acc + _dot((gate * _sigmoid(gate) * up).astype(BF16), wdn_ref[c])
    o_ref[0] = x1 + mod[5:6] * _rms(acc, gpf_ref[...])


def _out_ffn_call(x, yd, yr, mod, gpm, gff, gpf, wo, wfc, wdn, tm):
    B, S, D = x.shape
    nc, fc, _ = wdn.shape
    tok = lambda b, j: (b, j, 0)
    c2 = lambda b, j: (0, 0)
    c3 = lambda b, j: (0, 0, 0)
    return pl.pallas_call(
        _out_ffn_kernel,
        grid=(B, S // tm),
        in_specs=[
            pl.BlockSpec((1, tm, D), tok),
            pl.BlockSpec((1, tm, DSA_WIDTH), tok),
            pl.BlockSpec((1, tm, RWKV_WIDTH), tok),
            pl.BlockSpec((1, 6, D), lambda b, j: (b, 0, 0)),
            pl.BlockSpec((1, D), c2),
            pl.BlockSpec((1, D), c2),
            pl.BlockSpec((1, D), c2),
            pl.BlockSpec((2, DSA_WIDTH, D), c3, pipeline_mode=pl.Buffered(1)),
            pl.BlockSpec((D, 2 * nc * fc), c2, pipeline_mode=pl.Buffered(1)),
            pl.BlockSpec((nc, fc, D), c3, pipeline_mode=pl.Buffered(1)),
        ],
        out_specs=pl.BlockSpec((1, tm, D), tok),
        out_shape=jax.ShapeDtypeStruct((B, S, D), F32),
        compiler_params=pltpu.CompilerParams(
            dimension_semantics=("arbitrary", "arbitrary"), vmem_limit_bytes=VMEM_LIMIT),
    )(x, yd, yr, mod, gpm, gff, gpf, wo, wfc, wdn)


def _pad_rows(w, rows, at):
    out = jnp.zeros((rows,) + w.shape[1:], w.dtype)
    return out.at[at:at + w.shape[0]].set(w)


def kernel(x, c, ada_w, ada_b, pre_g_mix, post_g_mix, pre_g_ffn, post_g_ffn, w_in, w_in_vres, mu_shift, mu_vres, w_out, q_norm_g, kv_norm_g, w_q_up, w_qi_up, w_k_up, w_v_up, kidx_ln_g, kidx_ln_b, w0, w2, a0, a2, g2, v0, v2, k_k, k_a, r_k, lnx_g, lnx_b, w_fc, w_down):
    B, S, D = x.shape
    L = w_in.shape[0]
    n_dsa = Q_LORA + KV_LORA + IDX_DIM + IDX_HEADS
    tm = min(512, S)
    ff_chunk = FF_CHUNK

    mod_all = _mod_call(c, ada_w, ada_b).reshape(L, B, 6, D)

    row = lambda t: t.reshape(1, -1)
    lane_pad = lambda t: jnp.pad(t, (0, LANES - t.shape[0])).reshape(1, LANES)

    v_first_src = None
    for l in range(L):
        wl = w_in[l]
        dsa_w, rw_w = wl[:, :n_dsa], wl[:, n_dsa:]
        if l == 0:
            mv_w = jnp.zeros((D, MV_LORA), F32)
            mv_mu = jnp.zeros((MV_LORA,), F32)
        else:
            mv_w, mv_mu = w_in_vres[l - 1], mu_vres[l - 1]
        zpad = lambda n: jnp.zeros((D, n), F32)
        w_all = jnp.concatenate(
            [rw_w, mv_w, zpad(RW_PAD - rw_w.shape[1] - MV_LORA),
             dsa_w, zpad(P_PAD - COL_KW - IDX_DIM - IDX_HEADS)], axis=1).astype(BF16)
        mu_all = jnp.concatenate(
            [mu_shift[l], mv_mu, jnp.zeros((RW_PAD - rw_w.shape[1] - MV_LORA,), F32)]).reshape(1, RW_PAD)

        rw, cq, ckv, kw = _inproj_call(
            x, mod_all[l], row(pre_g_mix[l]), w_all, mu_all, row(q_norm_g[l]), row(kv_norm_g[l]),
            lane_pad(kidx_ln_g[l]), lane_pad(kidx_ln_b[l]), min(INPROJ_TILE, S))

        vec = jnp.stack([w0[l], a0[l], v0[l - 1] if l else jnp.zeros_like(w0[l]), k_k[l], k_a[l],
                         r_k[l].reshape(-1), lnx_g[l], lnx_b[l]])
        w2p = _pad_rows(w2[l], LANES, 0)
        a2p = _pad_rows(a2[l], LANES, DECAY_LORA).astype(BF16)
        g2p = _pad_rows(g2[l], 2 * LANES, 0).astype(BF16)
        v2p = _pad_rows(v2[l - 1], 2 * LANES, GATE_LORA).astype(BF16) if l else None
        y_rwkv = _rwkv_call(rw, v_first_src, vec, w2p, a2p, g2p, v2p)
        if l == 0:
            v_first_src = rw

        wqt = w_q_up[l].reshape(Q_LORA, DSA_WIDTH).T.astype(BF16)
        wk = jnp.transpose(w_k_up[l], (1, 0, 2)).astype(BF16)
        wqit = jnp.pad(jnp.transpose(w_qi_up[l], (1, 2, 0)),
                       ((0, 0), (0, LANES - IDX_DIM), (0, 0))).astype(BF16)
        wvt = jnp.transpose(w_v_up[l], (1, 2, 0)).astype(BF16)
        y_dsa = _dsa_call(cq, kw, ckv, wqt, wk, wqit, wvt)

        wo = w_out[l].reshape(2, DSA_WIDTH, D).astype(BF16)
        wfc = w_fc[l].astype(BF16)
        wdn = w_down[l].reshape(D_FF // ff_chunk, ff_chunk, D).astype(BF16)
        x = _out_ffn_call(x, y_dsa, y_rwkv, mod_all[l], row(post_g_mix[l]), row(pre_g_ffn[l]),
                          row(post_g_ffn[l]), wo, wfc, wdn, tm)
    return x
```

```python
import functools

import jax
import jax.numpy as jnp
from jax import lax
from jax.experimental import pallas as pl
from jax.experimental.pallas import tpu as pltpu

F32 = jnp.float32
BF16 = jnp.bfloat16

D_MODEL = 1024
CHUNK = 64
Q_BLOCK = 128
DSA_HEADS = 8
DSA_HEAD_DIM = 64
DSA_WIDTH = DSA_HEADS * DSA_HEAD_DIM
Q_LORA = 256
KV_LORA = 128
IDX_HEADS = 4
IDX_DIM = 64
TOPK_MAX = 256
RWKV_HEADS = 8
RWKV_HEAD_DIM = 64
RWKV_WIDTH = RWKV_HEADS * RWKV_HEAD_DIM
DECAY_LORA = 64
AAA_LORA = 64
MV_LORA = 32
GATE_LORA = 160
D_FF = ((8 * D_MODEL + 3 * 256 - 1) // (3 * 256)) * 256
NORM_EPS = 1e-6
LNX_EPS = 64e-5

LANES = 128
VMEM_LIMIT = 56 * 1024 * 1024

RW_R, RW_K, RW_V = 0, RWKV_WIDTH, 2 * RWKV_WIDTH
RW_WA = 3 * RWKV_WIDTH
RW_GM = RW_WA + DECAY_LORA + AAA_LORA
RW_PAD = RW_GM + 2 * LANES
COL_CQ = RW_PAD
COL_CKV = COL_CQ + Q_LORA
COL_KW = COL_CKV + KV_LORA
P_PAD = COL_KW + LANES

FF_CHUNK = 256
INPROJ_COL_BLOCK = 512
INPROJ_ROW_BLOCK = 512
INPROJ_TILE = 1024
RW_CHUNK = 64
RW_TILE = 256
RW_SEQS = 4
QUAD = 4 * RWKV_HEAD_DIM
N_BISECT = 18
KEY_BLOCK = 256
DSA_Q_TILE = 256
ATT_HEAD_GROUP = 8
PART_ROWS = 32
ONES_ROWS = 16
LOG2_E = 1.4426950408889634
NEG_BIG = -1e30


def _dot(a, b):
    return jnp.dot(a, b, preferred_element_type=F32)


def _dot_nt(a, b):
    return lax.dot_general(a, b, (((1,), (1,)), ((), ())), preferred_element_type=F32)


def _split(a):
    hi = a.astype(BF16)
    lo = (a - hi.astype(F32)).astype(BF16)
    return hi, lo


def _dot3(a, b):
    ah, al = _split(a)
    bh, bl = _split(b)
    return _dot(ah, bh) + _dot(al, bh) + _dot(ah, bl)


def _rms(x, g):
    return x * lax.rsqrt(jnp.mean(x * x, axis=-1, keepdims=True) + NORM_EPS) * g


def _sigmoid(x):
    return 1.0 / (1.0 + jnp.exp(-x))


def _mod_kernel(c_ref, w_ref, b_ref, o_ref):
    c = c_ref[...]
    cond = c * _sigmoid(c)
    o_ref[0] = _dot3(cond, w_ref[0]) + b_ref[0]


def _mod_call(c, ada_w, ada_b):
    L, D, D6 = ada_w.shape
    B = c.shape[0]
    nj = D6 // D
    return pl.pallas_call(
        _mod_kernel,
        grid=(L, nj),
        in_specs=[
            pl.BlockSpec((B, D), lambda l, j: (0, 0)),
            pl.BlockSpec((1, D, D), lambda l, j: (l, 0, j)),
            pl.BlockSpec((1, 1, D), lambda l, j: (l, 0, j)),
        ],
        out_specs=pl.BlockSpec((1, B, D), lambda l, j: (l, 0, j)),
        out_shape=jax.ShapeDtypeStruct((L, B, D6), F32),
        compiler_params=pltpu.CompilerParams(vmem_limit_bytes=VMEM_LIMIT),
    )(c, ada_w, ada_b.reshape(L, 1, D6))


def _inproj_kernel(x_ref, mod_ref, g_ref, w_ref, mu_ref, qg_ref, kvg_ref, lng_ref, lnb_ref,
                   rw_ref, cq_ref, ckv_ref, kw_ref, carry_ref):
    j = pl.program_id(1)
    tm = x_ref.shape[1]
    rb = min(INPROJ_ROW_BLOCK, tm)
    mod = mod_ref[0]

    @pl.when(j == 0)
    def _():
        carry_ref[...] = jnp.zeros_like(carry_ref)

    row = lax.broadcasted_iota(jnp.int32, (rb, 1), 0)
    lane = lax.broadcasted_iota(jnp.int32, (1, LANES), 1)
    is_k = lane < IDX_DIM

    for r0 in range(0, tm, rb):
        rs = slice(r0, r0 + rb)
        h = (_rms(x_ref[0, rs, :], g_ref[...]) * (1.0 + mod[1:2]) + mod[0:1]).astype(BF16)

        def proj(lo_, hi_):
            return _dot(h, w_ref[0, :, lo_:hi_])

        for lo_ in range(0, RW_PAD, INPROJ_COL_BLOCK):
            hi_ = min(lo_ + INPROJ_COL_BLOCK, RW_PAD)
            u = proj(lo_, hi_)
            prev = jnp.where(row == 0, carry_ref[0:1, lo_:hi_], pltpu.roll(u, 1, axis=0))
            carry_ref[0:1, lo_:hi_] = u[rb - 1:rb, :]
            rw_ref[0, rs, lo_:hi_] = u + (prev - u) * mu_ref[:, lo_:hi_]

        cq_ref[0, rs, :] = _rms(proj(COL_CQ, COL_CKV), qg_ref[...]).astype(BF16)
        ckv_ref[0, rs, :] = _rms(proj(COL_CKV, COL_KW), kvg_ref[...]).astype(BF16)

        kw = proj(COL_KW, P_PAD)
        mean = jnp.sum(jnp.where(is_k, kw, 0.0), axis=-1, keepdims=True) * (1.0 / IDX_DIM)
        cen = jnp.where(is_k, kw - mean, 0.0)
        var = jnp.sum(cen * cen, axis=-1, keepdims=True) * (1.0 / IDX_DIM)
        kn = cen * lax.rsqrt(var + NORM_EPS) * lng_ref[...] + lnb_ref[...]
        kw_ref[0, rs, :] = jnp.where(is_k, kn, kw * ((IDX_HEADS ** -0.5) * (IDX_DIM ** -0.5)))


def _inproj_call(x, mod, g, w, layer, mu, qg, kvg, lng, lnb, tm):
    B, S, D = x.shape
    const = lambda b, j: (0, 0)
    tok = lambda b, j: (b, j, 0)
    return pl.pallas_call(
        _inproj_kernel,
        grid=(B, S // tm),
        in_specs=[
            pl.BlockSpec((1, tm, D), tok),
            pl.BlockSpec((1, 6, D), lambda b, j: (b, 0, 0)),
            pl.BlockSpec((1, D), const),
            pl.BlockSpec((1, D, P_PAD), lambda b, j: (layer, 0, 0), pipeline_mode=pl.Buffered(1)),
            pl.BlockSpec((1, RW_PAD), const),
            pl.BlockSpec((1, Q_LORA), const),
            pl.BlockSpec((1, KV_LORA), const),
            pl.BlockSpec((1, LANES), const),
            pl.BlockSpec((1, LANES), const),
        ],
        out_specs=[
            pl.BlockSpec((1, tm, RW_PAD), tok),
            pl.BlockSpec((1, tm, Q_LORA), tok),
            pl.BlockSpec((1, tm, KV_LORA), tok),
            pl.BlockSpec((1, tm, LANES), tok),
        ],
        out_shape=[
            jax.ShapeDtypeStruct((B, S, RW_PAD), F32),
            jax.ShapeDtypeStruct((B, S, Q_LORA), BF16),
            jax.ShapeDtypeStruct((B, S, KV_LORA), BF16),
            jax.ShapeDtypeStruct((B, S, LANES), F32),
        ],
        scratch_shapes=[pltpu.VMEM((8, RW_PAD), F32)],
        compiler_params=pltpu.CompilerParams(
            dimension_semantics=("arbitrary", "arbitrary"), vmem_limit_bytes=VMEM_LIMIT),
    )(x, mod, g, w, mu, qg, kvg, lng, lnb)


def _softplus(z):
    return jnp.maximum(z, 0.0) + jnp.log(1.0 + jnp.exp(-jnp.abs(z)))


def _rwkv_kernel(*refs, has_vres):
    if has_vres:
        rw_ref, vf_ref, vec_ref, w2_ref, a2_ref, g2_ref, v2_ref, y_ref, h_ref = refs
    else:
        rw_ref, vec_ref, w2_ref, a2_ref, g2_ref, y_ref, h_ref = refs
    W = RWKV_WIDTH
    N = RWKV_HEAD_DIM
    C = RW_CHUNK
    nb, tcb = rw_ref.shape[0], rw_ref.shape[1]
    tc = nb * tcb

    @pl.when(pl.program_id(1) == 0)
    def _():
        h_ref[...] = jnp.zeros_like(h_ref)

    vec = vec_ref[...]
    w0, a0, v0, k_k, k_a, r_k, lnx_g, lnx_b = [vec[i:i + 1] for i in range(8)]

    def rows(ref, lo_, hi_):
        return ref[:, :, lo_:hi_].reshape(tc, hi_ - lo_)

    r = rows(rw_ref, RW_R, RW_R + W)
    k = rows(rw_ref, RW_K, RW_K + W)
    v = rows(rw_ref, RW_V, RW_V + W)
    wa = rows(rw_ref, RW_WA, RW_GM)
    gm = rows(rw_ref, RW_GM, RW_PAD)

    bd = (lax.broadcasted_iota(jnp.int32, (QUAD, QUAD), 0) // N
          == lax.broadcasted_iota(jnp.int32, (QUAD, QUAD), 1) // N)
    head_ones = bd.astype(BF16)

    def headsum(t, passes=2):
        parts = [t.astype(BF16)]
        if passes == 2:
            parts.append((t - parts[0].astype(F32)).astype(BF16))
        return jnp.concatenate(
            [sum(_dot(p[:, q * QUAD:(q + 1) * QUAD], head_ones) for p in parts)
             for q in range(W // QUAD)], axis=1)

    w_log = -_softplus(-(w0 + _dot3(jnp.tanh(wa), w2_ref[...]))) - 0.5
    logw = -jnp.exp(w_log)
    a = _sigmoid(a0 + _dot(wa.astype(BF16), a2_ref[...]))
    g = _dot(_sigmoid(gm).astype(BF16), g2_ref[...])
    if has_vres:
        v = v + (rows(vf_ref, 0, W) - v) * _sigmoid(v0 + _dot(gm.astype(BF16), v2_ref[...]))
    kk = k * k_k
    kk = kk * lax.rsqrt(jnp.maximum(headsum(kk * kk, passes=1), 1e-24))
    k = k * (1.0 + (a - 1.0) * k_a)

    ti = lax.broadcasted_iota(jnp.int32, (tc, tc), 0)
    tj = lax.broadcasted_iota(jnp.int32, (tc, tc), 1)
    tri = ((ti // C == tj // C) & (tj <= ti)).astype(BF16)
    lh = logw.astype(BF16)
    lm = (logw - lh.astype(F32)).astype(BF16)
    cum = _dot(tri, lh) + _dot(tri, lm)

    bt = kk * jnp.exp(cum - logw)
    at = -(a * kk) * jnp.exp(-cum)
    kt = k * jnp.exp(-cum)
    rt = r * jnp.exp(cum)

    lane_q = lax.broadcasted_iota(jnp.int32, (1, QUAD), 1) // N
    hmask = [lane_q == h for h in range(4)]
    gr = lax.broadcasted_iota(jnp.int32, (2 * C, 2 * QUAD), 0)
    gc = lax.broadcasted_iota(jnp.int32, (2 * C, 2 * QUAD), 1) % C
    gmask = ((gr < C) & (gc < gr)) | ((gr >= C) & (gc <= gr - C))
    pr = lax.broadcasted_iota(jnp.int32, (C, LANES), 0)
    pc = lax.broadcasted_iota(jnp.int32, (C, LANES), 1)
    eye_pad = (pr == pc).astype(F32)
    left = pc < C
    zeros_cq = jnp.zeros((C, QUAD), F32)
    n_ch = tc // C
    n_q = W // QUAD

    cells = [(c, q) for c in range(n_ch) for q in range(n_q)]

    cell = {}
    for c, q in cells:
        rs = slice(c * C, (c + 1) * C)
        cs = slice(q * QUAD, (q + 1) * QUAD)
        tot = cum[c * C + C - 1:c * C + C, cs]
        tail = jnp.exp(tot - cum[rs, cs])
        atail = -(a[rs, cs] * kk[rs, cs]) * tail
        ktail = k[rs, cs] * tail
        dp_t = jnp.broadcast_to(jnp.exp(tot), (LANES, QUAD)).T
        vq = v[rs, cs]
        lhs = jnp.concatenate([bt[rs, cs], rt[rs, cs]], axis=0).astype(BF16)
        rhs_t = jnp.concatenate(
            [jnp.where(hmask[h], t, 0.0) for h in range(4) for t in (kt[rs, cs], at[rs, cs])],
            axis=0).astype(BF16)
        gram = jnp.where(gmask, _dot_nt(lhs, rhs_t), 0.0)
        cell[c, q] = dict(
            ak_t=jnp.concatenate([atail, ktail], axis=0).T.astype(BF16),
            dp_col=jnp.concatenate([dp_t, dp_t], axis=1), vq=vq, lhs=lhs, gram=gram,
            gram_b=gram.astype(BF16),
            v_masked=[jnp.where(hmask[h], vq, 0.0) for h in range(4)])

    pairs = [(c, q, j) for c, q in cells for j in range(2)]
    st = {}
    for c, q, j in pairs:
        for h in (2 * j, 2 * j + 1):
            st[c, q, h] = jnp.where(left, eye_pad, cell[c, q]["gram"][0:C, h * LANES:(h + 1) * LANES])
    zeros_pad = jnp.zeros((C, LANES), F32)
    for _ in range(6):
        for c, q, j in pairs:
            s0, s1 = st[c, q, 2 * j], st[c, q, 2 * j + 1]
            lhs2 = jnp.where(left, pltpu.roll(s0, C, axis=1), s1).astype(BF16)
            rhs2 = jnp.concatenate(
                [jnp.concatenate([s0, zeros_pad], axis=1),
                 jnp.concatenate([zeros_pad, s1], axis=1)], axis=0).astype(BF16)
            res = _dot(lhs2, rhs2)
            st[c, q, 2 * j] = jnp.where(left, s0 + res[:, :LANES], res[:, :LANES])
            st[c, q, 2 * j + 1] = jnp.where(left, s1 + res[:, LANES:], res[:, LANES:])

    for c, q in cells:
        o = cell[c, q]
        o["t_cat"] = jnp.concatenate(
            [jnp.where(left, st[c, q, h], 0.0) for h in range(4)], axis=1).astype(BF16)
        vs0 = jnp.concatenate(
            [t for h in range(4) for t in (o["v_masked"][h], zeros_cq)], axis=0).astype(BF16)
        o["akv"] = _dot(o["gram_b"][0:C], vs0)

    def recur(chunks, hq):
        keys = [(b, q) for b in range(nb) for q in range(n_q)]
        z, u, y = {}, {}, {}
        for b, q in keys:
            z[b, q] = _dot(cell[chunks[b], q]["lhs"], hq[b][q].astype(BF16))
        for b, q in keys:
            o = cell[chunks[b], q]
            xs = z[b, q][0:C] + o["akv"]
            xs_st = jnp.concatenate(
                [t for h in range(4) for t in (jnp.where(hmask[h], xs, 0.0), zeros_cq)],
                axis=0).astype(BF16)
            u[b, q] = _dot(o["t_cat"], xs_st)
        for b, q in keys:
            o = cell[chunks[b], q]
            uv = jnp.concatenate([u[b, q], o["vq"]], axis=0).astype(BF16)
            hq[b][q] = o["dp_col"] * hq[b][q] + jnp.where(bd, _dot(o["ak_t"], uv), 0.0)
        for b, q in keys:
            o = cell[chunks[b], q]
            uv_st = jnp.concatenate(
                [t for h in range(4) for t in (o["v_masked"][h], jnp.where(hmask[h], u[b, q], 0.0))],
                axis=0).astype(BF16)
            y[b, q] = z[b, q][C:2 * C] + _dot(o["gram_b"][C:2 * C], uv_st)
        return [jnp.concatenate([y[b, q] for q in range(n_q)], axis=1) for b in range(nb)]

    per_seq = tcb // C
    hq = [[h_ref[b * n_q + q] for q in range(n_q)] for b in range(nb)]
    y_rows = [None] * n_ch
    for s_ in range(per_seq):
        ys = recur([b * per_seq + s_ for b in range(nb)], hq)
        for b in range(nb):
            y_rows[b * per_seq + s_] = ys[b]
    for b in range(nb):
        for q in range(n_q):
            h_ref[b * n_q + q] = hq[b][q]
    y = jnp.concatenate(y_rows, axis=0)

    mu = headsum(y) * (1.0 / N)
    yc = y - mu
    var = headsum(yc * yc) * (1.0 / N)
    yn = yc * lax.rsqrt(var + LNX_EPS) * lnx_g + lnx_b
    bonus = headsum(r * k * r_k, passes=1) * v
    y_ref[...] = ((yn + bonus) * g).astype(BF16).reshape(nb, tcb, W)


def _rwkv_call(rw, vfirst_src, vec, w2p, a2p, g2p, v2p):
    B, S, _ = rw.shape
    has_vres = vfirst_src is not None
    nb = min(RW_SEQS, B)
    tc = RW_TILE // nb
    const = lambda b, j: (0, 0)
    in_specs = [pl.BlockSpec((nb, tc, RW_PAD), lambda b, j: (b, j, 0))]
    args = [rw]
    if has_vres:
        in_specs.append(pl.BlockSpec((nb, tc, RWKV_WIDTH), lambda b, j: (b, j, RW_V // RWKV_WIDTH)))
        args.append(vfirst_src)
    in_specs += [pl.BlockSpec((8, RWKV_WIDTH), const),
                 pl.BlockSpec((LANES, RWKV_WIDTH), const),
                 pl.BlockSpec((LANES, RWKV_WIDTH), const),
                 pl.BlockSpec((2 * LANES, RWKV_WIDTH), const)]
    args += [vec, w2p, a2p, g2p]
    if has_vres:
        in_specs.append(pl.BlockSpec((2 * LANES, RWKV_WIDTH), const))
        args.append(v2p)
    return pl.pallas_call(
        functools.partial(_rwkv_kernel, has_vres=has_vres),
        grid=(B // nb, S // tc),
        in_specs=in_specs,
        out_specs=pl.BlockSpec((nb, tc, RWKV_WIDTH), lambda b, j: (b, j, 0)),
        out_shape=jax.ShapeDtypeStruct((B, S, RWKV_WIDTH), BF16),
        scratch_shapes=[pltpu.VMEM((nb * (RWKV_WIDTH // QUAD), QUAD, QUAD), F32)],
        compiler_params=pltpu.CompilerParams(
            dimension_semantics=("arbitrary", "arbitrary"), vmem_limit_bytes=VMEM_LIMIT),
    )(*args)


def _part(t, op):
    r, n = t.shape
    return op(t.reshape(r // PART_ROWS, PART_ROWS, n), axis=0)


def _sweep(n, body, init, pair=None):
    if pair is None:
        def pair(j, c):
            return body(2 * j + 1, body(2 * j, c))
    c = lax.fori_loop(0, n // 2, pair, init)
    return lax.cond(n % 2 == 1, lambda c: body(n - 1, c), lambda c: c, c)


def _dsa_kernel(cq_ref, kwq_ref, kw_ref, ckv_ref, wqt_ref, wk_ref, wqit_ref, wvt_ref, y_ref,
                kidx_s, ckvt_s, sc_s, bias_s, att_s, yt_s, *, top_k):
    i = pl.program_id(1)
    nblk = kidx_s.shape[0]
    QB = DSA_Q_TILE
    KB = KEY_BLOCK
    HG = ATT_HEAD_GROUP

    @pl.when(i == 0)
    def _():
        for b in range(nblk):
            kidx_s[b] = kw_ref[0, b * KB:(b + 1) * KB, :].astype(BF16)
            ckvt_s[b, :KV_LORA, :] = ckv_ref[0, b * KB:(b + 1) * KB, :].astype(F32).T.astype(BF16)
            ckvt_s[b, KV_LORA:, :] = jnp.ones((ONES_ROWS, KB), BF16)

    cq_t = cq_ref[0].astype(F32).T.astype(BF16)
    q_t = _dot(wqt_ref[...], cq_t).astype(BF16)
    w_t = kwq_ref[0].T
    qi_all = jnp.concatenate([_dot(wqit_ref[h], cq_t) for h in range(IDX_HEADS)],
                             axis=1).astype(BF16)
    w_all = jnp.concatenate([w_t[IDX_DIM + h:IDX_DIM + h + 1, :] for h in range(IDX_HEADS)],
                            axis=1)
    scale = DSA_HEAD_DIM ** -0.5 * LOG2_E
    dh = DSA_HEAD_DIM
    qa_groups = [
        jnp.concatenate(
            [_dot(wk_ref[g * HG + h], q_t[(g * HG + h) * dh:(g * HG + h + 1) * dh]) * scale
             for h in range(HG)], axis=1).astype(BF16)
        for g in range(DSA_HEADS // HG)]

    nkb = (i * QB + QB + KB - 1) // KB
    q_chunk = (i * QB + lax.broadcasted_iota(jnp.int32, (1, QB), 1)) // CHUNK
    k_row = lax.broadcasted_iota(jnp.int32, (KB, 1), 0)
    k_eff = jnp.minimum(top_k, (q_chunk + 1) * CHUNK).astype(F32)
    inf = jnp.float32(jnp.inf)
    one = jnp.float32(1.0)
    zero = jnp.float32(0.0)

    def fin_min(t):
        return jnp.min(t, axis=0, keepdims=True)

    def fin_max(t):
        return jnp.max(t, axis=0, keepdims=True)

    def fin_sum(t):
        return jnp.sum(t, axis=0, keepdims=True)

    def score_body(kb, carry):
        lo, hi = carry
        lg = jnp.maximum(_dot(kidx_s[kb], qi_all), zero) * w_all
        acc = jnp.zeros((KB, QB), F32)
        for h in range(IDX_HEADS):
            acc = acc + lg[:, h * QB:(h + 1) * QB]
        adm = (kb * KB + k_row) // CHUNK <= q_chunk
        sc_s[kb] = jnp.where(adm, acc, -inf)
        lo = jnp.minimum(lo, _part(jnp.where(adm, acc, inf), jnp.min))
        hi = jnp.maximum(hi, _part(jnp.where(adm, acc, -inf), jnp.max))
        return lo, hi

    lo, hi = _sweep(nkb,score_body,
                           (jnp.full((PART_ROWS, QB), inf, F32), jnp.full((PART_ROWS, QB), -inf, F32)))
    lo = fin_min(lo)
    hi = fin_max(hi)

    def count_ge(t):
        def body(kb, acc):
            return acc + _part(jnp.where(sc_s[kb] >= t, one, zero), jnp.sum)
        return fin_sum(_sweep(nkb,body, jnp.zeros((PART_ROWS, QB), F32)))

    def bisect(_, carry):
        lo, hi = carry
        mid = 0.5 * lo + 0.5 * hi
        ge = count_ge(mid) >= k_eff
        return jnp.where(ge, mid, lo), jnp.where(ge, hi, mid)

    lo, hi = lax.fori_loop(0, N_BISECT, bisect, (lo, hi))

    def min_ge(t):
        def body(kb, acc):
            s = sc_s[kb]
            return jnp.minimum(acc, _part(jnp.where(s >= t, s, inf), jnp.min))
        return fin_min(_sweep(nkb, body, jnp.full((PART_ROWS, QB), inf, F32)))

    def walk_cond(st):
        return st[2] > 0

    def walk_body(st):
        m, _, _ = st

        def body(kb, carry):
            cnt, nxt = carry
            s = sc_s[kb]
            gt = s > m
            return (cnt + _part(jnp.where(gt, one, zero), jnp.sum),
                    jnp.minimum(nxt, _part(jnp.where(gt, s, inf), jnp.min)))

        cnt, nxt = _sweep(nkb,body,
                                 (jnp.zeros((PART_ROWS, QB), F32), jnp.full((PART_ROWS, QB), inf, F32)))
        n_gt = fin_sum(cnt)
        done = n_gt < k_eff
        pending = jnp.max(jnp.where(done, 0, 1))
        return jnp.where(done, m, fin_min(nxt)), n_gt, pending

    thr, n_gt, _ = lax.while_loop(
        walk_cond, walk_body, (min_ge(lo), jnp.zeros((1, QB), F32), jnp.int32(1)))
    need = k_eff - n_gt

    tri = (lax.broadcasted_iota(jnp.int32, (KB, KB), 1)
           <= lax.broadcasted_iota(jnp.int32, (KB, KB), 0)).astype(BF16)

    def mask_body(kb, carry):
        s = sc_s[kb]
        tie = jnp.where(s == thr, one, zero)
        pre = _dot(tri, tie.astype(BF16)) + carry
        sel = (s > thr) | ((s == thr) & (pre <= need))
        bias_s[kb] = jnp.where(sel, zero, jnp.float32(NEG_BIG))
        return carry + fin_sum(_part(tie, jnp.sum))

    _sweep(nkb, mask_body, jnp.zeros((1, QB), F32))

    for g in range(DSA_HEADS // HG):
        qa_all = qa_groups[g]

        def att_body(kb, mx):
            kv = ckv_ref[0, pl.ds(pl.multiple_of(kb * KB, KB), KB), :]
            lg = _dot(kv, qa_all)
            b = bias_s[kb]
            parts = []
            for h in range(HG):
                att = lg[:, h * QB:(h + 1) * QB] + b
                att_s[kb, :, h * QB:(h + 1) * QB] = att
                parts.append(_part(att, jnp.max))
            return jnp.maximum(mx, jnp.concatenate(parts, axis=1))

        mx = fin_max(_sweep(nkb, att_body, jnp.full((PART_ROWS, HG * QB), NEG_BIG, F32)))

        def prob(kb):
            return jnp.exp2(att_s[kb] - mx).astype(BF16)

        def pv_body(kb, o):
            return o + _dot(ckvt_s[kb], prob(kb))

        def pv_pair(j, o):
            return o + _dot(jnp.concatenate([ckvt_s[2 * j], ckvt_s[2 * j + 1]], axis=1),
                            jnp.concatenate([prob(2 * j), prob(2 * j + 1)], axis=0))

        o = _sweep(nkb, pv_body, jnp.zeros((KV_LORA + ONES_ROWS, HG * QB), F32), pv_pair)
        o = (o[:KV_LORA] / o[KV_LORA:KV_LORA + 1]).astype(BF16)
        for h in range(HG):
            hh = g * HG + h
            yt_s[hh * dh:(hh + 1) * dh, :] = _dot(wvt_ref[hh], o[:, h * QB:(h + 1) * QB])

    y_ref[0] = yt_s[...].T.astype(BF16)


def _dsa_call(cq, kw, ckv, wqt, wk, wqit, wvt):
    B, S, _ = cq.shape
    QB = DSA_Q_TILE
    KB = KEY_BLOCK
    nblk = S // KB
    top_k = min(TOPK_MAX, S // 4)
    c2 = lambda b, i: (0, 0)
    c3 = lambda b, i: (0, 0, 0)
    return pl.pallas_call(
        functools.partial(_dsa_kernel, top_k=top_k),
        grid=(B, S // QB),
        in_specs=[
            pl.BlockSpec((1, QB, Q_LORA), lambda b, i: (b, i, 0)),
            pl.BlockSpec((1, QB, LANES), lambda b, i: (b, i, 0)),
            pl.BlockSpec((1, S, LANES), lambda b, i: (b, 0, 0)),
            pl.BlockSpec((1, S, KV_LORA), lambda b, i: (b, 0, 0)),
            pl.BlockSpec((DSA_WIDTH, Q_LORA), c2),
            pl.BlockSpec((DSA_HEADS, KV_LORA, DSA_HEAD_DIM), c3),
            pl.BlockSpec((IDX_HEADS, LANES, Q_LORA), c3),
            pl.BlockSpec((DSA_HEADS, DSA_HEAD_DIM, KV_LORA), c3),
        ],
        out_specs=pl.BlockSpec((1, QB, DSA_WIDTH), lambda b, i: (b, i, 0)),
        out_shape=jax.ShapeDtypeStruct((B, S, DSA_WIDTH), BF16),
        scratch_shapes=[
            pltpu.VMEM((nblk, KB, LANES), BF16),
            pltpu.VMEM((nblk, KV_LORA + ONES_ROWS, KB), BF16),
            pltpu.VMEM((nblk, KB, QB), F32),
            pltpu.VMEM((nblk, KB, QB), F32),
            pltpu.VMEM((nblk, KB, ATT_HEAD_GROUP * QB), F32),
            pltpu.VMEM((DSA_WIDTH, QB), F32),
        ],
        compiler_params=pltpu.CompilerParams(
            dimension_semantics=("arbitrary", "arbitrary"), vmem_limit_bytes=VMEM_LIMIT),
    )(cq, kw, kw, ckv, wqt, wk, wqit, wvt)


def _out_ffn_kernel(x_ref, yd_ref, yr_ref, mod_ref, gpm_ref, gff_ref, gpf_ref,
                    wo_ref, wfc_ref, wdn_ref, o_ref):
    mod = mod_ref[0]
    y = _dot(yd_ref[0], wo_ref[0, 0]) + _dot(yr_ref[0], wo_ref[0, 1])
    x1 = x_ref[0] + mod[2:3] * _rms(y, gpm_ref[...])
    h = (_rms(x1, gff_ref[...]) * (1.0 + mod[4:5]) + mod[3:4]).astype(BF16)
    _, nc, fc, _ = wdn_ref.shape
    acc = jnp.zeros_like(x1)
    for c in range(nc):
        gate = _dot(h, wfc_ref[0, :, c * fc:(c + 1) * fc])
        up = _dot(h, wfc_ref[0, :, (nc + c) * fc:(nc + c + 1) * fc])
        acc = acc + _dot((gate * _sigmoid(gate) * up).astype(BF16), wdn_ref[0, c])
    o_ref[0] = x1 + mod[5:6] * _rms(acc, gpf_ref[...])


def _out_ffn_call(x, yd, yr, mod, gpm, gff, gpf, wo, wfc, wdn, layer, tm):
    B, S, D = x.shape
    _, nc, fc, _ = wdn.shape
    lay3 = lambda b, j: (layer, 0, 0)
    lay4 = lambda b, j: (layer, 0, 0, 0)
    tok = lambda b, j: (b, j, 0)
    c2 = lambda b, j: (0, 0)
    c3 = lambda b, j: (0, 0, 0)
    return pl.pallas_call(
        _out_ffn_kernel,
        grid=(B, S // tm),
        in_specs=[
            pl.BlockSpec((1, tm, D), tok),
            pl.BlockSpec((1, tm, DSA_WIDTH), tok),
            pl.BlockSpec((1, tm, RWKV_WIDTH), tok),
            pl.BlockSpec((1, 6, D), lambda b, j: (b, 0, 0)),
            pl.BlockSpec((1, D), c2),
            pl.BlockSpec((1, D), c2),
            pl.BlockSpec((1, D), c2),
            pl.BlockSpec((1, 2, DSA_WIDTH, D), lay4, pipeline_mode=pl.Buffered(1)),
            pl.BlockSpec((1, D, 2 * nc * fc), lay3, pipeline_mode=pl.Buffered(1)),
            pl.BlockSpec((1, nc, fc, D), lay4, pipeline_mode=pl.Buffered(1)),
        ],
        out_specs=pl.BlockSpec((1, tm, D), tok),
        out_shape=jax.ShapeDtypeStruct((B, S, D), F32),
        compiler_params=pltpu.CompilerParams(
            dimension_semantics=("arbitrary", "arbitrary"), vmem_limit_bytes=VMEM_LIMIT),
    )(x, yd, yr, mod, gpm, gff, gpf, wo, wfc, wdn)


def _pad_rows(w, rows, at):
    out = jnp.zeros((rows,) + w.shape[1:], w.dtype)
    return out.at[at:at + w.shape[0]].set(w)


def kernel(x, c, ada_w, ada_b, pre_g_mix, post_g_mix, pre_g_ffn, post_g_ffn, w_in, w_in_vres, mu_shift, mu_vres, w_out, q_norm_g, kv_norm_g, w_q_up, w_qi_up, w_k_up, w_v_up, kidx_ln_g, kidx_ln_b, w0, w2, a0, a2, g2, v0, v2, k_k, k_a, r_k, lnx_g, lnx_b, w_fc, w_down):
    B, S, D = x.shape
    L = w_in.shape[0]
    n_dsa = Q_LORA + KV_LORA + IDX_DIM + IDX_HEADS
    tm = min(512, S)
    ff_chunk = FF_CHUNK

    mod_all = _mod_call(c, ada_w, ada_b).reshape(L, B, 6, D)

    row = lambda t: t.reshape(1, -1)
    lane_pad = lambda t: jnp.pad(t, (0, LANES - t.shape[0])).reshape(1, LANES)

    n_rw = w_in.shape[2] - n_dsa
    zpad = lambda n: jnp.zeros((L, D, n), F32)
    mv_w = jnp.concatenate([jnp.zeros((1, D, MV_LORA), F32), w_in_vres], axis=0)
    w_in_all = jnp.concatenate(
        [w_in[:, :, n_dsa:], mv_w, zpad(RW_PAD - n_rw - MV_LORA),
         w_in[:, :, :n_dsa], zpad(P_PAD - COL_KW - IDX_DIM - IDX_HEADS)], axis=2).astype(BF16)
    mv_mu = jnp.concatenate([jnp.zeros((1, MV_LORA), F32), mu_vres], axis=0)
    mu_all = jnp.concatenate(
        [mu_shift, mv_mu, jnp.zeros((L, RW_PAD - n_rw - MV_LORA), F32)], axis=1)
    wo_all = w_out.astype(BF16).reshape(L, 2, DSA_WIDTH, D)
    wfc_all = w_fc.astype(BF16)
    wdn_all = w_down.astype(BF16).reshape(L, D_FF // ff_chunk, ff_chunk, D)

    v_first_src = None
    for l in range(L):
        rw, cq, ckv, kw = _inproj_call(
            x, mod_all[l], row(pre_g_mix[l]), w_in_all, l, mu_all[l:l + 1], row(q_norm_g[l]),
            row(kv_norm_g[l]), lane_pad(kidx_ln_g[l]), lane_pad(kidx_ln_b[l]), min(INPROJ_TILE, S))

        vec = jnp.stack([w0[l], a0[l], v0[l - 1] if l else jnp.zeros_like(w0[l]), k_k[l], k_a[l],
                         r_k[l].reshape(-1), lnx_g[l], lnx_b[l]])
        w2p = _pad_rows(w2[l], LANES, 0)
        a2p = _pad_rows(a2[l], LANES, DECAY_LORA).astype(BF16)
        g2p = _pad_rows(g2[l], 2 * LANES, 0).astype(BF16)
        v2p = _pad_rows(v2[l - 1], 2 * LANES, GATE_LORA).astype(BF16) if l else None
        y_rwkv = _rwkv_call(rw, v_first_src, vec, w2p, a2p, g2p, v2p)
        if l == 0:
            v_first_src = rw

        wqt = w_q_up[l].reshape(Q_LORA, DSA_WIDTH).T.astype(BF16)
        wk = jnp.transpose(w_k_up[l], (1, 0, 2)).astype(BF16)
        wqit = jnp.pad(jnp.transpose(w_qi_up[l], (1, 2, 0)),
                       ((0, 0), (0, LANES - IDX_DIM), (0, 0))).astype(BF16)
        wvt = jnp.transpose(w_v_up[l], (1, 2, 0)).astype(BF16)
        y_dsa = _dsa_call(cq, kw, ckv, wqt, wk, wqit, wvt)

        x = _out_ffn_call(x, y_dsa, y_rwkv, mod_all[l], row(post_g_mix[l]), row(pre_g_ffn[l]),
                          row(post_g_ffn[l]), wo_all, wfc_all, wdn_all, l, tm)
    return x
```

```python
import functools

import jax
import jax.numpy as jnp
from jax import lax
from jax.experimental import pallas as pl
from jax.experimental.pallas import tpu as pltpu

F32 = jnp.float32
BF16 = jnp.bfloat16

D_MODEL = 1024
CHUNK = 64
DSA_HEADS = 8
DSA_HEAD_DIM = 64
DSA_WIDTH = DSA_HEADS * DSA_HEAD_DIM
Q_LORA = 256
KV_LORA = 128
IDX_HEADS = 4
IDX_DIM = 64
TOPK_MAX = 256
RWKV_HEADS = 8
RWKV_HEAD_DIM = 64
RWKV_WIDTH = RWKV_HEADS * RWKV_HEAD_DIM
DECAY_LORA = 64
AAA_LORA = 64
MV_LORA = 32
GATE_LORA = 160
D_FF = ((8 * D_MODEL + 3 * 256 - 1) // (3 * 256)) * 256
NORM_EPS = 1e-6
LNX_EPS = 64e-5

LANES = 128
VMEM_LIMIT = 56 * 1024 * 1024

RW_R, RW_K, RW_V = 0, RWKV_WIDTH, 2 * RWKV_WIDTH
RW_WA = 3 * RWKV_WIDTH
RW_GM = RW_WA + DECAY_LORA + AAA_LORA
RW_PAD = RW_GM + 2 * LANES
COL_CQ = RW_PAD
COL_CKV = COL_CQ + Q_LORA
COL_KW = COL_CKV + KV_LORA
P_PAD = COL_KW + LANES

FFN_ROW_BLOCK = 512
FFN_TILE = 1024
FF_CHUNK = 256
INPROJ_COL_BLOCK = 512
INPROJ_ROW_BLOCK = 512
INPROJ_TILE = 1024
RW_CHUNK = 64
RW_TILE = 256
RW_SEQS = 4
QUAD = 4 * RWKV_HEAD_DIM
N_BISECT = 18
KEY_BLOCK = 256
DSA_Q_TILE = 256
ATT_HEAD_GROUP = 8
PART_ROWS = 32
ONES_ROWS = 16
LOG2_E = 1.4426950408889634
NEG_BIG = -1e30


def _dot(a, b):
    return jnp.dot(a, b, preferred_element_type=F32)


def _dot_nt(a, b):
    return lax.dot_general(a, b, (((1,), (1,)), ((), ())), preferred_element_type=F32)


def _split(a):
    hi = a.astype(BF16)
    lo = (a - hi.astype(F32)).astype(BF16)
    return hi, lo


def _dot3(a, b):
    ah, al = _split(a)
    bh, bl = _split(b)
    return _dot(ah, bh) + _dot(al, bh) + _dot(ah, bl)


def _rms(x, g):
    return x * lax.rsqrt(jnp.mean(x * x, axis=-1, keepdims=True) + NORM_EPS) * g


def _sigmoid(x):
    return 1.0 / (1.0 + jnp.exp(-x))


def _mod_kernel(c_ref, w_ref, b_ref, o_ref):
    c = c_ref[...]
    cond = c * _sigmoid(c)
    o_ref[0] = _dot3(cond, w_ref[0]) + b_ref[0]


def _mod_call(c, ada_w, ada_b):
    L, D, D6 = ada_w.shape
    B = c.shape[0]
    nj = D6 // D
    return pl.pallas_call(
        _mod_kernel,
        grid=(L, nj),
        in_specs=[
            pl.BlockSpec((B, D), lambda l, j: (0, 0)),
            pl.BlockSpec((1, D, D), lambda l, j: (l, 0, j)),
            pl.BlockSpec((1, 1, D), lambda l, j: (l, 0, j)),
        ],
        out_specs=pl.BlockSpec((1, B, D), lambda l, j: (l, 0, j)),
        out_shape=jax.ShapeDtypeStruct((L, B, D6), F32),
        compiler_params=pltpu.CompilerParams(vmem_limit_bytes=VMEM_LIMIT),
    )(c, ada_w, ada_b.reshape(L, 1, D6))


def _inproj_kernel(x_ref, mod_ref, g_ref, w_ref, mu_ref, qg_ref, kvg_ref, lng_ref, lnb_ref,
                   rw_ref, cq_ref, ckv_ref, kw_ref, carry_ref):
    j = pl.program_id(1)
    tm = x_ref.shape[1]
    rb = min(INPROJ_ROW_BLOCK, tm)
    mod = mod_ref[0]

    @pl.when(j == 0)
    def _():
        carry_ref[...] = jnp.zeros_like(carry_ref)

    row = lax.broadcasted_iota(jnp.int32, (rb, 1), 0)
    lane = lax.broadcasted_iota(jnp.int32, (1, LANES), 1)
    is_k = lane < IDX_DIM

    for r0 in range(0, tm, rb):
        rs = slice(r0, r0 + rb)
        h = (_rms(x_ref[0, rs, :], g_ref[...]) * (1.0 + mod[1:2]) + mod[0:1]).astype(BF16)

        def proj(lo_, hi_):
            return _dot(h, w_ref[0, :, lo_:hi_])

        for lo_ in range(0, RW_PAD, INPROJ_COL_BLOCK):
            hi_ = min(lo_ + INPROJ_COL_BLOCK, RW_PAD)
            u = proj(lo_, hi_)
            prev = jnp.where(row == 0, carry_ref[0:1, lo_:hi_], pltpu.roll(u, 1, axis=0))
            carry_ref[0:1, lo_:hi_] = u[rb - 1:rb, :]
            rw_ref[0, rs, lo_:hi_] = u + (prev - u) * mu_ref[:, lo_:hi_]

        cq_ref[0, rs, :] = _rms(proj(COL_CQ, COL_CKV), qg_ref[...]).astype(BF16)
        ckv_ref[0, rs, :] = _rms(proj(COL_CKV, COL_KW), kvg_ref[...]).astype(BF16)

        kw = proj(COL_KW, P_PAD)
        mean = jnp.sum(jnp.where(is_k, kw, 0.0), axis=-1, keepdims=True) * (1.0 / IDX_DIM)
        cen = jnp.where(is_k, kw - mean, 0.0)
        var = jnp.sum(cen * cen, axis=-1, keepdims=True) * (1.0 / IDX_DIM)
        kn = cen * lax.rsqrt(var + NORM_EPS) * lng_ref[...] + lnb_ref[...]
        kw_ref[0, rs, :] = jnp.where(is_k, kn, kw * ((IDX_HEADS ** -0.5) * (IDX_DIM ** -0.5)))


def _inproj_call(x, mod, g, w, layer, mu, qg, kvg, lng, lnb, tm):
    B, S, D = x.shape
    const = lambda b, j: (0, 0)
    tok = lambda b, j: (b, j, 0)
    return pl.pallas_call(
        _inproj_kernel,
        grid=(B, S // tm),
        in_specs=[
            pl.BlockSpec((1, tm, D), tok),
            pl.BlockSpec((1, 6, D), lambda b, j: (b, 0, 0)),
            pl.BlockSpec((1, D), const),
            pl.BlockSpec((1, D, P_PAD), lambda b, j: (layer, 0, 0), pipeline_mode=pl.Buffered(1)),
            pl.BlockSpec((1, RW_PAD), const),
            pl.BlockSpec((1, Q_LORA), const),
            pl.BlockSpec((1, KV_LORA), const),
            pl.BlockSpec((1, LANES), const),
            pl.BlockSpec((1, LANES), const),
        ],
        out_specs=[
            pl.BlockSpec((1, tm, RW_PAD), tok),
            pl.BlockSpec((1, tm, Q_LORA), tok),
            pl.BlockSpec((1, tm, KV_LORA), tok),
            pl.BlockSpec((1, tm, LANES), tok),
        ],
        out_shape=[
            jax.ShapeDtypeStruct((B, S, RW_PAD), F32),
            jax.ShapeDtypeStruct((B, S, Q_LORA), BF16),
            jax.ShapeDtypeStruct((B, S, KV_LORA), BF16),
            jax.ShapeDtypeStruct((B, S, LANES), F32),
        ],
        scratch_shapes=[pltpu.VMEM((8, RW_PAD), F32)],
        compiler_params=pltpu.CompilerParams(
            dimension_semantics=("arbitrary", "arbitrary"), vmem_limit_bytes=VMEM_LIMIT),
    )(x, mod, g, w, mu, qg, kvg, lng, lnb)


def _softplus(z):
    return jnp.maximum(z, 0.0) + jnp.log(1.0 + jnp.exp(-jnp.abs(z)))


def _rwkv_kernel(*refs, has_vres):
    if has_vres:
        rw_ref, vf_ref, vec_ref, w2_ref, a2_ref, g2_ref, v2_ref, y_ref, h_ref = refs
    else:
        rw_ref, vec_ref, w2_ref, a2_ref, g2_ref, y_ref, h_ref = refs
    W = RWKV_WIDTH
    N = RWKV_HEAD_DIM
    C = RW_CHUNK
    nb, tcb = rw_ref.shape[0], rw_ref.shape[1]
    tc = nb * tcb

    @pl.when(pl.program_id(1) == 0)
    def _():
        h_ref[...] = jnp.zeros_like(h_ref)

    vec = vec_ref[...]
    w0, a0, v0, k_k, k_a, r_k, lnx_g, lnx_b = [vec[i:i + 1] for i in range(8)]

    def rows(ref, lo_, hi_):
        return ref[:, :, lo_:hi_].reshape(tc, hi_ - lo_)

    r = rows(rw_ref, RW_R, RW_R + W)
    k = rows(rw_ref, RW_K, RW_K + W)
    v = rows(rw_ref, RW_V, RW_V + W)
    wa = rows(rw_ref, RW_WA, RW_GM)
    gm = rows(rw_ref, RW_GM, RW_PAD)

    bd = (lax.broadcasted_iota(jnp.int32, (QUAD, QUAD), 0) // N
          == lax.broadcasted_iota(jnp.int32, (QUAD, QUAD), 1) // N)
    head_ones = bd.astype(BF16)

    def headsum(t):
        tb = t.astype(BF16)
        return jnp.concatenate(
            [_dot(tb[:, q * QUAD:(q + 1) * QUAD], head_ones) for q in range(W // QUAD)], axis=1)

    w_log = -_softplus(-(w0 + _dot3(jnp.tanh(wa), w2_ref[...]))) - 0.5
    logw = -jnp.exp(w_log)
    a = _sigmoid(a0 + _dot(wa.astype(BF16), a2_ref[...]))
    g = _dot(_sigmoid(gm).astype(BF16), g2_ref[...])
    if has_vres:
        v = v + (rows(vf_ref, 0, W) - v) * _sigmoid(v0 + _dot(gm.astype(BF16), v2_ref[...]))
    kk = k * k_k
    kk = kk * lax.rsqrt(jnp.maximum(headsum(kk * kk), 1e-24))
    k = k * (1.0 + (a - 1.0) * k_a)

    ti = lax.broadcasted_iota(jnp.int32, (tc, tc), 0)
    tj = lax.broadcasted_iota(jnp.int32, (tc, tc), 1)
    tri = ((ti // C == tj // C) & (tj <= ti)).astype(BF16)
    lh = logw.astype(BF16)
    lm = (logw - lh.astype(F32)).astype(BF16)
    cum = _dot(tri, lh) + _dot(tri, lm)

    bt = kk * jnp.exp(cum - logw)
    at = -(a * kk) * jnp.exp(-cum)
    kt = k * jnp.exp(-cum)
    rt = r * jnp.exp(cum)

    lane_q = lax.broadcasted_iota(jnp.int32, (1, QUAD), 1) // N
    hmask = [lane_q == h for h in range(4)]
    gr = lax.broadcasted_iota(jnp.int32, (2 * C, 2 * QUAD), 0)
    gc = lax.broadcasted_iota(jnp.int32, (2 * C, 2 * QUAD), 1) % C
    gmask = ((gr < C) & (gc < gr)) | ((gr >= C) & (gc <= gr - C))
    pr = lax.broadcasted_iota(jnp.int32, (C, LANES), 0)
    pc = lax.broadcasted_iota(jnp.int32, (C, LANES), 1)
    eye_pad = (pr == pc).astype(F32)
    left = pc < C
    zeros_cq = jnp.zeros((C, QUAD), F32)
    n_ch = tc // C
    n_q = W // QUAD

    cells = [(c, q) for c in range(n_ch) for q in range(n_q)]

    cell = {}
    for c, q in cells:
        rs = slice(c * C, (c + 1) * C)
        cs = slice(q * QUAD, (q + 1) * QUAD)
        tot = cum[c * C + C - 1:c * C + C, cs]
        tail = jnp.exp(tot - cum[rs, cs])
        atail = -(a[rs, cs] * kk[rs, cs]) * tail
        ktail = k[rs, cs] * tail
        dp_t = jnp.broadcast_to(jnp.exp(tot), (LANES, QUAD)).T
        vq = v[rs, cs]
        lhs = jnp.concatenate([bt[rs, cs], rt[rs, cs]], axis=0).astype(BF16)
        rhs_t = jnp.concatenate(
            [jnp.where(hmask[h], t, 0.0) for h in range(4) for t in (kt[rs, cs], at[rs, cs])],
            axis=0).astype(BF16)
        gram = jnp.where(gmask, _dot_nt(lhs, rhs_t), 0.0)
        cell[c, q] = dict(
            ak_t=jnp.concatenate([atail, ktail], axis=0).T.astype(BF16),
            dp_col=jnp.concatenate([dp_t, dp_t], axis=1), vq=vq, lhs=lhs, gram=gram,
            gram_b=gram.astype(BF16),
            v_masked=[jnp.where(hmask[h], vq, 0.0) for h in range(4)])

    pairs = [(c, q, j) for c, q in cells for j in range(2)]
    st = {}
    for c, q, j in pairs:
        for h in (2 * j, 2 * j + 1):
            st[c, q, h] = jnp.where(left, eye_pad, cell[c, q]["gram"][0:C, h * LANES:(h + 1) * LANES])
    zeros_pad = jnp.zeros((C, LANES), F32)
    for _ in range(6):
        for c, q, j in pairs:
            s0, s1 = st[c, q, 2 * j], st[c, q, 2 * j + 1]
            lhs2 = jnp.where(left, pltpu.roll(s0, C, axis=1), s1).astype(BF16)
            rhs2 = jnp.concatenate(
                [jnp.concatenate([s0, zeros_pad], axis=1),
                 jnp.concatenate([zeros_pad, s1], axis=1)], axis=0).astype(BF16)
            res = _dot(lhs2, rhs2)
            st[c, q, 2 * j] = jnp.where(left, s0 + res[:, :LANES], res[:, :LANES])
            st[c, q, 2 * j + 1] = jnp.where(left, s1 + res[:, LANES:], res[:, LANES:])

    for c, q in cells:
        o = cell[c, q]
        o["t_cat"] = jnp.concatenate(
            [jnp.where(left, st[c, q, h], 0.0) for h in range(4)], axis=1).astype(BF16)
        vs0 = jnp.concatenate(
            [t for h in range(4) for t in (o["v_masked"][h], zeros_cq)], axis=0).astype(BF16)
        o["akv"] = _dot(o["gram_b"][0:C], vs0)

    def recur(chunks, hq):
        keys = [(b, q) for b in range(nb) for q in range(n_q)]
        z, u, y = {}, {}, {}
        for b, q in keys:
            z[b, q] = _dot(cell[chunks[b], q]["lhs"], hq[b][q].astype(BF16))
        for b, q in keys:
            o = cell[chunks[b], q]
            xs = z[b, q][0:C] + o["akv"]
            xs_st = jnp.concatenate(
                [t for h in range(4) for t in (jnp.where(hmask[h], xs, 0.0), zeros_cq)],
                axis=0).astype(BF16)
            u[b, q] = _dot(o["t_cat"], xs_st)
        for b, q in keys:
            o = cell[chunks[b], q]
            uv = jnp.concatenate([u[b, q], o["vq"]], axis=0).astype(BF16)
            hq[b][q] = o["dp_col"] * hq[b][q] + jnp.where(bd, _dot(o["ak_t"], uv), 0.0)
        for b, q in keys:
            o = cell[chunks[b], q]
            uv_st = jnp.concatenate(
                [t for h in range(4) for t in (o["v_masked"][h], jnp.where(hmask[h], u[b, q], 0.0))],
                axis=0).astype(BF16)
            y[b, q] = z[b, q][C:2 * C] + _dot(o["gram_b"][C:2 * C], uv_st)
        return [jnp.concatenate([y[b, q] for q in range(n_q)], axis=1) for b in range(nb)]

    per_seq = tcb // C
    hq = [[h_ref[b * n_q + q] for q in range(n_q)] for b in range(nb)]
    y_rows = [None] * n_ch
    for s_ in range(per_seq):
        ys = recur([b * per_seq + s_ for b in range(nb)], hq)
        for b in range(nb):
            y_rows[b * per_seq + s_] = ys[b]
    for b in range(nb):
        for q in range(n_q):
            h_ref[b * n_q + q] = hq[b][q]
    y = jnp.concatenate(y_rows, axis=0)

    mu = headsum(y) * (1.0 / N)
    yc = y - mu
    var = headsum(yc * yc) * (1.0 / N)
    yn = yc * lax.rsqrt(var + LNX_EPS) * lnx_g + lnx_b
    bonus = headsum(r * k * r_k) * v
    y_ref[...] = ((yn + bonus) * g).astype(BF16).reshape(nb, tcb, W)


def _rwkv_call(rw, vfirst_src, vec, w2p, a2p, g2p, v2p):
    B, S, _ = rw.shape
    has_vres = vfirst_src is not None
    nb = min(RW_SEQS, B)
    tc = RW_TILE // nb
    const = lambda b, j: (0, 0)
    in_specs = [pl.BlockSpec((nb, tc, RW_PAD), lambda b, j: (b, j, 0))]
    args = [rw]
    if has_vres:
        in_specs.append(pl.BlockSpec((nb, tc, RWKV_WIDTH), lambda b, j: (b, j, RW_V // RWKV_WIDTH)))
        args.append(vfirst_src)
    in_specs += [pl.BlockSpec((8, RWKV_WIDTH), const),
                 pl.BlockSpec((LANES, RWKV_WIDTH), const),
                 pl.BlockSpec((LANES, RWKV_WIDTH), const),
                 pl.BlockSpec((2 * LANES, RWKV_WIDTH), const)]
    args += [vec, w2p, a2p, g2p]
    if has_vres:
        in_specs.append(pl.BlockSpec((2 * LANES, RWKV_WIDTH), const))
        args.append(v2p)
    return pl.pallas_call(
        functools.partial(_rwkv_kernel, has_vres=has_vres),
        grid=(B // nb, S // tc),
        in_specs=in_specs,
        out_specs=pl.BlockSpec((nb, tc, RWKV_WIDTH), lambda b, j: (b, j, 0)),
        out_shape=jax.ShapeDtypeStruct((B, S, RWKV_WIDTH), BF16),
        scratch_shapes=[pltpu.VMEM((nb * (RWKV_WIDTH // QUAD), QUAD, QUAD), F32)],
        compiler_params=pltpu.CompilerParams(
            dimension_semantics=("arbitrary", "arbitrary"), vmem_limit_bytes=VMEM_LIMIT),
    )(*args)


def _part(t, op):
    r, n = t.shape
    return op(t.reshape(r // PART_ROWS, PART_ROWS, n), axis=0)


def _sweep(n, body, init, pair=None):
    if pair is None:
        def pair(j, c):
            return body(2 * j + 1, body(2 * j, c))
    c = lax.fori_loop(0, n // 2, pair, init)
    return lax.cond(n % 2 == 1, lambda c: body(n - 1, c), lambda c: c, c)


def _dsa_kernel(cq_ref, kwq_ref, kw_ref, ckv_ref, wqt_ref, wk_ref, wqit_ref, wvt_ref, y_ref,
                kidx_s, ckvt_s, sc_s, bias_s, att_s, yt_s, *, top_k):
    i = pl.program_id(1)
    nblk = kidx_s.shape[0]
    QB = DSA_Q_TILE
    KB = KEY_BLOCK
    HG = ATT_HEAD_GROUP

    @pl.when(i == 0)
    def _():
        for b in range(nblk):
            kidx_s[b] = kw_ref[0, b * KB:(b + 1) * KB, :].astype(BF16)
            ckvt_s[b, :KV_LORA, :] = ckv_ref[0, b * KB:(b + 1) * KB, :].astype(F32).T.astype(BF16)
            ckvt_s[b, KV_LORA:, :] = jnp.ones((ONES_ROWS, KB), BF16)

    cq_t = cq_ref[0].astype(F32).T.astype(BF16)
    q_t = _dot(wqt_ref[...], cq_t).astype(BF16)
    w_t = kwq_ref[0].T
    qi_all = jnp.concatenate([_dot(wqit_ref[h], cq_t) for h in range(IDX_HEADS)],
                             axis=1).astype(BF16)
    w_all = jnp.concatenate([w_t[IDX_DIM + h:IDX_DIM + h + 1, :] for h in range(IDX_HEADS)],
                            axis=1)
    scale = DSA_HEAD_DIM ** -0.5 * LOG2_E
    dh = DSA_HEAD_DIM
    qa_groups = [
        jnp.concatenate(
            [_dot(wk_ref[g * HG + h], q_t[(g * HG + h) * dh:(g * HG + h + 1) * dh]) * scale
             for h in range(HG)], axis=1).astype(BF16)
        for g in range(DSA_HEADS // HG)]

    nkb = (i * QB + QB + KB - 1) // KB
    q_chunk = (i * QB + lax.broadcasted_iota(jnp.int32, (1, QB), 1)) // CHUNK
    k_row = lax.broadcasted_iota(jnp.int32, (KB, 1), 0)
    k_eff = jnp.minimum(top_k, (q_chunk + 1) * CHUNK).astype(F32)
    inf = jnp.float32(jnp.inf)
    one = jnp.float32(1.0)
    zero = jnp.float32(0.0)

    def fin_min(t):
        return jnp.min(t, axis=0, keepdims=True)

    def fin_max(t):
        return jnp.max(t, axis=0, keepdims=True)

    def fin_sum(t):
        return jnp.sum(t, axis=0, keepdims=True)

    def score_body(kb, carry):
        lo, hi = carry
        lg = jnp.maximum(_dot(kidx_s[kb], qi_all), zero) * w_all
        acc = jnp.zeros((KB, QB), F32)
        for h in range(IDX_HEADS):
            acc = acc + lg[:, h * QB:(h + 1) * QB]
        adm = (kb * KB + k_row) // CHUNK <= q_chunk
        sc_s[kb] = jnp.where(adm, acc, -inf)
        lo = jnp.minimum(lo, _part(jnp.where(adm, acc, inf), jnp.min))
        hi = jnp.maximum(hi, _part(jnp.where(adm, acc, -inf), jnp.max))
        return lo, hi

    lo, hi = _sweep(nkb,score_body,
                           (jnp.full((PART_ROWS, QB), inf, F32), jnp.full((PART_ROWS, QB), -inf, F32)))
    lo = fin_min(lo)
    hi = fin_max(hi)

    def count_ge(t):
        def body(kb, acc):
            return acc + _part(jnp.where(sc_s[kb] >= t, one, zero), jnp.sum)
        return fin_sum(_sweep(nkb,body, jnp.zeros((PART_ROWS, QB), F32)))

    def bisect(_, carry):
        lo, hi = carry
        mid = 0.5 * lo + 0.5 * hi
        ge = count_ge(mid) >= k_eff
        return jnp.where(ge, mid, lo), jnp.where(ge, hi, mid)

    lo, hi = lax.fori_loop(0, N_BISECT, bisect, (lo, hi))

    def min_ge(t):
        def body(kb, acc):
            s = sc_s[kb]
            return jnp.minimum(acc, _part(jnp.where(s >= t, s, inf), jnp.min))
        return fin_min(_sweep(nkb, body, jnp.full((PART_ROWS, QB), inf, F32)))

    def walk_cond(st):
        return st[2] > 0

    def walk_body(st):
        m, _, _ = st

        def body(kb, carry):
            cnt, nxt = carry
            s = sc_s[kb]
            gt = s > m
            return (cnt + _part(jnp.where(gt, one, zero), jnp.sum),
                    jnp.minimum(nxt, _part(jnp.where(gt, s, inf), jnp.min)))

        cnt, nxt = _sweep(nkb,body,
                                 (jnp.zeros((PART_ROWS, QB), F32), jnp.full((PART_ROWS, QB), inf, F32)))
        n_gt = fin_sum(cnt)
        done = n_gt < k_eff
        pending = jnp.max(jnp.where(done, 0, 1))
        return jnp.where(done, m, fin_min(nxt)), n_gt, pending

    thr, n_gt, _ = lax.while_loop(
        walk_cond, walk_body, (min_ge(lo), jnp.zeros((1, QB), F32), jnp.int32(1)))
    need = k_eff - n_gt

    tri = (lax.broadcasted_iota(jnp.int32, (KB, KB), 1)
           <= lax.broadcasted_iota(jnp.int32, (KB, KB), 0)).astype(BF16)

    def mask_body(kb, carry):
        s = sc_s[kb]
        tie = jnp.where(s == thr, one, zero)
        pre = _dot(tri, tie.astype(BF16)) + carry
        sel = (s > thr) | ((s == thr) & (pre <= need))
        bias_s[kb] = jnp.where(sel, zero, jnp.float32(NEG_BIG))
        return carry + fin_sum(_part(tie, jnp.sum))

    _sweep(nkb, mask_body, jnp.zeros((1, QB), F32))

    for g in range(DSA_HEADS // HG):
        qa_all = qa_groups[g]

        def att_body(kb, mx):
            kv = ckv_ref[0, pl.ds(pl.multiple_of(kb * KB, KB), KB), :]
            lg = _dot(kv, qa_all)
            b = bias_s[kb]
            parts = []
            for h in range(HG):
                att = lg[:, h * QB:(h + 1) * QB] + b
                att_s[kb, :, h * QB:(h + 1) * QB] = att
                parts.append(_part(att, jnp.max))
            return jnp.maximum(mx, jnp.concatenate(parts, axis=1))

        mx = fin_max(_sweep(nkb, att_body, jnp.full((PART_ROWS, HG * QB), NEG_BIG, F32)))

        def prob(kb):
            return jnp.exp2(att_s[kb] - mx).astype(BF16)

        def pv_body(kb, o):
            return o + _dot(ckvt_s[kb], prob(kb))

        def pv_pair(j, o):
            return o + _dot(jnp.concatenate([ckvt_s[2 * j], ckvt_s[2 * j + 1]], axis=1),
                            jnp.concatenate([prob(2 * j), prob(2 * j + 1)], axis=0))

        o = _sweep(nkb, pv_body, jnp.zeros((KV_LORA + ONES_ROWS, HG * QB), F32), pv_pair)
        o = (o[:KV_LORA] / o[KV_LORA:KV_LORA + 1]).astype(BF16)
        for h in range(HG):
            hh = g * HG + h
            yt_s[hh * dh:(hh + 1) * dh, :] = _dot(wvt_ref[hh], o[:, h * QB:(h + 1) * QB])

    y_ref[0] = yt_s[...].T.astype(BF16)


def _dsa_call(cq, kw, ckv, wqt, wk, wqit, wvt):
    B, S, _ = cq.shape
    QB = DSA_Q_TILE
    KB = KEY_BLOCK
    nblk = S // KB
    top_k = min(TOPK_MAX, S // 4)
    c2 = lambda b, i: (0, 0)
    c3 = lambda b, i: (0, 0, 0)
    return pl.pallas_call(
        functools.partial(_dsa_kernel, top_k=top_k),
        grid=(B, S // QB),
        in_specs=[
            pl.BlockSpec((1, QB, Q_LORA), lambda b, i: (b, i, 0)),
            pl.BlockSpec((1, QB, LANES), lambda b, i: (b, i, 0)),
            pl.BlockSpec((1, S, LANES), lambda b, i: (b, 0, 0)),
            pl.BlockSpec((1, S, KV_LORA), lambda b, i: (b, 0, 0)),
            pl.BlockSpec((DSA_WIDTH, Q_LORA), c2),
            pl.BlockSpec((DSA_HEADS, KV_LORA, DSA_HEAD_DIM), c3),
            pl.BlockSpec((IDX_HEADS, LANES, Q_LORA), c3),
            pl.BlockSpec((DSA_HEADS, DSA_HEAD_DIM, KV_LORA), c3),
        ],
        out_specs=pl.BlockSpec((1, QB, DSA_WIDTH), lambda b, i: (b, i, 0)),
        out_shape=jax.ShapeDtypeStruct((B, S, DSA_WIDTH), BF16),
        scratch_shapes=[
            pltpu.VMEM((nblk, KB, LANES), BF16),
            pltpu.VMEM((nblk, KV_LORA + ONES_ROWS, KB), BF16),
            pltpu.VMEM((nblk, KB, QB), F32),
            pltpu.VMEM((nblk, KB, QB), F32),
            pltpu.VMEM((nblk, KB, ATT_HEAD_GROUP * QB), F32),
            pltpu.VMEM((DSA_WIDTH, QB), F32),
        ],
        compiler_params=pltpu.CompilerParams(
            dimension_semantics=("arbitrary", "arbitrary"), vmem_limit_bytes=VMEM_LIMIT),
    )(cq, kw, kw, ckv, wqt, wk, wqit, wvt)


def _out_ffn_kernel(x_ref, yd_ref, yr_ref, mod_ref, gpm_ref, gff_ref, gpf_ref,
                    wo_ref, wfc_ref, wdn_ref, o_ref):
    mod = mod_ref[0]
    tm = x_ref.shape[1]
    rb = min(FFN_ROW_BLOCK, tm)
    _, nc, fc, _ = wdn_ref.shape
    for r0 in range(0, tm, rb):
        rs = slice(r0, r0 + rb)
        y = _dot(yd_ref[0, rs, :], wo_ref[0, 0]) + _dot(yr_ref[0, rs, :], wo_ref[0, 1])
        x1 = x_ref[0, rs, :] + mod[2:3] * _rms(y, gpm_ref[...])
        h = (_rms(x1, gff_ref[...]) * (1.0 + mod[4:5]) + mod[3:4]).astype(BF16)
        acc = jnp.zeros_like(x1)
        for c in range(nc):
            gate = _dot(h, wfc_ref[0, :, c * fc:(c + 1) * fc])
            up = _dot(h, wfc_ref[0, :, (nc + c) * fc:(nc + c + 1) * fc])
            acc = acc + _dot((gate * _sigmoid(gate) * up).astype(BF16), wdn_ref[0, c])
        o_ref[0, rs, :] = x1 + mod[5:6] * _rms(acc, gpf_ref[...])


def _out_ffn_call(x, yd, yr, mod, gpm, gff, gpf, wo, wfc, wdn, layer, tm):
    B, S, D = x.shape
    _, nc, fc, _ = wdn.shape
    lay3 = lambda b, j: (layer, 0, 0)
    lay4 = lambda b, j: (layer, 0, 0, 0)
    tok = lambda b, j: (b, j, 0)
    c2 = lambda b, j: (0, 0)
    c3 = lambda b, j: (0, 0, 0)
    return pl.pallas_call(
        _out_ffn_kernel,
        grid=(B, S // tm),
        in_specs=[
            pl.BlockSpec((1, tm, D), tok),
            pl.BlockSpec((1, tm, DSA_WIDTH), tok),
            pl.BlockSpec((1, tm, RWKV_WIDTH), tok),
            pl.BlockSpec((1, 6, D), lambda b, j: (b, 0, 0)),
            pl.BlockSpec((1, D), c2),
            pl.BlockSpec((1, D), c2),
            pl.BlockSpec((1, D), c2),
            pl.BlockSpec((1, 2, DSA_WIDTH, D), lay4, pipeline_mode=pl.Buffered(1)),
            pl.BlockSpec((1, D, 2 * nc * fc), lay3, pipeline_mode=pl.Buffered(1)),
            pl.BlockSpec((1, nc, fc, D), lay4, pipeline_mode=pl.Buffered(1)),
        ],
        out_specs=pl.BlockSpec((1, tm, D), tok),
        out_shape=jax.ShapeDtypeStruct((B, S, D), F32),
        compiler_params=pltpu.CompilerParams(
            dimension_semantics=("arbitrary", "arbitrary"), vmem_limit_bytes=VMEM_LIMIT),
    )(x, yd, yr, mod, gpm, gff, gpf, wo, wfc, wdn)


def _pad_rows(w, rows, at):
    out = jnp.zeros((rows,) + w.shape[1:], w.dtype)
    return out.at[at:at + w.shape[0]].set(w)


def kernel(x, c, ada_w, ada_b, pre_g_mix, post_g_mix, pre_g_ffn, post_g_ffn, w_in, w_in_vres, mu_shift, mu_vres, w_out, q_norm_g, kv_norm_g, w_q_up, w_qi_up, w_k_up, w_v_up, kidx_ln_g, kidx_ln_b, w0, w2, a0, a2, g2, v0, v2, k_k, k_a, r_k, lnx_g, lnx_b, w_fc, w_down):
    B, S, D = x.shape
    L = w_in.shape[0]
    n_dsa = Q_LORA + KV_LORA + IDX_DIM + IDX_HEADS
    tm = min(FFN_TILE, S)
    ff_chunk = FF_CHUNK

    mod_all = _mod_call(c, ada_w, ada_b).reshape(L, B, 6, D)

    row = lambda t: t.reshape(1, -1)
    lane_pad = lambda t: jnp.pad(t, (0, LANES - t.shape[0])).reshape(1, LANES)

    n_rw = w_in.shape[2] - n_dsa
    zpad = lambda n: jnp.zeros((L, D, n), F32)
    mv_w = jnp.concatenate([jnp.zeros((1, D, MV_LORA), F32), w_in_vres], axis=0)
    w_in_all = jnp.concatenate(
        [w_in[:, :, n_dsa:], mv_w, zpad(RW_PAD - n_rw - MV_LORA),
         w_in[:, :, :n_dsa], zpad(P_PAD - COL_KW - IDX_DIM - IDX_HEADS)], axis=2).astype(BF16)
    mv_mu = jnp.concatenate([jnp.zeros((1, MV_LORA), F32), mu_vres], axis=0)
    mu_all = jnp.concatenate(
        [mu_shift, mv_mu, jnp.zeros((L, RW_PAD - n_rw - MV_LORA), F32)], axis=1)
    wo_all = w_out.astype(BF16).reshape(L, 2, DSA_WIDTH, D)
    wfc_all = w_fc.astype(BF16)
    wdn_all = w_down.astype(BF16).reshape(L, D_FF // ff_chunk, ff_chunk, D)

    v_first_src = None
    for l in range(L):
        rw, cq, ckv, kw = _inproj_call(
            x, mod_all[l], row(pre_g_mix[l]), w_in_all, l, mu_all[l:l + 1], row(q_norm_g[l]),
            row(kv_norm_g[l]), lane_pad(kidx_ln_g[l]), lane_pad(kidx_ln_b[l]), min(INPROJ_TILE, S))

        vec = jnp.stack([w0[l], a0[l], v0[l - 1] if l else jnp.zeros_like(w0[l]), k_k[l], k_a[l],
                         r_k[l].reshape(-1), lnx_g[l], lnx_b[l]])
        w2p = _pad_rows(w2[l], LANES, 0)
        a2p = _pad_rows(a2[l], LANES, DECAY_LORA).astype(BF16)
        g2p = _pad_rows(g2[l], 2 * LANES, 0).astype(BF16)
        v2p = _pad_rows(v2[l - 1], 2 * LANES, GATE_LORA).astype(BF16) if l else None
        y_rwkv = _rwkv_call(rw, v_first_src, vec, w2p, a2p, g2p, v2p)
        if l == 0:
            v_first_src = rw

        wqt = w_q_up[l].reshape(Q_LORA, DSA_WIDTH).T.astype(BF16)
        wk = jnp.transpose(w_k_up[l], (1, 0, 2)).astype(BF16)
        wqit = jnp.pad(jnp.transpose(w_qi_up[l], (1, 2, 0)),
                       ((0, 0), (0, LANES - IDX_DIM), (0, 0))).astype(BF16)
        wvt = jnp.transpose(w_v_up[l], (1, 2, 0)).astype(BF16)
        y_dsa = _dsa_call(cq, kw, ckv, wqt, wk, wqit, wvt)

        x = _out_ffn_call(x, y_dsa, y_rwkv, mod_all[l], row(post_g_mix[l]), row(pre_g_ffn[l]),
                          row(post_g_ffn[l]), wo_all, wfc_all, wdn_all, l, tm)
    return x
```

```python
import functools

import jax
import jax.numpy as jnp
from jax import lax
from jax.experimental import pallas as pl
from jax.experimental.pallas import tpu as pltpu

F32 = jnp.float32
BF16 = jnp.bfloat16

D_MODEL = 1024
CHUNK = 64
DSA_HEADS = 8
DSA_HEAD_DIM = 64
DSA_WIDTH = DSA_HEADS * DSA_HEAD_DIM
Q_LORA = 256
KV_LORA = 128
IDX_HEADS = 4
IDX_DIM = 64
TOPK_MAX = 256
RWKV_HEADS = 8
RWKV_HEAD_DIM = 64
RWKV_WIDTH = RWKV_HEADS * RWKV_HEAD_DIM
DECAY_LORA = 64
AAA_LORA = 64
MV_LORA = 32
GATE_LORA = 160
D_FF = ((8 * D_MODEL + 3 * 256 - 1) // (3 * 256)) * 256
NORM_EPS = 1e-6
LNX_EPS = 64e-5

LANES = 128
VMEM_LIMIT = 56 * 1024 * 1024

RW_R, RW_K, RW_V = 0, RWKV_WIDTH, 2 * RWKV_WIDTH
RW_WA = 3 * RWKV_WIDTH
RW_GM = RW_WA + DECAY_LORA + AAA_LORA
RW_PAD = RW_GM + 2 * LANES
COL_CQ = RW_PAD
COL_CKV = COL_CQ + Q_LORA
COL_KW = COL_CKV + KV_LORA
P_PAD = COL_KW + LANES

FFN_ROW_BLOCK = 512
FFN_TILE = 1024
FF_CHUNK = 256
INPROJ_COL_BLOCK = 512
INPROJ_ROW_BLOCK = 512
INPROJ_TILE = 1024
RW_CHUNK = 64
RW_TILE = 256
RW_SEQS = 4
QUAD = 4 * RWKV_HEAD_DIM
N_BISECT = 18
KEY_BLOCK = 256
DSA_Q_TILE = 256
ATT_HEAD_GROUP = 8
PART_ROWS = 32
ONES_ROWS = 16
LOG2_E = 1.4426950408889634
NEG_BIG = -1e30


def _dot(a, b):
    return jnp.dot(a, b, preferred_element_type=F32)


def _dot_nt(a, b):
    return lax.dot_general(a, b, (((1,), (1,)), ((), ())), preferred_element_type=F32)


def _split(a):
    hi = a.astype(BF16)
    lo = (a - hi.astype(F32)).astype(BF16)
    return hi, lo


def _dot3(a, b):
    ah, al = _split(a)
    bh, bl = _split(b)
    return _dot(ah, bh) + _dot(al, bh) + _dot(ah, bl)


def _rms(x, g):
    return x * lax.rsqrt(jnp.mean(x * x, axis=-1, keepdims=True) + NORM_EPS) * g


def _sigmoid(x):
    return 1.0 / (1.0 + jnp.exp(-x))


def _mod_kernel(c_ref, w_ref, b_ref, o_ref):
    c = c_ref[...]
    cond = c * _sigmoid(c)
    o_ref[0] = _dot3(cond, w_ref[0]) + b_ref[0]


def _mod_call(c, ada_w, ada_b):
    L, D, D6 = ada_w.shape
    B = c.shape[0]
    nj = D6 // D
    return pl.pallas_call(
        _mod_kernel,
        grid=(L, nj),
        in_specs=[
            pl.BlockSpec((B, D), lambda l, j: (0, 0)),
            pl.BlockSpec((1, D, D), lambda l, j: (l, 0, j)),
            pl.BlockSpec((1, 1, D), lambda l, j: (l, 0, j)),
        ],
        out_specs=pl.BlockSpec((1, B, D), lambda l, j: (l, 0, j)),
        out_shape=jax.ShapeDtypeStruct((L, B, D6), F32),
        compiler_params=pltpu.CompilerParams(vmem_limit_bytes=VMEM_LIMIT),
    )(c, ada_w, ada_b.reshape(L, 1, D6))


def _inproj_kernel(x_ref, mod_ref, g_ref, w_ref, mu_ref, qg_ref, kvg_ref, lng_ref, lnb_ref,
                   rw_ref, cq_ref, ckv_ref, kw_ref, carry_ref):
    j = pl.program_id(1)
    tm = x_ref.shape[1]
    rb = min(INPROJ_ROW_BLOCK, tm)
    mod = mod_ref[0]

    @pl.when(j == 0)
    def _():
        carry_ref[...] = jnp.zeros_like(carry_ref)

    row = lax.broadcasted_iota(jnp.int32, (rb, 1), 0)
    lane = lax.broadcasted_iota(jnp.int32, (1, LANES), 1)
    is_k = lane < IDX_DIM

    for r0 in range(0, tm, rb):
        rs = slice(r0, r0 + rb)
        h = (_rms(x_ref[0, rs, :], g_ref[...]) * (1.0 + mod[1:2]) + mod[0:1]).astype(BF16)

        def proj(lo_, hi_):
            return _dot(h, w_ref[0, :, lo_:hi_])

        for lo_ in range(0, RW_PAD, INPROJ_COL_BLOCK):
            hi_ = min(lo_ + INPROJ_COL_BLOCK, RW_PAD)
            u = proj(lo_, hi_)
            prev = jnp.where(row == 0, carry_ref[0:1, lo_:hi_], pltpu.roll(u, 1, axis=0))
            carry_ref[0:1, lo_:hi_] = u[rb - 1:rb, :]
            rw_ref[0, rs, lo_:hi_] = u + (prev - u) * mu_ref[:, lo_:hi_]

        cq_ref[0, rs, :] = _rms(proj(COL_CQ, COL_CKV), qg_ref[...]).astype(BF16)
        ckv_ref[0, rs, :] = _rms(proj(COL_CKV, COL_KW), kvg_ref[...]).astype(BF16)

        kw = proj(COL_KW, P_PAD)
        mean = jnp.sum(jnp.where(is_k, kw, 0.0), axis=-1, keepdims=True) * (1.0 / IDX_DIM)
        cen = jnp.where(is_k, kw - mean, 0.0)
        var = jnp.sum(cen * cen, axis=-1, keepdims=True) * (1.0 / IDX_DIM)
        kn = cen * lax.rsqrt(var + NORM_EPS) * lng_ref[...] + lnb_ref[...]
        kw_ref[0, rs, :] = jnp.where(is_k, kn, kw * ((IDX_HEADS ** -0.5) * (IDX_DIM ** -0.5)))


def _inproj_call(x, mod, g, w, layer, mu, qg, kvg, lng, lnb, tm):
    B, S, D = x.shape
    const = lambda b, j: (0, 0)
    tok = lambda b, j: (b, j, 0)
    return pl.pallas_call(
        _inproj_kernel,
        grid=(B, S // tm),
        in_specs=[
            pl.BlockSpec((1, tm, D), tok),
            pl.BlockSpec((1, 6, D), lambda b, j: (b, 0, 0)),
            pl.BlockSpec((1, D), const),
            pl.BlockSpec((1, D, P_PAD), lambda b, j: (layer, 0, 0), pipeline_mode=pl.Buffered(1)),
            pl.BlockSpec((1, RW_PAD), const),
            pl.BlockSpec((1, Q_LORA), const),
            pl.BlockSpec((1, KV_LORA), const),
            pl.BlockSpec((1, LANES), const),
            pl.BlockSpec((1, LANES), const),
        ],
        out_specs=[
            pl.BlockSpec((1, tm, RW_PAD), tok),
            pl.BlockSpec((1, tm, Q_LORA), tok),
            pl.BlockSpec((1, tm, KV_LORA), tok),
            pl.BlockSpec((1, tm, LANES), tok),
        ],
        out_shape=[
            jax.ShapeDtypeStruct((B, S, RW_PAD), F32),
            jax.ShapeDtypeStruct((B, S, Q_LORA), BF16),
            jax.ShapeDtypeStruct((B, S, KV_LORA), BF16),
            jax.ShapeDtypeStruct((B, S, LANES), F32),
        ],
        scratch_shapes=[pltpu.VMEM((8, RW_PAD), F32)],
        compiler_params=pltpu.CompilerParams(
            dimension_semantics=("arbitrary", "arbitrary"), vmem_limit_bytes=VMEM_LIMIT),
    )(x, mod, g, w, mu, qg, kvg, lng, lnb)


def _softplus(z):
    return jnp.maximum(z, 0.0) + jnp.log(1.0 + jnp.exp(-jnp.abs(z)))


def _rwkv_kernel(*refs, has_vres):
    if has_vres:
        rw_ref, vf_ref, vec_ref, w2_ref, a2_ref, g2_ref, v2_ref, y_ref, h_ref = refs
    else:
        rw_ref, vec_ref, w2_ref, a2_ref, g2_ref, y_ref, h_ref = refs
    W = RWKV_WIDTH
    N = RWKV_HEAD_DIM
    C = RW_CHUNK
    nb, tcb = rw_ref.shape[0], rw_ref.shape[1]
    tc = nb * tcb

    @pl.when(pl.program_id(1) == 0)
    def _():
        h_ref[...] = jnp.zeros_like(h_ref)

    vec = vec_ref[...]
    w0, a0, v0, k_k, k_a, r_k, lnx_g, lnx_b = [vec[i:i + 1] for i in range(8)]

    def rows(ref, lo_, hi_):
        return ref[:, :, lo_:hi_].reshape(tc, hi_ - lo_)

    r = rows(rw_ref, RW_R, RW_R + W)
    k = rows(rw_ref, RW_K, RW_K + W)
    v = rows(rw_ref, RW_V, RW_V + W)
    wa = rows(rw_ref, RW_WA, RW_GM)
    gm = rows(rw_ref, RW_GM, RW_PAD)

    bd = (lax.broadcasted_iota(jnp.int32, (QUAD, QUAD), 0) // N
          == lax.broadcasted_iota(jnp.int32, (QUAD, QUAD), 1) // N)
    head_ones = bd.astype(BF16)

    def headsum(t):
        tb = t.astype(BF16)
        return jnp.concatenate(
            [_dot(tb[:, q * QUAD:(q + 1) * QUAD], head_ones) for q in range(W // QUAD)], axis=1)

    w_log = -_softplus(-(w0 + _dot3(jnp.tanh(wa), w2_ref[...]))) - 0.5
    logw = -jnp.exp(w_log)
    a = _sigmoid(a0 + _dot(wa.astype(BF16), a2_ref[...]))
    g = _dot(_sigmoid(gm).astype(BF16), g2_ref[...])
    if has_vres:
        v = v + (rows(vf_ref, 0, W) - v) * _sigmoid(v0 + _dot(gm.astype(BF16), v2_ref[...]))
    kk = k * k_k
    kk = kk * lax.rsqrt(jnp.maximum(headsum(kk * kk), 1e-24))
    k = k * (1.0 + (a - 1.0) * k_a)

    ti = lax.broadcasted_iota(jnp.int32, (tc, tc), 0)
    tj = lax.broadcasted_iota(jnp.int32, (tc, tc), 1)
    tri = ((ti // C == tj // C) & (tj <= ti)).astype(BF16)
    lh = logw.astype(BF16)
    lm = (logw - lh.astype(F32)).astype(BF16)
    cum = _dot(tri, lh) + _dot(tri, lm)

    bt = kk * jnp.exp(cum - logw)
    at = -(a * kk) * jnp.exp(-cum)
    kt = k * jnp.exp(-cum)
    rt = r * jnp.exp(cum)

    lane_q = lax.broadcasted_iota(jnp.int32, (1, QUAD), 1) // N
    hmask = [lane_q == h for h in range(4)]
    gr = lax.broadcasted_iota(jnp.int32, (2 * C, 2 * QUAD), 0)
    gc = lax.broadcasted_iota(jnp.int32, (2 * C, 2 * QUAD), 1) % C
    gmask = ((gr < C) & (gc < gr)) | ((gr >= C) & (gc <= gr - C))
    pr = lax.broadcasted_iota(jnp.int32, (C, LANES), 0)
    pc = lax.broadcasted_iota(jnp.int32, (C, LANES), 1)
    eye_pad = (pr == pc).astype(F32)
    left = pc < C
    zeros_cq = jnp.zeros((C, QUAD), F32)
    n_ch = tc // C
    n_q = W // QUAD

    cells = [(c, q) for c in range(n_ch) for q in range(n_q)]

    cell = {}
    for c, q in cells:
        rs = slice(c * C, (c + 1) * C)
        cs = slice(q * QUAD, (q + 1) * QUAD)
        tot = cum[c * C + C - 1:c * C + C, cs]
        tail = jnp.exp(tot - cum[rs, cs])
        atail = -(a[rs, cs] * kk[rs, cs]) * tail
        ktail = k[rs, cs] * tail
        dp_t = jnp.broadcast_to(jnp.exp(tot), (LANES, QUAD)).T
        vq = v[rs, cs]
        lhs = jnp.concatenate([bt[rs, cs], rt[rs, cs]], axis=0).astype(BF16)
        rhs_t = jnp.concatenate(
            [jnp.where(hmask[h], t, 0.0) for h in range(4) for t in (kt[rs, cs], at[rs, cs])],
            axis=0).astype(BF16)
        gram = jnp.where(gmask, _dot_nt(lhs, rhs_t), 0.0)
        cell[c, q] = dict(
            ak_t=jnp.concatenate([atail, ktail], axis=0).T.astype(BF16),
            dp_col=jnp.concatenate([dp_t, dp_t], axis=1), vq=vq, lhs=lhs, gram=gram,
            gram_b=gram.astype(BF16),
            v_masked=[jnp.where(hmask[h], vq, 0.0) for h in range(4)])

    pairs = [(c, q, j) for c, q in cells for j in range(2)]
    st = {}
    for c, q, j in pairs:
        for h in (2 * j, 2 * j + 1):
            st[c, q, h] = jnp.where(left, eye_pad, cell[c, q]["gram"][0:C, h * LANES:(h + 1) * LANES])
    zeros_pad = jnp.zeros((C, LANES), F32)
    for _ in range(6):
        for c, q, j in pairs:
            s0, s1 = st[c, q, 2 * j], st[c, q, 2 * j + 1]
            lhs2 = jnp.where(left, pltpu.roll(s0, C, axis=1), s1).astype(BF16)
            rhs2 = jnp.concatenate(
                [jnp.concatenate([s0, zeros_pad], axis=1),
                 jnp.concatenate([zeros_pad, s1], axis=1)], axis=0).astype(BF16)
            res = _dot(lhs2, rhs2)
            st[c, q, 2 * j] = jnp.where(left, s0 + res[:, :LANES], res[:, :LANES])
            st[c, q, 2 * j + 1] = jnp.where(left, s1 + res[:, LANES:], res[:, LANES:])

    for c, q in cells:
        o = cell[c, q]
        o["t_cat"] = jnp.concatenate(
            [jnp.where(left, st[c, q, h], 0.0) for h in range(4)], axis=1).astype(BF16)
        vs0 = jnp.concatenate(
            [t for h in range(4) for t in (o["v_masked"][h], zeros_cq)], axis=0).astype(BF16)
        o["akv"] = _dot(o["gram_b"][0:C], vs0)

    def recur(chunks, hq):
        keys = [(b, q) for b in range(nb) for q in range(n_q)]
        z, u, y = {}, {}, {}
        for b, q in keys:
            z[b, q] = _dot(cell[chunks[b], q]["lhs"], hq[b][q].astype(BF16))
        for b, q in keys:
            o = cell[chunks[b], q]
            xs = z[b, q][0:C] + o["akv"]
            xs_st = jnp.concatenate(
                [t for h in range(4) for t in (jnp.where(hmask[h], xs, 0.0), zeros_cq)],
                axis=0).astype(BF16)
            u[b, q] = _dot(o["t_cat"], xs_st)
        for b, q in keys:
            o = cell[chunks[b], q]
            uv = jnp.concatenate([u[b, q], o["vq"]], axis=0).astype(BF16)
            hq[b][q] = o["dp_col"] * hq[b][q] + jnp.where(bd, _dot(o["ak_t"], uv), 0.0)
        for b, q in keys:
            o = cell[chunks[b], q]
            uv_st = jnp.concatenate(
                [t for h in range(4) for t in (o["v_masked"][h], jnp.where(hmask[h], u[b, q], 0.0))],
                axis=0).astype(BF16)
            y[b, q] = z[b, q][C:2 * C] + _dot(o["gram_b"][C:2 * C], uv_st)
        return [jnp.concatenate([y[b, q] for q in range(n_q)], axis=1) for b in range(nb)]

    per_seq = tcb // C
    hq = [[h_ref[b * n_q + q] for q in range(n_q)] for b in range(nb)]
    y_rows = [None] * n_ch
    for s_ in range(per_seq):
        ys = recur([b * per_seq + s_ for b in range(nb)], hq)
        for b in range(nb):
            y_rows[b * per_seq + s_] = ys[b]
    for b in range(nb):
        for q in range(n_q):
            h_ref[b * n_q + q] = hq[b][q]
    y = jnp.concatenate(y_rows, axis=0)

    mu = headsum(y) * (1.0 / N)
    yc = y - mu
    var = headsum(yc * yc) * (1.0 / N)
    yn = yc * lax.rsqrt(var + LNX_EPS) * lnx_g + lnx_b
    bonus = headsum(r * k * r_k) * v
    y_ref[...] = ((yn + bonus) * g).astype(BF16).reshape(nb, tcb, W)


def _rwkv_call(rw, vfirst_src, vec, w2p, a2p, g2p, v2p):
    B, S, _ = rw.shape
    has_vres = vfirst_src is not None
    nb = min(RW_SEQS, B)
    tc = RW_TILE // nb
    const = lambda b, j: (0, 0)
    in_specs = [pl.BlockSpec((nb, tc, RW_PAD), lambda b, j: (b, j, 0))]
    args = [rw]
    if has_vres:
        in_specs.append(pl.BlockSpec((nb, tc, RWKV_WIDTH), lambda b, j: (b, j, RW_V // RWKV_WIDTH)))
        args.append(vfirst_src)
    in_specs += [pl.BlockSpec((8, RWKV_WIDTH), const),
                 pl.BlockSpec((LANES, RWKV_WIDTH), const),
                 pl.BlockSpec((LANES, RWKV_WIDTH), const),
                 pl.BlockSpec((2 * LANES, RWKV_WIDTH), const)]
    args += [vec, w2p, a2p, g2p]
    if has_vres:
        in_specs.append(pl.BlockSpec((2 * LANES, RWKV_WIDTH), const))
        args.append(v2p)
    return pl.pallas_call(
        functools.partial(_rwkv_kernel, has_vres=has_vres),
        grid=(B // nb, S // tc),
        in_specs=in_specs,
        out_specs=pl.BlockSpec((nb, tc, RWKV_WIDTH), lambda b, j: (b, j, 0)),
        out_shape=jax.ShapeDtypeStruct((B, S, RWKV_WIDTH), BF16),
        scratch_shapes=[pltpu.VMEM((nb * (RWKV_WIDTH // QUAD), QUAD, QUAD), F32)],
        compiler_params=pltpu.CompilerParams(
            dimension_semantics=("arbitrary", "arbitrary"), vmem_limit_bytes=VMEM_LIMIT),
    )(*args)


def _part(t, op):
    r, n = t.shape
    return op(t.reshape(r // PART_ROWS, PART_ROWS, n), axis=0)


def _sweep(n, body, init, pair=None):
    if pair is None:
        def pair(j, c):
            return body(2 * j + 1, body(2 * j, c))
    c = lax.fori_loop(0, n // 2, pair, init)
    return lax.cond(n % 2 == 1, lambda c: body(n - 1, c), lambda c: c, c)


def _dsa_kernel(cq_ref, kwq_ref, kw_ref, ckv_ref, wqt_ref, wk_ref, wqit_ref, wvt_ref, y_ref,
                kidx_s, ckvt_s, sc_s, bias_s, att_s, yt_s, *, top_k):
    i = pl.program_id(1)
    nblk = kidx_s.shape[0]
    QB = DSA_Q_TILE
    KB = KEY_BLOCK
    HG = ATT_HEAD_GROUP

    @pl.when(i == 0)
    def _():
        for b in range(nblk):
            kidx_s[b] = kw_ref[0, b * KB:(b + 1) * KB, :].astype(BF16)
            ckvt_s[b, :KV_LORA, :] = ckv_ref[0, b * KB:(b + 1) * KB, :].astype(F32).T.astype(BF16)
            ckvt_s[b, KV_LORA:, :] = jnp.ones((ONES_ROWS, KB), BF16)

    cq_t = cq_ref[0].astype(F32).T.astype(BF16)
    q_t = _dot(wqt_ref[...], cq_t).astype(BF16)
    w_t = kwq_ref[0].T
    qi_all = jnp.concatenate([_dot(wqit_ref[h], cq_t) for h in range(IDX_HEADS)],
                             axis=1).astype(BF16)
    w_all = jnp.concatenate([w_t[IDX_DIM + h:IDX_DIM + h + 1, :] for h in range(IDX_HEADS)],
                            axis=1)
    scale = DSA_HEAD_DIM ** -0.5 * LOG2_E
    dh = DSA_HEAD_DIM
    qa_groups = [
        jnp.concatenate(
            [_dot(wk_ref[g * HG + h], q_t[(g * HG + h) * dh:(g * HG + h + 1) * dh]) * scale
             for h in range(HG)], axis=1).astype(BF16)
        for g in range(DSA_HEADS // HG)]

    nkb = (i * QB + QB + KB - 1) // KB
    q_chunk = (i * QB + lax.broadcasted_iota(jnp.int32, (1, QB), 1)) // CHUNK
    k_row = lax.broadcasted_iota(jnp.int32, (KB, 1), 0)
    k_eff = jnp.minimum(top_k, (q_chunk + 1) * CHUNK).astype(F32)
    inf = jnp.float32(jnp.inf)
    one = jnp.float32(1.0)
    zero = jnp.float32(0.0)

    def fin_min(t):
        return jnp.min(t, axis=0, keepdims=True)

    def fin_max(t):
        return jnp.max(t, axis=0, keepdims=True)

    def fin_sum(t):
        return jnp.sum(t, axis=0, keepdims=True)

    def score_body(kb, carry):
        lo, hi = carry
        lg = jnp.maximum(_dot(kidx_s[kb], qi_all), zero) * w_all
        acc = jnp.zeros((KB, QB), F32)
        for h in range(IDX_HEADS):
            acc = acc + lg[:, h * QB:(h + 1) * QB]
        adm = (kb * KB + k_row) // CHUNK <= q_chunk
        sc_s[kb] = jnp.where(adm, acc, -inf)
        lo = jnp.minimum(lo, _part(jnp.where(adm, acc, inf), jnp.min))
        hi = jnp.maximum(hi, _part(jnp.where(adm, acc, -inf), jnp.max))
        return lo, hi

    lo, hi = _sweep(nkb,score_body,
                           (jnp.full((PART_ROWS, QB), inf, F32), jnp.full((PART_ROWS, QB), -inf, F32)))
    lo = fin_min(lo)
    hi = fin_max(hi)

    def count_ge(t):
        def body(kb, acc):
            return acc + _part(jnp.where(sc_s[kb] >= t, one, zero), jnp.sum)
        return fin_sum(_sweep(nkb,body, jnp.zeros((PART_ROWS, QB), F32)))

    def bisect(_, carry):
        lo, hi = carry
        mid = 0.5 * lo + 0.5 * hi
        ge = count_ge(mid) >= k_eff
        return jnp.where(ge, mid, lo), jnp.where(ge, hi, mid)

    n_bisect = jnp.where((i + 1) * QB <= top_k, 0, N_BISECT)
    lo, hi = lax.fori_loop(0, n_bisect, bisect, (lo, hi))

    def min_ge(t):
        def body(kb, acc):
            s = sc_s[kb]
            return jnp.minimum(acc, _part(jnp.where(s >= t, s, inf), jnp.min))
        return fin_min(_sweep(nkb, body, jnp.full((PART_ROWS, QB), inf, F32)))

    def walk_cond(st):
        return st[2] > 0

    def walk_body(st):
        m, _, _ = st

        def body(kb, carry):
            cnt, nxt = carry
            s = sc_s[kb]
            gt = s > m
            return (cnt + _part(jnp.where(gt, one, zero), jnp.sum),
                    jnp.minimum(nxt, _part(jnp.where(gt, s, inf), jnp.min)))

        cnt, nxt = _sweep(nkb,body,
                                 (jnp.zeros((PART_ROWS, QB), F32), jnp.full((PART_ROWS, QB), inf, F32)))
        n_gt = fin_sum(cnt)
        done = n_gt < k_eff
        pending = jnp.max(jnp.where(done, 0, 1))
        return jnp.where(done, m, fin_min(nxt)), n_gt, pending

    thr, n_gt, _ = lax.while_loop(
        walk_cond, walk_body, (min_ge(lo), jnp.zeros((1, QB), F32), jnp.int32(1)))
    need = k_eff - n_gt

    tri = (lax.broadcasted_iota(jnp.int32, (KB, KB), 1)
           <= lax.broadcasted_iota(jnp.int32, (KB, KB), 0)).astype(BF16)

    def mask_body(kb, carry):
        s = sc_s[kb]
        tie = jnp.where(s == thr, one, zero)
        pre = _dot(tri, tie.astype(BF16)) + carry
        sel = (s > thr) | ((s == thr) & (pre <= need))
        bias_s[kb] = jnp.where(sel, zero, jnp.float32(NEG_BIG))
        return carry + fin_sum(_part(tie, jnp.sum))

    _sweep(nkb, mask_body, jnp.zeros((1, QB), F32))

    for g in range(DSA_HEADS // HG):
        qa_all = qa_groups[g]

        def att_body(kb, mx):
            kv = ckv_ref[0, pl.ds(pl.multiple_of(kb * KB, KB), KB), :]
            lg = _dot(kv, qa_all)
            b = bias_s[kb]
            parts = []
            for h in range(HG):
                att = lg[:, h * QB:(h + 1) * QB] + b
                att_s[kb, :, h * QB:(h + 1) * QB] = att
                parts.append(_part(att, jnp.max))
            return jnp.maximum(mx, jnp.concatenate(parts, axis=1))

        mx = fin_max(_sweep(nkb, att_body, jnp.full((PART_ROWS, HG * QB), NEG_BIG, F32)))

        def prob(kb):
            return jnp.exp2(att_s[kb] - mx).astype(BF16)

        def pv_body(kb, o):
            return o + _dot(ckvt_s[kb], prob(kb))

        def pv_pair(j, o):
            return o + _dot(jnp.concatenate([ckvt_s[2 * j], ckvt_s[2 * j + 1]], axis=1),
                            jnp.concatenate([prob(2 * j), prob(2 * j + 1)], axis=0))

        o = _sweep(nkb, pv_body, jnp.zeros((KV_LORA + ONES_ROWS, HG * QB), F32), pv_pair)
        o = (o[:KV_LORA] / o[KV_LORA:KV_LORA + 1]).astype(BF16)
        for h in range(HG):
            hh = g * HG + h
            yt_s[hh * dh:(hh + 1) * dh, :] = _dot(wvt_ref[hh], o[:, h * QB:(h + 1) * QB])

    y_ref[0] = yt_s[...].T.astype(BF16)


def _dsa_call(cq, kw, ckv, wqt, wk, wqit, wvt):
    B, S, _ = cq.shape
    QB = DSA_Q_TILE
    KB = KEY_BLOCK
    nblk = S // KB
    top_k = min(TOPK_MAX, S // 4)
    c2 = lambda b, i: (0, 0)
    c3 = lambda b, i: (0, 0, 0)
    return pl.pallas_call(
        functools.partial(_dsa_kernel, top_k=top_k),
        grid=(B, S // QB),
        in_specs=[
            pl.BlockSpec((1, QB, Q_LORA), lambda b, i: (b, i, 0)),
            pl.BlockSpec((1, QB, LANES), lambda b, i: (b, i, 0)),
            pl.BlockSpec((1, S, LANES), lambda b, i: (b, 0, 0)),
            pl.BlockSpec((1, S, KV_LORA), lambda b, i: (b, 0, 0)),
            pl.BlockSpec((DSA_WIDTH, Q_LORA), c2),
            pl.BlockSpec((DSA_HEADS, KV_LORA, DSA_HEAD_DIM), c3),
            pl.BlockSpec((IDX_HEADS, LANES, Q_LORA), c3),
            pl.BlockSpec((DSA_HEADS, DSA_HEAD_DIM, KV_LORA), c3),
        ],
        out_specs=pl.BlockSpec((1, QB, DSA_WIDTH), lambda b, i: (b, i, 0)),
        out_shape=jax.ShapeDtypeStruct((B, S, DSA_WIDTH), BF16),
        scratch_shapes=[
            pltpu.VMEM((nblk, KB, LANES), BF16),
            pltpu.VMEM((nblk, KV_LORA + ONES_ROWS, KB), BF16),
            pltpu.VMEM((nblk, KB, QB), F32),
            pltpu.VMEM((nblk, KB, QB), F32),
            pltpu.VMEM((nblk, KB, ATT_HEAD_GROUP * QB), F32),
            pltpu.VMEM((DSA_WIDTH, QB), F32),
        ],
        compiler_params=pltpu.CompilerParams(
            dimension_semantics=("arbitrary", "arbitrary"), vmem_limit_bytes=VMEM_LIMIT),
    )(cq, kw, kw, ckv, wqt, wk, wqit, wvt)


def _out_ffn_kernel(x_ref, yd_ref, yr_ref, mod_ref, gpm_ref, gff_ref, gpf_ref,
                    wo_ref, wfc_ref, wdn_ref, o_ref):
    mod = mod_ref[0]
    tm = x_ref.shape[1]
    rb = min(FFN_ROW_BLOCK, tm)
    _, nc, fc, _ = wdn_ref.shape
    for r0 in range(0, tm, rb):
        rs = slice(r0, r0 + rb)
        y = _dot(yd_ref[0, rs, :], wo_ref[0, 0]) + _dot(yr_ref[0, rs, :], wo_ref[0, 1])
        x1 = x_ref[0, rs, :] + mod[2:3] * _rms(y, gpm_ref[...])
        h = (_rms(x1, gff_ref[...]) * (1.0 + mod[4:5]) + mod[3:4]).astype(BF16)
        acc = jnp.zeros_like(x1)
        for c in range(nc):
            gate = _dot(h, wfc_ref[0, :, c * fc:(c + 1) * fc])
            up = _dot(h, wfc_ref[0, :, (nc + c) * fc:(nc + c + 1) * fc])
            acc = acc + _dot((gate * _sigmoid(gate) * up).astype(BF16), wdn_ref[0, c])
        o_ref[0, rs, :] = x1 + mod[5:6] * _rms(acc, gpf_ref[...])


def _out_ffn_call(x, yd, yr, mod, gpm, gff, gpf, wo, wfc, wdn, layer, tm):
    B, S, D = x.shape
    _, nc, fc, _ = wdn.shape
    lay3 = lambda b, j: (layer, 0, 0)
    lay4 = lambda b, j: (layer, 0, 0, 0)
    tok = lambda b, j: (b, j, 0)
    c2 = lambda b, j: (0, 0)
    c3 = lambda b, j: (0, 0, 0)
    return pl.pallas_call(
        _out_ffn_kernel,
        grid=(B, S // tm),
        in_specs=[
            pl.BlockSpec((1, tm, D), tok),
            pl.BlockSpec((1, tm, DSA_WIDTH), tok),
            pl.BlockSpec((1, tm, RWKV_WIDTH), tok),
            pl.BlockSpec((1, 6, D), lambda b, j: (b, 0, 0)),
            pl.BlockSpec((1, D), c2),
            pl.BlockSpec((1, D), c2),
            pl.BlockSpec((1, D), c2),
            pl.BlockSpec((1, 2, DSA_WIDTH, D), lay4, pipeline_mode=pl.Buffered(1)),
            pl.BlockSpec((1, D, 2 * nc * fc), lay3, pipeline_mode=pl.Buffered(1)),
            pl.BlockSpec((1, nc, fc, D), lay4, pipeline_mode=pl.Buffered(1)),
        ],
        out_specs=pl.BlockSpec((1, tm, D), tok),
        out_shape=jax.ShapeDtypeStruct((B, S, D), F32),
        compiler_params=pltpu.CompilerParams(
            dimension_semantics=("arbitrary", "arbitrary"), vmem_limit_bytes=VMEM_LIMIT),
    )(x, yd, yr, mod, gpm, gff, gpf, wo, wfc, wdn)


def _pad_rows(w, rows, at):
    out = jnp.zeros((rows,) + w.shape[1:], w.dtype)
    return out.at[at:at + w.shape[0]].set(w)


def kernel(x, c, ada_w, ada_b, pre_g_mix, post_g_mix, pre_g_ffn, post_g_ffn, w_in, w_in_vres, mu_shift, mu_vres, w_out, q_norm_g, kv_norm_g, w_q_up, w_qi_up, w_k_up, w_v_up, kidx_ln_g, kidx_ln_b, w0, w2, a0, a2, g2, v0, v2, k_k, k_a, r_k, lnx_g, lnx_b, w_fc, w_down):
    B, S, D = x.shape
    L = w_in.shape[0]
    n_dsa = Q_LORA + KV_LORA + IDX_DIM + IDX_HEADS
    tm = min(FFN_TILE, S)
    ff_chunk = FF_CHUNK

    mod_all = _mod_call(c, ada_w, ada_b).reshape(L, B, 6, D)

    row = lambda t: t.reshape(1, -1)
    lane_pad = lambda t: jnp.pad(t, (0, LANES - t.shape[0])).reshape(1, LANES)

    n_rw = w_in.shape[2] - n_dsa
    zpad = lambda n: jnp.zeros((L, D, n), F32)
    mv_w = jnp.concatenate([jnp.zeros((1, D, MV_LORA), F32), w_in_vres], axis=0)
    w_in_all = jnp.concatenate(
        [w_in[:, :, n_dsa:], mv_w, zpad(RW_PAD - n_rw - MV_LORA),
         w_in[:, :, :n_dsa], zpad(P_PAD - COL_KW - IDX_DIM - IDX_HEADS)], axis=2).astype(BF16)
    mv_mu = jnp.concatenate([jnp.zeros((1, MV_LORA), F32), mu_vres], axis=0)
    mu_all = jnp.concatenate(
        [mu_shift, mv_mu, jnp.zeros((L, RW_PAD - n_rw - MV_LORA), F32)], axis=1)
    wo_all = w_out.astype(BF16).reshape(L, 2, DSA_WIDTH, D)
    wfc_all = w_fc.astype(BF16)
    wdn_all = w_down.astype(BF16).reshape(L, D_FF // ff_chunk, ff_chunk, D)

    v_first_src = None
    for l in range(L):
        rw, cq, ckv, kw = _inproj_call(
            x, mod_all[l], row(pre_g_mix[l]), w_in_all, l, mu_all[l:l + 1], row(q_norm_g[l]),
            row(kv_norm_g[l]), lane_pad(kidx_ln_g[l]), lane_pad(kidx_ln_b[l]), min(INPROJ_TILE, S))

        vec = jnp.stack([w0[l], a0[l], v0[l - 1] if l else jnp.zeros_like(w0[l]), k_k[l], k_a[l],
                         r_k[l].reshape(-1), lnx_g[l], lnx_b[l]])
        w2p = _pad_rows(w2[l], LANES, 0)
        a2p = _pad_rows(a2[l], LANES, DECAY_LORA).astype(BF16)
        g2p = _pad_rows(g2[l], 2 * LANES, 0).astype(BF16)
        v2p = _pad_rows(v2[l - 1], 2 * LANES, GATE_LORA).astype(BF16) if l else None
        y_rwkv = _rwkv_call(rw, v_first_src, vec, w2p, a2p, g2p, v2p)
        if l == 0:
            v_first_src = rw

        wqt = w_q_up[l].reshape(Q_LORA, DSA_WIDTH).T.astype(BF16)
        wk = jnp.transpose(w_k_up[l], (1, 0, 2)).astype(BF16)
        wqit = jnp.pad(jnp.transpose(w_qi_up[l], (1, 2, 0)),
                       ((0, 0), (0, LANES - IDX_DIM), (0, 0))).astype(BF16)
        wvt = jnp.transpose(w_v_up[l], (1, 2, 0)).astype(BF16)
        y_dsa = _dsa_call(cq, kw, ckv, wqt, wk, wqit, wvt)

        x = _out_ffn_call(x, y_dsa, y_rwkv, mod_all[l], row(post_g_mix[l]), row(pre_g_ffn[l]),
                          row(post_g_ffn[l]), wo_all, wfc_all, wdn_all, l, tm)
    return x
```

```python
import functools

import jax
import jax.numpy as jnp
from jax import lax
from jax.experimental import pallas as pl
from jax.experimental.pallas import tpu as pltpu

F32 = jnp.float32
BF16 = jnp.bfloat16

D_MODEL = 1024
CHUNK = 64
DSA_HEADS = 8
DSA_HEAD_DIM = 64
DSA_WIDTH = DSA_HEADS * DSA_HEAD_DIM
Q_LORA = 256
KV_LORA = 128
IDX_HEADS = 4
IDX_DIM = 64
TOPK_MAX = 256
RWKV_HEADS = 8
RWKV_HEAD_DIM = 64
RWKV_WIDTH = RWKV_HEADS * RWKV_HEAD_DIM
DECAY_LORA = 64
AAA_LORA = 64
MV_LORA = 32
GATE_LORA = 160
D_FF = ((8 * D_MODEL + 3 * 256 - 1) // (3 * 256)) * 256
NORM_EPS = 1e-6
LNX_EPS = 64e-5

LANES = 128
VMEM_LIMIT = 56 * 1024 * 1024

RW_R, RW_K, RW_V = 0, RWKV_WIDTH, 2 * RWKV_WIDTH
RW_WA = 3 * RWKV_WIDTH
RW_GM = RW_WA + DECAY_LORA + AAA_LORA
RW_PAD = RW_GM + 2 * LANES
COL_CQ = RW_PAD
COL_CKV = COL_CQ + Q_LORA
COL_KW = COL_CKV + KV_LORA
P_PAD = COL_KW + LANES

FFN_ROW_BLOCK = 512
FFN_TILE = 1024
FF_CHUNK = 256
INPROJ_COL_BLOCK = 512
INPROJ_ROW_BLOCK = 512
INPROJ_TILE = 1024
RW_CHUNK = 64
RW_TILE = 256
RW_SEQS = 4
QUAD = 4 * RWKV_HEAD_DIM
N_BISECT = 18
KEY_BLOCK = 256
DSA_Q_TILE = 256
ATT_HEAD_GROUP = 8
PART_ROWS = 32
ONES_ROWS = 16
LOG2_E = 1.4426950408889634
NEG_BIG = -1e30


def _dot(a, b):
    return jnp.dot(a, b, preferred_element_type=F32)


def _dot_nt(a, b):
    return lax.dot_general(a, b, (((1,), (1,)), ((), ())), preferred_element_type=F32)


def _split(a):
    hi = a.astype(BF16)
    lo = (a - hi.astype(F32)).astype(BF16)
    return hi, lo


def _dot3(a, b):
    ah, al = _split(a)
    bh, bl = _split(b)
    return _dot(ah, bh) + _dot(al, bh) + _dot(ah, bl)


def _rms(x, g):
    return x * lax.rsqrt(jnp.mean(x * x, axis=-1, keepdims=True) + NORM_EPS) * g


def _sigmoid(x):
    return 1.0 / (1.0 + jnp.exp(-x))


def _mod_kernel(c_ref, w_ref, b_ref, o_ref):
    c = c_ref[...]
    cond = c * _sigmoid(c)
    o_ref[0] = _dot3(cond, w_ref[0]) + b_ref[0]


def _mod_call(c, ada_w, ada_b):
    L, D, D6 = ada_w.shape
    B = c.shape[0]
    nj = D6 // D
    return pl.pallas_call(
        _mod_kernel,
        grid=(L, nj),
        in_specs=[
            pl.BlockSpec((B, D), lambda l, j: (0, 0)),
            pl.BlockSpec((1, D, D), lambda l, j: (l, 0, j)),
            pl.BlockSpec((1, 1, D), lambda l, j: (l, 0, j)),
        ],
        out_specs=pl.BlockSpec((1, B, D), lambda l, j: (l, 0, j)),
        out_shape=jax.ShapeDtypeStruct((L, B, D6), F32),
        compiler_params=pltpu.CompilerParams(vmem_limit_bytes=VMEM_LIMIT),
    )(c, ada_w, ada_b.reshape(L, 1, D6))


def _inproj_kernel(x_ref, mod_ref, g_ref, w_ref, mu_ref, qg_ref, kvg_ref, lng_ref, lnb_ref,
                   rw_ref, cq_ref, ckv_ref, kw_ref, carry_ref):
    j = pl.program_id(1)
    tm = x_ref.shape[1]
    rb = min(INPROJ_ROW_BLOCK, tm)
    mod = mod_ref[0]

    @pl.when(j == 0)
    def _():
        carry_ref[...] = jnp.zeros_like(carry_ref)

    row = lax.broadcasted_iota(jnp.int32, (rb, 1), 0)
    lane = lax.broadcasted_iota(jnp.int32, (1, LANES), 1)
    is_k = lane < IDX_DIM

    for r0 in range(0, tm, rb):
        rs = slice(r0, r0 + rb)
        h = (_rms(x_ref[0, rs, :], g_ref[...]) * (1.0 + mod[1:2]) + mod[0:1]).astype(BF16)

        def proj(lo_, hi_):
            return _dot(h, w_ref[0, :, lo_:hi_])

        for lo_ in range(0, RW_PAD, INPROJ_COL_BLOCK):
            hi_ = min(lo_ + INPROJ_COL_BLOCK, RW_PAD)
            u = proj(lo_, hi_)
            prev = jnp.where(row == 0, carry_ref[0:1, lo_:hi_], pltpu.roll(u, 1, axis=0))
            carry_ref[0:1, lo_:hi_] = u[rb - 1:rb, :]
            rw_ref[0, rs, lo_:hi_] = u + (prev - u) * mu_ref[:, lo_:hi_]

        cq_ref[0, rs, :] = _rms(proj(COL_CQ, COL_CKV), qg_ref[...]).astype(BF16)
        ckv_ref[0, rs, :] = _rms(proj(COL_CKV, COL_KW), kvg_ref[...]).astype(BF16)

        kw = proj(COL_KW, P_PAD)
        mean = jnp.sum(jnp.where(is_k, kw, 0.0), axis=-1, keepdims=True) * (1.0 / IDX_DIM)
        cen = jnp.where(is_k, kw - mean, 0.0)
        var = jnp.sum(cen * cen, axis=-1, keepdims=True) * (1.0 / IDX_DIM)
        kn = cen * lax.rsqrt(var + NORM_EPS) * lng_ref[...] + lnb_ref[...]
        kw_ref[0, rs, :] = jnp.where(is_k, kn, kw * ((IDX_HEADS ** -0.5) * (IDX_DIM ** -0.5)))


def _inproj_call(x, mod, g, w, layer, mu, qg, kvg, lng, lnb, tm):
    B, S, D = x.shape
    const = lambda b, j: (0, 0)
    tok = lambda b, j: (b, j, 0)
    return pl.pallas_call(
        _inproj_kernel,
        grid=(B, S // tm),
        in_specs=[
            pl.BlockSpec((1, tm, D), tok),
            pl.BlockSpec((1, 6, D), lambda b, j: (b, 0, 0)),
            pl.BlockSpec((1, D), const),
            pl.BlockSpec((1, D, P_PAD), lambda b, j: (layer, 0, 0), pipeline_mode=pl.Buffered(1)),
            pl.BlockSpec((1, RW_PAD), const),
            pl.BlockSpec((1, Q_LORA), const),
            pl.BlockSpec((1, KV_LORA), const),
            pl.BlockSpec((1, LANES), const),
            pl.BlockSpec((1, LANES), const),
        ],
        out_specs=[
            pl.BlockSpec((1, tm, RW_PAD), tok),
            pl.BlockSpec((1, tm, Q_LORA), tok),
            pl.BlockSpec((1, tm, KV_LORA), tok),
            pl.BlockSpec((1, tm, LANES), tok),
        ],
        out_shape=[
            jax.ShapeDtypeStruct((B, S, RW_PAD), F32),
            jax.ShapeDtypeStruct((B, S, Q_LORA), BF16),
            jax.ShapeDtypeStruct((B, S, KV_LORA), BF16),
            jax.ShapeDtypeStruct((B, S, LANES), F32),
        ],
        scratch_shapes=[pltpu.VMEM((8, RW_PAD), F32)],
        compiler_params=pltpu.CompilerParams(
            dimension_semantics=("arbitrary", "arbitrary"), vmem_limit_bytes=VMEM_LIMIT),
    )(x, mod, g, w, mu, qg, kvg, lng, lnb)


def _softplus(z):
    return jnp.maximum(z, 0.0) + jnp.log(1.0 + jnp.exp(-jnp.abs(z)))


def _rwkv_kernel(*refs, has_vres):
    if has_vres:
        rw_ref, vf_ref, vec_ref, w2_ref, a2_ref, g2_ref, v2_ref, y_ref, h_ref = refs
    else:
        rw_ref, vec_ref, w2_ref, a2_ref, g2_ref, y_ref, h_ref = refs
    W = RWKV_WIDTH
    N = RWKV_HEAD_DIM
    C = RW_CHUNK
    nb, tcb = rw_ref.shape[0], rw_ref.shape[1]
    tc = nb * tcb

    @pl.when(pl.program_id(1) == 0)
    def _():
        h_ref[...] = jnp.zeros_like(h_ref)

    vec = vec_ref[...]
    w0, a0, v0, k_k, k_a, r_k, lnx_g, lnx_b = [vec[i:i + 1] for i in range(8)]

    def rows(ref, lo_, hi_):
        return ref[:, :, lo_:hi_].reshape(tc, hi_ - lo_)

    r = rows(rw_ref, RW_R, RW_R + W)
    k = rows(rw_ref, RW_K, RW_K + W)
    v = rows(rw_ref, RW_V, RW_V + W)
    wa = rows(rw_ref, RW_WA, RW_GM)
    gm = rows(rw_ref, RW_GM, RW_PAD)

    bd = (lax.broadcasted_iota(jnp.int32, (QUAD, QUAD), 0) // N
          == lax.broadcasted_iota(jnp.int32, (QUAD, QUAD), 1) // N)
    head_ones = bd.astype(BF16)

    def headsum(t):
        tb = t.astype(BF16)
        return jnp.concatenate(
            [_dot(tb[:, q * QUAD:(q + 1) * QUAD], head_ones) for q in range(W // QUAD)], axis=1)

    w_log = -_softplus(-(w0 + _dot3(jnp.tanh(wa), w2_ref[...]))) - 0.5
    logw = -jnp.exp(w_log)
    a = _sigmoid(a0 + _dot(wa.astype(BF16), a2_ref[...]))
    g = _dot(_sigmoid(gm).astype(BF16), g2_ref[...])
    if has_vres:
        v = v + (rows(vf_ref, 0, W) - v) * _sigmoid(v0 + _dot(gm.astype(BF16), v2_ref[...]))
    kk = k * k_k
    kk = kk * lax.rsqrt(jnp.maximum(headsum(kk * kk), 1e-24))
    k = k * (1.0 + (a - 1.0) * k_a)

    ti = lax.broadcasted_iota(jnp.int32, (tc, tc), 0)
    tj = lax.broadcasted_iota(jnp.int32, (tc, tc), 1)
    tri = ((ti // C == tj // C) & (tj <= ti)).astype(BF16)
    lh = logw.astype(BF16)
    lm = (logw - lh.astype(F32)).astype(BF16)
    cum = _dot(tri, lh) + _dot(tri, lm)

    bt = kk * jnp.exp(cum - logw)
    at = -(a * kk) * jnp.exp(-cum)
    kt = k * jnp.exp(-cum)
    rt = r * jnp.exp(cum)

    lane_q = lax.broadcasted_iota(jnp.int32, (1, QUAD), 1) // N
    hmask = [lane_q == h for h in range(4)]
    gr = lax.broadcasted_iota(jnp.int32, (2 * C, 2 * QUAD), 0)
    gc = lax.broadcasted_iota(jnp.int32, (2 * C, 2 * QUAD), 1) % C
    gmask = ((gr < C) & (gc < gr)) | ((gr >= C) & (gc <= gr - C))
    pr = lax.broadcasted_iota(jnp.int32, (C, LANES), 0)
    pc = lax.broadcasted_iota(jnp.int32, (C, LANES), 1)
    eye_pad = (pr == pc).astype(F32)
    left = pc < C
    zeros_cq = jnp.zeros((C, QUAD), F32)
    n_ch = tc // C
    n_q = W // QUAD

    cells = [(c, q) for c in range(n_ch) for q in range(n_q)]

    cell = {}
    for c, q in cells:
        rs = slice(c * C, (c + 1) * C)
        cs = slice(q * QUAD, (q + 1) * QUAD)
        tot = cum[c * C + C - 1:c * C + C, cs]
        tail = jnp.exp(tot - cum[rs, cs])
        atail = -(a[rs, cs] * kk[rs, cs]) * tail
        ktail = k[rs, cs] * tail
        dp_t = jnp.broadcast_to(jnp.exp(tot), (LANES, QUAD)).T
        vq = v[rs, cs]
        lhs = jnp.concatenate([bt[rs, cs], rt[rs, cs]], axis=0).astype(BF16)
        rhs_t = jnp.concatenate(
            [jnp.where(hmask[h], t, 0.0) for h in range(4) for t in (kt[rs, cs], at[rs, cs])],
            axis=0).astype(BF16)
        gram = jnp.where(gmask, _dot_nt(lhs, rhs_t), 0.0)
        cell[c, q] = dict(
            ak_t=jnp.concatenate([atail, ktail], axis=0).T.astype(BF16),
            dp_col=jnp.concatenate([dp_t, dp_t], axis=1), vq=vq, lhs=lhs, gram=gram,
            gram_b=gram.astype(BF16),
            v_masked=[jnp.where(hmask[h], vq, 0.0) for h in range(4)])

    pairs = [(c, q, j) for c, q in cells for j in range(2)]
    st = {}
    for c, q, j in pairs:
        for h in (2 * j, 2 * j + 1):
            st[c, q, h] = jnp.where(left, eye_pad, cell[c, q]["gram"][0:C, h * LANES:(h + 1) * LANES])
    zeros_pad = jnp.zeros((C, LANES), F32)
    for _ in range(6):
        for c, q, j in pairs:
            s0, s1 = st[c, q, 2 * j], st[c, q, 2 * j + 1]
            lhs2 = jnp.where(left, pltpu.roll(s0, C, axis=1), s1).astype(BF16)
            rhs2 = jnp.concatenate(
                [jnp.concatenate([s0, zeros_pad], axis=1),
                 jnp.concatenate([zeros_pad, s1], axis=1)], axis=0).astype(BF16)
            res = _dot(lhs2, rhs2)
            st[c, q, 2 * j] = jnp.where(left, s0 + res[:, :LANES], res[:, :LANES])
            st[c, q, 2 * j + 1] = jnp.where(left, s1 + res[:, LANES:], res[:, LANES:])

    for c, q in cells:
        o = cell[c, q]
        o["t_cat"] = jnp.concatenate(
            [jnp.where(left, st[c, q, h], 0.0) for h in range(4)], axis=1).astype(BF16)
        vs0 = jnp.concatenate(
            [t for h in range(4) for t in (o["v_masked"][h], zeros_cq)], axis=0).astype(BF16)
        o["akv"] = _dot(o["gram_b"][0:C], vs0)

    def recur(chunks, hq):
        keys = [(b, q) for b in range(nb) for q in range(n_q)]
        z, u, y = {}, {}, {}
        for b, q in keys:
            z[b, q] = _dot(cell[chunks[b], q]["lhs"], hq[b][q].astype(BF16))
        for b, q in keys:
            o = cell[chunks[b], q]
            xs = z[b, q][0:C] + o["akv"]
            xs_st = jnp.concatenate(
                [t for h in range(4) for t in (jnp.where(hmask[h], xs, 0.0), zeros_cq)],
                axis=0).astype(BF16)
            u[b, q] = _dot(o["t_cat"], xs_st)
        for b, q in keys:
            o = cell[chunks[b], q]
            uv = jnp.concatenate([u[b, q], o["vq"]], axis=0).astype(BF16)
            hq[b][q] = o["dp_col"] * hq[b][q] + jnp.where(bd, _dot(o["ak_t"], uv), 0.0)
        for b, q in keys:
            o = cell[chunks[b], q]
            uv_st = jnp.concatenate(
                [t for h in range(4) for t in (o["v_masked"][h], jnp.where(hmask[h], u[b, q], 0.0))],
                axis=0).astype(BF16)
            y[b, q] = z[b, q][C:2 * C] + _dot(o["gram_b"][C:2 * C], uv_st)
        return [jnp.concatenate([y[b, q] for q in range(n_q)], axis=1) for b in range(nb)]

    per_seq = tcb // C
    hq = [[h_ref[b * n_q + q] for q in range(n_q)] for b in range(nb)]
    y_rows = [None] * n_ch
    for s_ in range(per_seq):
        ys = recur([b * per_seq + s_ for b in range(nb)], hq)
        for b in range(nb):
            y_rows[b * per_seq + s_] = ys[b]
    for b in range(nb):
        for q in range(n_q):
            h_ref[b * n_q + q] = hq[b][q]
    y = jnp.concatenate(y_rows, axis=0)

    mu = headsum(y) * (1.0 / N)
    yc = y - mu
    var = headsum(yc * yc) * (1.0 / N)
    yn = yc * lax.rsqrt(var + LNX_EPS) * lnx_g + lnx_b
    bonus = headsum(r * k * r_k) * v
    y_ref[...] = ((yn + bonus) * g).astype(BF16).reshape(nb, tcb, W)


def _rwkv_call(rw, vfirst_src, vec, w2p, a2p, g2p, v2p):
    B, S, _ = rw.shape
    has_vres = vfirst_src is not None
    nb = min(RW_SEQS, B)
    tc = RW_TILE // nb
    const = lambda b, j: (0, 0)
    in_specs = [pl.BlockSpec((nb, tc, RW_PAD), lambda b, j: (b, j, 0))]
    args = [rw]
    if has_vres:
        in_specs.append(pl.BlockSpec((nb, tc, RWKV_WIDTH), lambda b, j: (b, j, RW_V // RWKV_WIDTH)))
        args.append(vfirst_src)
    in_specs += [pl.BlockSpec((8, RWKV_WIDTH), const),
                 pl.BlockSpec((LANES, RWKV_WIDTH), const),
                 pl.BlockSpec((LANES, RWKV_WIDTH), const),
                 pl.BlockSpec((2 * LANES, RWKV_WIDTH), const)]
    args += [vec, w2p, a2p, g2p]
    if has_vres:
        in_specs.append(pl.BlockSpec((2 * LANES, RWKV_WIDTH), const))
        args.append(v2p)
    return pl.pallas_call(
        functools.partial(_rwkv_kernel, has_vres=has_vres),
        grid=(B // nb, S // tc),
        in_specs=in_specs,
        out_specs=pl.BlockSpec((nb, tc, RWKV_WIDTH), lambda b, j: (b, j, 0)),
        out_shape=jax.ShapeDtypeStruct((B, S, RWKV_WIDTH), BF16),
        scratch_shapes=[pltpu.VMEM((nb * (RWKV_WIDTH // QUAD), QUAD, QUAD), F32)],
        compiler_params=pltpu.CompilerParams(
            dimension_semantics=("arbitrary", "arbitrary"), vmem_limit_bytes=VMEM_LIMIT),
    )(*args)


def _part(t, op):
    r, n = t.shape
    return op(t.reshape(r // PART_ROWS, PART_ROWS, n), axis=0)


def _sweep(n, body, init, pair=None):
    if pair is None:
        def pair(j, c):
            return body(2 * j + 1, body(2 * j, c))
    c = lax.fori_loop(0, n // 2, pair, init)
    return lax.cond(n % 2 == 1, lambda c: body(n - 1, c), lambda c: c, c)


def _dsa_kernel(cq_ref, kwq_ref, kw_ref, ckv_ref, wqt_ref, wk_ref, wqit_ref, wvt_ref, y_ref,
                kidx_s, ckvt_s, sc_s, bias_s, att_s, yt_s, *, top_k):
    i = pl.program_id(1)
    nblk = kidx_s.shape[0]
    QB = DSA_Q_TILE
    KB = KEY_BLOCK
    HG = ATT_HEAD_GROUP

    @pl.when(i == 0)
    def _():
        for b in range(nblk):
            kidx_s[b] = kw_ref[0, b * KB:(b + 1) * KB, :].astype(BF16)
            ckvt_s[b, :KV_LORA, :] = ckv_ref[0, b * KB:(b + 1) * KB, :].astype(F32).T.astype(BF16)
            ckvt_s[b, KV_LORA:, :] = jnp.ones((ONES_ROWS, KB), BF16)

    cq_t = cq_ref[0].astype(F32).T.astype(BF16)
    q_t = _dot(wqt_ref[...], cq_t).astype(BF16)
    w_t = kwq_ref[0].T
    qi_all = jnp.concatenate([_dot(wqit_ref[h], cq_t) for h in range(IDX_HEADS)],
                             axis=1).astype(BF16)
    w_all = jnp.concatenate([w_t[IDX_DIM + h:IDX_DIM + h + 1, :] for h in range(IDX_HEADS)],
                            axis=1)
    scale = DSA_HEAD_DIM ** -0.5 * LOG2_E
    dh = DSA_HEAD_DIM
    qa_groups = [
        jnp.concatenate(
            [_dot(wk_ref[g * HG + h], q_t[(g * HG + h) * dh:(g * HG + h + 1) * dh]) * scale
             for h in range(HG)], axis=1).astype(BF16)
        for g in range(DSA_HEADS // HG)]

    nkb = (i * QB + QB + KB - 1) // KB
    q_chunk = (i * QB + lax.broadcasted_iota(jnp.int32, (1, QB), 1)) // CHUNK
    k_row = lax.broadcasted_iota(jnp.int32, (KB, 1), 0)
    k_eff = jnp.minimum(top_k, (q_chunk + 1) * CHUNK).astype(F32)
    inf = jnp.float32(jnp.inf)
    one = jnp.float32(1.0)
    zero = jnp.float32(0.0)

    def fin_min(t):
        return jnp.min(t, axis=0, keepdims=True)

    def fin_max(t):
        return jnp.max(t, axis=0, keepdims=True)

    def fin_sum(t):
        return jnp.sum(t, axis=0, keepdims=True)

    def score_body(kb, carry, masked):
        lo, hi = carry
        lg = jnp.maximum(_dot(kidx_s[kb], qi_all), zero) * w_all
        acc = jnp.zeros((KB, QB), F32)
        for h in range(IDX_HEADS):
            acc = acc + lg[:, h * QB:(h + 1) * QB]
        if masked:
            adm = (kb * KB + k_row) // CHUNK <= q_chunk
            sc_s[kb] = jnp.where(adm, acc, -inf)
            lo = jnp.minimum(lo, _part(jnp.where(adm, acc, inf), jnp.min))
            hi = jnp.maximum(hi, _part(jnp.where(adm, acc, -inf), jnp.max))
        else:
            sc_s[kb] = acc
            lo = jnp.minimum(lo, _part(acc, jnp.min))
            hi = jnp.maximum(hi, _part(acc, jnp.max))
        return lo, hi

    n_open = (i * QB + CHUNK) // KB
    carry = _sweep(n_open, functools.partial(score_body, masked=False),
                   (jnp.full((PART_ROWS, QB), inf, F32), jnp.full((PART_ROWS, QB), -inf, F32)))
    lo, hi = lax.fori_loop(n_open, nkb, functools.partial(score_body, masked=True), carry)
    lo = fin_min(lo)
    hi = fin_max(hi)

    def count_ge(t):
        def body(kb, acc):
            return acc + _part(jnp.where(sc_s[kb] >= t, one, zero), jnp.sum)
        return fin_sum(_sweep(nkb,body, jnp.zeros((PART_ROWS, QB), F32)))

    def bisect(_, carry):
        lo, hi = carry
        mid = 0.5 * lo + 0.5 * hi
        ge = count_ge(mid) >= k_eff
        return jnp.where(ge, mid, lo), jnp.where(ge, hi, mid)

    n_bisect = jnp.where((i + 1) * QB <= top_k, 0, N_BISECT)
    lo, hi = lax.fori_loop(0, n_bisect, bisect, (lo, hi))

    def min_ge(t):
        def body(kb, acc):
            s = sc_s[kb]
            return jnp.minimum(acc, _part(jnp.where(s >= t, s, inf), jnp.min))
        return fin_min(_sweep(nkb, body, jnp.full((PART_ROWS, QB), inf, F32)))

    def walk_cond(st):
        return st[2] > 0

    def walk_body(st):
        m, _, _ = st

        def body(kb, carry):
            cnt, nxt = carry
            s = sc_s[kb]
            gt = s > m
            return (cnt + _part(jnp.where(gt, one, zero), jnp.sum),
                    jnp.minimum(nxt, _part(jnp.where(gt, s, inf), jnp.min)))

        cnt, nxt = _sweep(nkb,body,
                                 (jnp.zeros((PART_ROWS, QB), F32), jnp.full((PART_ROWS, QB), inf, F32)))
        n_gt = fin_sum(cnt)
        done = n_gt < k_eff
        pending = jnp.max(jnp.where(done, 0, 1))
        return jnp.where(done, m, fin_min(nxt)), n_gt, pending

    thr, n_gt, _ = lax.while_loop(
        walk_cond, walk_body, (min_ge(lo), jnp.zeros((1, QB), F32), jnp.int32(1)))
    need = k_eff - n_gt

    tri = (lax.broadcasted_iota(jnp.int32, (KB, KB), 1)
           <= lax.broadcasted_iota(jnp.int32, (KB, KB), 0)).astype(BF16)

    def mask_body(kb, carry):
        s = sc_s[kb]
        tie = jnp.where(s == thr, one, zero)
        pre = _dot(tri, tie.astype(BF16)) + carry
        sel = (s > thr) | ((s == thr) & (pre <= need))
        bias_s[kb] = jnp.where(sel, zero, jnp.float32(NEG_BIG))
        return carry + fin_sum(_part(tie, jnp.sum))

    _sweep(nkb, mask_body, jnp.zeros((1, QB), F32))

    for g in range(DSA_HEADS // HG):
        qa_all = qa_groups[g]

        def att_body(kb, mx):
            kv = ckv_ref[0, pl.ds(pl.multiple_of(kb * KB, KB), KB), :]
            lg = _dot(kv, qa_all)
            b = bias_s[kb]
            parts = []
            for h in range(HG):
                att = lg[:, h * QB:(h + 1) * QB] + b
                att_s[kb, :, h * QB:(h + 1) * QB] = att
                parts.append(_part(att, jnp.max))
            return jnp.maximum(mx, jnp.concatenate(parts, axis=1))

        mx = fin_max(_sweep(nkb, att_body, jnp.full((PART_ROWS, HG * QB), NEG_BIG, F32)))

        def prob(kb):
            return jnp.exp2(att_s[kb] - mx).astype(BF16)

        def pv_body(kb, o):
            return o + _dot(ckvt_s[kb], prob(kb))

        def pv_pair(j, o):
            return o + _dot(jnp.concatenate([ckvt_s[2 * j], ckvt_s[2 * j + 1]], axis=1),
                            jnp.concatenate([prob(2 * j), prob(2 * j + 1)], axis=0))

        o = _sweep(nkb, pv_body, jnp.zeros((KV_LORA + ONES_ROWS, HG * QB), F32), pv_pair)
        o = (o[:KV_LORA] / o[KV_LORA:KV_LORA + 1]).astype(BF16)
        for h in range(HG):
            hh = g * HG + h
            yt_s[hh * dh:(hh + 1) * dh, :] = _dot(wvt_ref[hh], o[:, h * QB:(h + 1) * QB])

    y_ref[0] = yt_s[...].T.astype(BF16)


def _dsa_call(cq, kw, ckv, wqt, wk, wqit, wvt):
    B, S, _ = cq.shape
    QB = DSA_Q_TILE
    KB = KEY_BLOCK
    nblk = S // KB
    top_k = min(TOPK_MAX, S // 4)
    c2 = lambda b, i: (0, 0)
    c3 = lambda b, i: (0, 0, 0)
    return pl.pallas_call(
        functools.partial(_dsa_kernel, top_k=top_k),
        grid=(B, S // QB),
        in_specs=[
            pl.BlockSpec((1, QB, Q_LORA), lambda b, i: (b, i, 0)),
            pl.BlockSpec((1, QB, LANES), lambda b, i: (b, i, 0)),
            pl.BlockSpec((1, S, LANES), lambda b, i: (b, 0, 0)),
            pl.BlockSpec((1, S, KV_LORA), lambda b, i: (b, 0, 0)),
            pl.BlockSpec((DSA_WIDTH, Q_LORA), c2),
            pl.BlockSpec((DSA_HEADS, KV_LORA, DSA_HEAD_DIM), c3),
            pl.BlockSpec((IDX_HEADS, LANES, Q_LORA), c3),
            pl.BlockSpec((DSA_HEADS, DSA_HEAD_DIM, KV_LORA), c3),
        ],
        out_specs=pl.BlockSpec((1, QB, DSA_WIDTH), lambda b, i: (b, i, 0)),
        out_shape=jax.ShapeDtypeStruct((B, S, DSA_WIDTH), BF16),
        scratch_shapes=[
            pltpu.VMEM((nblk, KB, LANES), BF16),
            pltpu.VMEM((nblk, KV_LORA + ONES_ROWS, KB), BF16),
            pltpu.VMEM((nblk, KB, QB), F32),
            pltpu.VMEM((nblk, KB, QB), F32),
            pltpu.VMEM((nblk, KB, ATT_HEAD_GROUP * QB), F32),
            pltpu.VMEM((DSA_WIDTH, QB), F32),
        ],
        compiler_params=pltpu.CompilerParams(
            dimension_semantics=("arbitrary", "arbitrary"), vmem_limit_bytes=VMEM_LIMIT),
    )(cq, kw, kw, ckv, wqt, wk, wqit, wvt)


def _out_ffn_kernel(x_ref, yd_ref, yr_ref, mod_ref, gpm_ref, gff_ref, gpf_ref,
                    wo_ref, wfc_ref, wdn_ref, o_ref):
    mod = mod_ref[0]
    tm = x_ref.shape[1]
    rb = min(FFN_ROW_BLOCK, tm)
    _, nc, fc, _ = wdn_ref.shape
    for r0 in range(0, tm, rb):
        rs = slice(r0, r0 + rb)
        y = _dot(yd_ref[0, rs, :], wo_ref[0, 0]) + _dot(yr_ref[0, rs, :], wo_ref[0, 1])
        x1 = x_ref[0, rs, :] + mod[2:3] * _rms(y, gpm_ref[...])
        h = (_rms(x1, gff_ref[...]) * (1.0 + mod[4:5]) + mod[3:4]).astype(BF16)
        acc = jnp.zeros_like(x1)
        for c in range(nc):
            gate = _dot(h, wfc_ref[0, :, c * fc:(c + 1) * fc])
            up = _dot(h, wfc_ref[0, :, (nc + c) * fc:(nc + c + 1) * fc])
            acc = acc + _dot((gate * _sigmoid(gate) * up).astype(BF16), wdn_ref[0, c])
        o_ref[0, rs, :] = x1 + mod[5:6] * _rms(acc, gpf_ref[...])


def _out_ffn_call(x, yd, yr, mod, gpm, gff, gpf, wo, wfc, wdn, layer, tm):
    B, S, D = x.shape
    _, nc, fc, _ = wdn.shape
    lay3 = lambda b, j: (layer, 0, 0)
    lay4 = lambda b, j: (layer, 0, 0, 0)
    tok = lambda b, j: (b, j, 0)
    c2 = lambda b, j: (0, 0)
    c3 = lambda b, j: (0, 0, 0)
    return pl.pallas_call(
        _out_ffn_kernel,
        grid=(B, S // tm),
        in_specs=[
            pl.BlockSpec((1, tm, D), tok),
            pl.BlockSpec((1, tm, DSA_WIDTH), tok),
            pl.BlockSpec((1, tm, RWKV_WIDTH), tok),
            pl.BlockSpec((1, 6, D), lambda b, j: (b, 0, 0)),
            pl.BlockSpec((1, D), c2),
            pl.BlockSpec((1, D), c2),
            pl.BlockSpec((1, D), c2),
            pl.BlockSpec((1, 2, DSA_WIDTH, D), lay4, pipeline_mode=pl.Buffered(1)),
            pl.BlockSpec((1, D, 2 * nc * fc), lay3, pipeline_mode=pl.Buffered(1)),
            pl.BlockSpec((1, nc, fc, D), lay4, pipeline_mode=pl.Buffered(1)),
        ],
        out_specs=pl.BlockSpec((1, tm, D), tok),
        out_shape=jax.ShapeDtypeStruct((B, S, D), F32),
        compiler_params=pltpu.CompilerParams(
            dimension_semantics=("arbitrary", "arbitrary"), vmem_limit_bytes=VMEM_LIMIT),
    )(x, yd, yr, mod, gpm, gff, gpf, wo, wfc, wdn)


def _pad_rows(w, rows, at):
    out = jnp.zeros((rows,) + w.shape[1:], w.dtype)
    return out.at[at:at + w.shape[0]].set(w)


def kernel(x, c, ada_w, ada_b, pre_g_mix, post_g_mix, pre_g_ffn, post_g_ffn, w_in, w_in_vres, mu_shift, mu_vres, w_out, q_norm_g, kv_norm_g, w_q_up, w_qi_up, w_k_up, w_v_up, kidx_ln_g, kidx_ln_b, w0, w2, a0, a2, g2, v0, v2, k_k, k_a, r_k, lnx_g, lnx_b, w_fc, w_down):
    B, S, D = x.shape
    L = w_in.shape[0]
    n_dsa = Q_LORA + KV_LORA + IDX_DIM + IDX_HEADS
    tm = min(FFN_TILE, S)
    ff_chunk = FF_CHUNK

    mod_all = _mod_call(c, ada_w, ada_b).reshape(L, B, 6, D)

    row = lambda t: t.reshape(1, -1)
    lane_pad = lambda t: jnp.pad(t, (0, LANES - t.shape[0])).reshape(1, LANES)

    n_rw = w_in.shape[2] - n_dsa
    zpad = lambda n: jnp.zeros((L, D, n), F32)
    mv_w = jnp.concatenate([jnp.zeros((1, D, MV_LORA), F32), w_in_vres], axis=0)
    w_in_all = jnp.concatenate(
        [w_in[:, :, n_dsa:], mv_w, zpad(RW_PAD - n_rw - MV_LORA),
         w_in[:, :, :n_dsa], zpad(P_PAD - COL_KW - IDX_DIM - IDX_HEADS)], axis=2).astype(BF16)
    mv_mu = jnp.concatenate([jnp.zeros((1, MV_LORA), F32), mu_vres], axis=0)
    mu_all = jnp.concatenate(
        [mu_shift, mv_mu, jnp.zeros((L, RW_PAD - n_rw - MV_LORA), F32)], axis=1)
    wo_all = w_out.astype(BF16).reshape(L, 2, DSA_WIDTH, D)
    wfc_all = w_fc.astype(BF16)
    wdn_all = w_down.astype(BF16).reshape(L, D_FF // ff_chunk, ff_chunk, D)

    v_first_src = None
    for l in range(L):
        rw, cq, ckv, kw = _inproj_call(
            x, mod_all[l], row(pre_g_mix[l]), w_in_all, l, mu_all[l:l + 1], row(q_norm_g[l]),
            row(kv_norm_g[l]), lane_pad(kidx_ln_g[l]), lane_pad(kidx_ln_b[l]), min(INPROJ_TILE, S))

        vec = jnp.stack([w0[l], a0[l], v0[l - 1] if l else jnp.zeros_like(w0[l]), k_k[l], k_a[l],
                         r_k[l].reshape(-1), lnx_g[l], lnx_b[l]])
        w2p = _pad_rows(w2[l], LANES, 0)
        a2p = _pad_rows(a2[l], LANES, DECAY_LORA).astype(BF16)
        g2p = _pad_rows(g2[l], 2 * LANES, 0).astype(BF16)
        v2p = _pad_rows(v2[l - 1], 2 * LANES, GATE_LORA).astype(BF16) if l else None
        y_rwkv = _rwkv_call(rw, v_first_src, vec, w2p, a2p, g2p, v2p)
        if l == 0:
            v_first_src = rw

        wqt = w_q_up[l].reshape(Q_LORA, DSA_WIDTH).T.astype(BF16)
        wk = jnp.transpose(w_k_up[l], (1, 0, 2)).astype(BF16)
        wqit = jnp.pad(jnp.transpose(w_qi_up[l], (1, 2, 0)),
                       ((0, 0), (0, LANES - IDX_DIM), (0, 0))).astype(BF16)
        wvt = jnp.transpose(w_v_up[l], (1, 2, 0)).astype(BF16)
        y_dsa = _dsa_call(cq, kw, ckv, wqt, wk, wqit, wvt)

        x = _out_ffn_call(x, y_dsa, y_rwkv, mod_all[l], row(post_g_mix[l]), row(pre_g_ffn[l]),
                          row(post_g_ffn[l]), wo_all, wfc_all, wdn_all, l, tm)
    return x
```

```python
import functools

import jax
import jax.numpy as jnp
from jax import lax
from jax.experimental import pallas as pl
from jax.experimental.pallas import tpu as pltpu

F32 = jnp.float32
BF16 = jnp.bfloat16

D_MODEL = 1024
CHUNK = 64
DSA_HEADS = 8
DSA_HEAD_DIM = 64
DSA_WIDTH = DSA_HEADS * DSA_HEAD_DIM
Q_LORA = 256
KV_LORA = 128
IDX_HEADS = 4
IDX_DIM = 64
TOPK_MAX = 256
RWKV_HEADS = 8
RWKV_HEAD_DIM = 64
RWKV_WIDTH = RWKV_HEADS * RWKV_HEAD_DIM
DECAY_LORA = 64
AAA_LORA = 64
MV_LORA = 32
GATE_LORA = 160
D_FF = ((8 * D_MODEL + 3 * 256 - 1) // (3 * 256)) * 256
NORM_EPS = 1e-6
LNX_EPS = 64e-5

LANES = 128
VMEM_LIMIT = 56 * 1024 * 1024

RW_R, RW_K, RW_V = 0, RWKV_WIDTH, 2 * RWKV_WIDTH
RW_WA = 3 * RWKV_WIDTH
RW_GM = RW_WA + DECAY_LORA + AAA_LORA
RW_PAD = RW_GM + 2 * LANES
COL_CQ = RW_PAD
COL_CKV = COL_CQ + Q_LORA
COL_KW = COL_CKV + KV_LORA
P_PAD = COL_KW + LANES

FFN_ROW_BLOCK = 512
FFN_TILE = 1024
FF_CHUNK = 256
INPROJ_COL_BLOCK = 512
INPROJ_ROW_BLOCK = 512
INPROJ_TILE = 1024
RW_CHUNK = 64
RW_TILE = 256
RW_SEQS = 4
QUAD = 4 * RWKV_HEAD_DIM
N_BISECT = 20
KEY_BLOCK = 256
DSA_Q_TILE = 256
ATT_HEAD_GROUP = 8
PART_ROWS = 32
ONES_ROWS = 16
LOG2_E = 1.4426950408889634
NEG_BIG = -1e30


def _dot(a, b):
    return jnp.dot(a, b, preferred_element_type=F32)


def _dot_nt(a, b):
    return lax.dot_general(a, b, (((1,), (1,)), ((), ())), preferred_element_type=F32)


def _split(a):
    hi = a.astype(BF16)
    lo = (a - hi.astype(F32)).astype(BF16)
    return hi, lo


def _dot3(a, b):
    ah, al = _split(a)
    bh, bl = _split(b)
    return _dot(ah, bh) + _dot(al, bh) + _dot(ah, bl)


def _rms(x, g):
    return x * lax.rsqrt(jnp.mean(x * x, axis=-1, keepdims=True) + NORM_EPS) * g


def _sigmoid(x):
    return 1.0 / (1.0 + jnp.exp(-x))


def _mod_kernel(c_ref, w_ref, b_ref, o_ref):
    c = c_ref[...]
    cond = c * _sigmoid(c)
    o_ref[0] = _dot3(cond, w_ref[0]) + b_ref[0]


def _mod_call(c, ada_w, ada_b):
    L, D, D6 = ada_w.shape
    B = c.shape[0]
    nj = D6 // D
    return pl.pallas_call(
        _mod_kernel,
        grid=(L, nj),
        in_specs=[
            pl.BlockSpec((B, D), lambda l, j: (0, 0)),
            pl.BlockSpec((1, D, D), lambda l, j: (l, 0, j)),
            pl.BlockSpec((1, 1, D), lambda l, j: (l, 0, j)),
        ],
        out_specs=pl.BlockSpec((1, B, D), lambda l, j: (l, 0, j)),
        out_shape=jax.ShapeDtypeStruct((L, B, D6), F32),
        compiler_params=pltpu.CompilerParams(vmem_limit_bytes=VMEM_LIMIT),
    )(c, ada_w, ada_b.reshape(L, 1, D6))


def _inproj_kernel(x_ref, mod_ref, g_ref, w_ref, mu_ref, qg_ref, kvg_ref, lng_ref, lnb_ref,
                   rw_ref, cq_ref, ckv_ref, kw_ref, carry_ref):
    j = pl.program_id(1)
    tm = x_ref.shape[1]
    rb = min(INPROJ_ROW_BLOCK, tm)
    mod = mod_ref[0]

    @pl.when(j == 0)
    def _():
        carry_ref[...] = jnp.zeros_like(carry_ref)

    row = lax.broadcasted_iota(jnp.int32, (rb, 1), 0)
    lane = lax.broadcasted_iota(jnp.int32, (1, LANES), 1)
    is_k = lane < IDX_DIM

    for r0 in range(0, tm, rb):
        rs = slice(r0, r0 + rb)
        h = (_rms(x_ref[0, rs, :], g_ref[...]) * (1.0 + mod[1:2]) + mod[0:1]).astype(BF16)

        def proj(lo_, hi_):
            return _dot(h, w_ref[0, :, lo_:hi_])

        for lo_ in range(0, RW_PAD, INPROJ_COL_BLOCK):
            hi_ = min(lo_ + INPROJ_COL_BLOCK, RW_PAD)
            u = proj(lo_, hi_)
            prev = jnp.where(row == 0, carry_ref[0:1, lo_:hi_], pltpu.roll(u, 1, axis=0))
            carry_ref[0:1, lo_:hi_] = u[rb - 1:rb, :]
            rw_ref[0, rs, lo_:hi_] = u + (prev - u) * mu_ref[:, lo_:hi_]

        cq_ref[0, rs, :] = _rms(proj(COL_CQ, COL_CKV), qg_ref[...]).astype(BF16)
        ckv_ref[0, rs, :] = _rms(proj(COL_CKV, COL_KW), kvg_ref[...]).astype(BF16)

        kw = proj(COL_KW, P_PAD)
        mean = jnp.sum(jnp.where(is_k, kw, 0.0), axis=-1, keepdims=True) * (1.0 / IDX_DIM)
        cen = jnp.where(is_k, kw - mean, 0.0)
        var = jnp.sum(cen * cen, axis=-1, keepdims=True) * (1.0 / IDX_DIM)
        kn = cen * lax.rsqrt(var + NORM_EPS) * lng_ref[...] + lnb_ref[...]
        kw_ref[0, rs, :] = jnp.where(is_k, kn, kw * ((IDX_HEADS ** -0.5) * (IDX_DIM ** -0.5)))


def _inproj_call(x, mod, g, w, layer, mu, qg, kvg, lng, lnb, tm):
    B, S, D = x.shape
    const = lambda b, j: (0, 0)
    tok = lambda b, j: (b, j, 0)
    return pl.pallas_call(
        _inproj_kernel,
        grid=(B, S // tm),
        in_specs=[
            pl.BlockSpec((1, tm, D), tok),
            pl.BlockSpec((1, 6, D), lambda b, j: (b, 0, 0)),
            pl.BlockSpec((1, D), const),
            pl.BlockSpec((1, D, P_PAD), lambda b, j: (layer, 0, 0), pipeline_mode=pl.Buffered(1)),
            pl.BlockSpec((1, RW_PAD), const),
            pl.BlockSpec((1, Q_LORA), const),
            pl.BlockSpec((1, KV_LORA), const),
            pl.BlockSpec((1, LANES), const),
            pl.BlockSpec((1, LANES), const),
        ],
        out_specs=[
            pl.BlockSpec((1, tm, RW_PAD), tok),
            pl.BlockSpec((1, tm, Q_LORA), tok),
            pl.BlockSpec((1, tm, KV_LORA), tok),
            pl.BlockSpec((1, tm, LANES), tok),
        ],
        out_shape=[
            jax.ShapeDtypeStruct((B, S, RW_PAD), F32),
            jax.ShapeDtypeStruct((B, S, Q_LORA), BF16),
            jax.ShapeDtypeStruct((B, S, KV_LORA), BF16),
            jax.ShapeDtypeStruct((B, S, LANES), F32),
        ],
        scratch_shapes=[pltpu.VMEM((8, RW_PAD), F32)],
        compiler_params=pltpu.CompilerParams(
            dimension_semantics=("arbitrary", "arbitrary"), vmem_limit_bytes=VMEM_LIMIT),
    )(x, mod, g, w, mu, qg, kvg, lng, lnb)


def _softplus(z):
    return jnp.maximum(z, 0.0) + jnp.log(1.0 + jnp.exp(-jnp.abs(z)))


def _rwkv_kernel(*refs, has_vres):
    if has_vres:
        rw_ref, vf_ref, vec_ref, w2_ref, a2_ref, g2_ref, v2_ref, y_ref, h_ref = refs
    else:
        rw_ref, vec_ref, w2_ref, a2_ref, g2_ref, y_ref, h_ref = refs
    W = RWKV_WIDTH
    N = RWKV_HEAD_DIM
    C = RW_CHUNK
    nb, tcb = rw_ref.shape[0], rw_ref.shape[1]
    tc = nb * tcb

    @pl.when(pl.program_id(1) == 0)
    def _():
        h_ref[...] = jnp.zeros_like(h_ref)

    vec = vec_ref[...]
    w0, a0, v0, k_k, k_a, r_k, lnx_g, lnx_b = [vec[i:i + 1] for i in range(8)]

    def rows(ref, lo_, hi_):
        return ref[:, :, lo_:hi_].reshape(tc, hi_ - lo_)

    r = rows(rw_ref, RW_R, RW_R + W)
    k = rows(rw_ref, RW_K, RW_K + W)
    v = rows(rw_ref, RW_V, RW_V + W)
    wa = rows(rw_ref, RW_WA, RW_GM)
    gm = rows(rw_ref, RW_GM, RW_PAD)

    bd = (lax.broadcasted_iota(jnp.int32, (QUAD, QUAD), 0) // N
          == lax.broadcasted_iota(jnp.int32, (QUAD, QUAD), 1) // N)
    head_ones = bd.astype(BF16)

    def headsum(t):
        tb = t.astype(BF16)
        return jnp.concatenate(
            [_dot(tb[:, q * QUAD:(q + 1) * QUAD], head_ones) for q in range(W // QUAD)], axis=1)

    w_log = -_softplus(-(w0 + _dot3(jnp.tanh(wa), w2_ref[...]))) - 0.5
    logw = -jnp.exp(w_log)
    a = _sigmoid(a0 + _dot(wa.astype(BF16), a2_ref[...]))
    g = _dot(_sigmoid(gm).astype(BF16), g2_ref[...])
    if has_vres:
        v = v + (rows(vf_ref, 0, W) - v) * _sigmoid(v0 + _dot(gm.astype(BF16), v2_ref[...]))
    kk = k * k_k
    kk = kk * lax.rsqrt(jnp.maximum(headsum(kk * kk), 1e-24))
    k = k * (1.0 + (a - 1.0) * k_a)

    ti = lax.broadcasted_iota(jnp.int32, (tc, tc), 0)
    tj = lax.broadcasted_iota(jnp.int32, (tc, tc), 1)
    tri = ((ti // C == tj // C) & (tj <= ti)).astype(BF16)
    lh = logw.astype(BF16)
    lm = (logw - lh.astype(F32)).astype(BF16)
    cum = _dot(tri, lh) + _dot(tri, lm)

    bt = kk * jnp.exp(cum - logw)
    at = -(a * kk) * jnp.exp(-cum)
    kt = k * jnp.exp(-cum)
    rt = r * jnp.exp(cum)

    lane_q = lax.broadcasted_iota(jnp.int32, (1, QUAD), 1) // N
    hmask = [lane_q == h for h in range(4)]
    gr = lax.broadcasted_iota(jnp.int32, (2 * C, 2 * QUAD), 0)
    gc = lax.broadcasted_iota(jnp.int32, (2 * C, 2 * QUAD), 1) % C
    gmask = ((gr < C) & (gc < gr)) | ((gr >= C) & (gc <= gr - C))
    pr = lax.broadcasted_iota(jnp.int32, (C, LANES), 0)
    pc = lax.broadcasted_iota(jnp.int32, (C, LANES), 1)
    eye_pad = (pr == pc).astype(F32)
    left = pc < C
    zeros_cq = jnp.zeros((C, QUAD), F32)
    n_ch = tc // C
    n_q = W // QUAD

    cells = [(c, q) for c in range(n_ch) for q in range(n_q)]

    cell = {}
    for c, q in cells:
        rs = slice(c * C, (c + 1) * C)
        cs = slice(q * QUAD, (q + 1) * QUAD)
        tot = cum[c * C + C - 1:c * C + C, cs]
        tail = jnp.exp(tot - cum[rs, cs])
        atail = -(a[rs, cs] * kk[rs, cs]) * tail
        ktail = k[rs, cs] * tail
        dp_t = jnp.broadcast_to(jnp.exp(tot), (LANES, QUAD)).T
        vq = v[rs, cs]
        lhs = jnp.concatenate([bt[rs, cs], rt[rs, cs]], axis=0).astype(BF16)
        rhs_t = jnp.concatenate(
            [jnp.where(hmask[h], t, 0.0) for h in range(4) for t in (kt[rs, cs], at[rs, cs])],
            axis=0).astype(BF16)
        gram = jnp.where(gmask, _dot_nt(lhs, rhs_t), 0.0)
        cell[c, q] = dict(
            ak_t=jnp.concatenate([atail, ktail], axis=0).T.astype(BF16),
            dp_col=jnp.concatenate([dp_t, dp_t], axis=1), vq=vq, lhs=lhs, gram=gram,
            gram_b=gram.astype(BF16),
            v_masked=[jnp.where(hmask[h], vq, 0.0) for h in range(4)])

    pairs = [(c, q, j) for c, q in cells for j in range(2)]
    st = {}
    for c, q, j in pairs:
        for h in (2 * j, 2 * j + 1):
            st[c, q, h] = jnp.where(left, eye_pad, cell[c, q]["gram"][0:C, h * LANES:(h + 1) * LANES])
    zeros_pad = jnp.zeros((C, LANES), F32)
    for _ in range(6):
        for c, q, j in pairs:
            s0, s1 = st[c, q, 2 * j], st[c, q, 2 * j + 1]
            lhs2 = jnp.where(left, pltpu.roll(s0, C, axis=1), s1).astype(BF16)
            rhs2 = jnp.concatenate(
                [jnp.concatenate([s0, zeros_pad], axis=1),
                 jnp.concatenate([zeros_pad, s1], axis=1)], axis=0).astype(BF16)
            res = _dot(lhs2, rhs2)
            st[c, q, 2 * j] = jnp.where(left, s0 + res[:, :LANES], res[:, :LANES])
            st[c, q, 2 * j + 1] = jnp.where(left, s1 + res[:, LANES:], res[:, LANES:])

    for c, q in cells:
        o = cell[c, q]
        o["t_cat"] = jnp.concatenate(
            [jnp.where(left, st[c, q, h], 0.0) for h in range(4)], axis=1).astype(BF16)
        vs0 = jnp.concatenate(
            [t for h in range(4) for t in (o["v_masked"][h], zeros_cq)], axis=0).astype(BF16)
        o["akv"] = _dot(o["gram_b"][0:C], vs0)

    def recur(chunks, hq):
        keys = [(b, q) for b in range(nb) for q in range(n_q)]
        z, u, y = {}, {}, {}
        for b, q in keys:
            z[b, q] = _dot(cell[chunks[b], q]["lhs"], hq[b][q].astype(BF16))
        for b, q in keys:
            o = cell[chunks[b], q]
            xs = z[b, q][0:C] + o["akv"]
            xs_st = jnp.concatenate(
                [t for h in range(4) for t in (jnp.where(hmask[h], xs, 0.0), zeros_cq)],
                axis=0).astype(BF16)
            u[b, q] = _dot(o["t_cat"], xs_st)
        for b, q in keys:
            o = cell[chunks[b], q]
            uv = jnp.concatenate([u[b, q], o["vq"]], axis=0).astype(BF16)
            hq[b][q] = o["dp_col"] * hq[b][q] + jnp.where(bd, _dot(o["ak_t"], uv), 0.0)
        for b, q in keys:
            o = cell[chunks[b], q]
            uv_st = jnp.concatenate(
                [t for h in range(4) for t in (o["v_masked"][h], jnp.where(hmask[h], u[b, q], 0.0))],
                axis=0).astype(BF16)
            y[b, q] = z[b, q][C:2 * C] + _dot(o["gram_b"][C:2 * C], uv_st)
        return [jnp.concatenate([y[b, q] for q in range(n_q)], axis=1) for b in range(nb)]

    per_seq = tcb // C
    hq = [[h_ref[b * n_q + q] for q in range(n_q)] for b in range(nb)]
    y_rows = [None] * n_ch
    for s_ in range(per_seq):
        ys = recur([b * per_seq + s_ for b in range(nb)], hq)
        for b in range(nb):
            y_rows[b * per_seq + s_] = ys[b]
    for b in range(nb):
        for q in range(n_q):
            h_ref[b * n_q + q] = hq[b][q]
    y = jnp.concatenate(y_rows, axis=0)

    mu = headsum(y) * (1.0 / N)
    yc = y - mu
    var = headsum(yc * yc) * (1.0 / N)
    yn = yc * lax.rsqrt(var + LNX_EPS) * lnx_g + lnx_b
    bonus = headsum(r * k * r_k) * v
    y_ref[...] = ((yn + bonus) * g).astype(BF16).reshape(nb, tcb, W)


def _rwkv_call(rw, vfirst_src, vec, w2p, a2p, g2p, v2p):
    B, S, _ = rw.shape
    has_vres = vfirst_src is not None
    nb = min(RW_SEQS, B)
    tc = RW_TILE // nb
    const = lambda b, j: (0, 0)
    in_specs = [pl.BlockSpec((nb, tc, RW_PAD), lambda b, j: (b, j, 0))]
    args = [rw]
    if has_vres:
        in_specs.append(pl.BlockSpec((nb, tc, RWKV_WIDTH), lambda b, j: (b, j, RW_V // RWKV_WIDTH)))
        args.append(vfirst_src)
    in_specs += [pl.BlockSpec((8, RWKV_WIDTH), const),
                 pl.BlockSpec((LANES, RWKV_WIDTH), const),
                 pl.BlockSpec((LANES, RWKV_WIDTH), const),
                 pl.BlockSpec((2 * LANES, RWKV_WIDTH), const)]
    args += [vec, w2p, a2p, g2p]
    if has_vres:
        in_specs.append(pl.BlockSpec((2 * LANES, RWKV_WIDTH), const))
        args.append(v2p)
    return pl.pallas_call(
        functools.partial(_rwkv_kernel, has_vres=has_vres),
        grid=(B // nb, S // tc),
        in_specs=in_specs,
        out_specs=pl.BlockSpec((nb, tc, RWKV_WIDTH), lambda b, j: (b, j, 0)),
        out_shape=jax.ShapeDtypeStruct((B, S, RWKV_WIDTH), BF16),
        scratch_shapes=[pltpu.VMEM((nb * (RWKV_WIDTH // QUAD), QUAD, QUAD), F32)],
        compiler_params=pltpu.CompilerParams(
            dimension_semantics=("arbitrary", "arbitrary"), vmem_limit_bytes=VMEM_LIMIT),
    )(*args)


def _part(t, op):
    r, n = t.shape
    return op(t.reshape(r // PART_ROWS, PART_ROWS, n), axis=0)


def _sweep(n, body, init, pair=None):
    if pair is None:
        def pair(j, c):
            return body(2 * j + 1, body(2 * j, c))
    c = lax.fori_loop(0, n // 2, pair, init)
    return lax.cond(n % 2 == 1, lambda c: body(n - 1, c), lambda c: c, c)


def _dsa_kernel(cq_ref, kwq_ref, kw_ref, ckv_ref, wqt_ref, wk_ref, wqit_ref, wvt_ref, y_ref,
                kidx_s, ckvt_s, sc_s, bias_s, att_s, yt_s, *, top_k):
    i = pl.program_id(1)
    nblk = kidx_s.shape[0]
    QB = DSA_Q_TILE
    KB = KEY_BLOCK
    HG = ATT_HEAD_GROUP

    @pl.when(i == 0)
    def _():
        for b in range(nblk):
            kidx_s[b] = kw_ref[0, b * KB:(b + 1) * KB, :].astype(BF16)
            ckvt_s[b, :KV_LORA, :] = ckv_ref[0, b * KB:(b + 1) * KB, :].astype(F32).T.astype(BF16)
            ckvt_s[b, KV_LORA:, :] = jnp.ones((ONES_ROWS, KB), BF16)

    cq_t = cq_ref[0].astype(F32).T.astype(BF16)
    q_t = _dot(wqt_ref[...], cq_t).astype(BF16)
    w_t = kwq_ref[0].T
    qi_all = jnp.concatenate([_dot(wqit_ref[h], cq_t) for h in range(IDX_HEADS)],
                             axis=1).astype(BF16)
    w_all = jnp.concatenate([w_t[IDX_DIM + h:IDX_DIM + h + 1, :] for h in range(IDX_HEADS)],
                            axis=1)
    scale = DSA_HEAD_DIM ** -0.5 * LOG2_E
    dh = DSA_HEAD_DIM
    qa_groups = [
        jnp.concatenate(
            [_dot(wk_ref[g * HG + h], q_t[(g * HG + h) * dh:(g * HG + h + 1) * dh]) * scale
             for h in range(HG)], axis=1).astype(BF16)
        for g in range(DSA_HEADS // HG)]

    nkb = (i * QB + QB + KB - 1) // KB
    q_chunk = (i * QB + lax.broadcasted_iota(jnp.int32, (1, QB), 1)) // CHUNK
    k_row = lax.broadcasted_iota(jnp.int32, (KB, 1), 0)
    k_eff = jnp.minimum(top_k, (q_chunk + 1) * CHUNK).astype(F32)
    inf = jnp.float32(jnp.inf)
    one = jnp.float32(1.0)
    zero = jnp.float32(0.0)

    def fin_min(t):
        return jnp.min(t, axis=0, keepdims=True)

    def fin_max(t):
        return jnp.max(t, axis=0, keepdims=True)

    def fin_sum(t):
        return jnp.sum(t, axis=0, keepdims=True)

    def score_body(kb, carry, masked):
        lo, hi = carry
        lg = jnp.maximum(_dot(kidx_s[kb], qi_all), zero) * w_all
        acc = jnp.zeros((KB, QB), F32)
        for h in range(IDX_HEADS):
            acc = acc + lg[:, h * QB:(h + 1) * QB]
        if masked:
            adm = (kb * KB + k_row) // CHUNK <= q_chunk
            sc_s[kb] = jnp.where(adm, acc, -inf)
            lo = jnp.minimum(lo, _part(jnp.where(adm, acc, inf), jnp.min))
            hi = jnp.maximum(hi, _part(jnp.where(adm, acc, -inf), jnp.max))
        else:
            sc_s[kb] = acc
            lo = jnp.minimum(lo, _part(acc, jnp.min))
            hi = jnp.maximum(hi, _part(acc, jnp.max))
        return lo, hi

    n_open = (i * QB + CHUNK) // KB
    carry = _sweep(n_open, functools.partial(score_body, masked=False),
                   (jnp.full((PART_ROWS, QB), inf, F32), jnp.full((PART_ROWS, QB), -inf, F32)))
    lo, hi = lax.fori_loop(n_open, nkb, functools.partial(score_body, masked=True), carry)
    lo = fin_min(lo)
    hi = fin_max(hi)

    def count_ge(t):
        def body(kb, acc):
            return acc + _part(jnp.where(sc_s[kb] >= t, one, zero), jnp.sum)
        return fin_sum(_sweep(nkb,body, jnp.zeros((PART_ROWS, QB), F32)))

    def bisect(_, carry):
        lo, hi = carry
        mid = 0.5 * lo + 0.5 * hi
        ge = count_ge(mid) >= k_eff
        return jnp.where(ge, mid, lo), jnp.where(ge, hi, mid)

    n_bisect = jnp.where((i + 1) * QB <= top_k, 0, N_BISECT)
    lo, hi = lax.fori_loop(0, n_bisect, bisect, (lo, hi))

    def min_ge(t):
        def body(kb, acc):
            s = sc_s[kb]
            return jnp.minimum(acc, _part(jnp.where(s >= t, s, inf), jnp.min))
        return fin_min(_sweep(nkb, body, jnp.full((PART_ROWS, QB), inf, F32)))

    def walk_cond(st):
        return st[2] > 0

    def walk_body(st):
        m, _, _ = st

        def body(kb, carry):
            cnt, nxt = carry
            s = sc_s[kb]
            gt = s > m
            return (cnt + _part(jnp.where(gt, one, zero), jnp.sum),
                    jnp.minimum(nxt, _part(jnp.where(gt, s, inf), jnp.min)))

        cnt, nxt = _sweep(nkb,body,
                                 (jnp.zeros((PART_ROWS, QB), F32), jnp.full((PART_ROWS, QB), inf, F32)))
        n_gt = fin_sum(cnt)
        done = n_gt < k_eff
        pending = jnp.max(jnp.where(done, 0, 1))
        return jnp.where(done, m, fin_min(nxt)), n_gt, pending

    thr, n_gt, _ = lax.while_loop(
        walk_cond, walk_body, (min_ge(lo), jnp.zeros((1, QB), F32), jnp.int32(1)))
    need = k_eff - n_gt

    tri = (lax.broadcasted_iota(jnp.int32, (KB, KB), 1)
           <= lax.broadcasted_iota(jnp.int32, (KB, KB), 0)).astype(BF16)

    def mask_body(kb, carry):
        s = sc_s[kb]
        tie = jnp.where(s == thr, one, zero)
        pre = _dot(tri, tie.astype(BF16)) + carry
        sel = (s > thr) | ((s == thr) & (pre <= need))
        bias_s[kb] = jnp.where(sel, zero, jnp.float32(NEG_BIG))
        return carry + fin_sum(_part(tie, jnp.sum))

    _sweep(nkb, mask_body, jnp.zeros((1, QB), F32))

    for g in range(DSA_HEADS // HG):
        qa_all = qa_groups[g]

        def att_body(kb, mx):
            kv = ckv_ref[0, pl.ds(pl.multiple_of(kb * KB, KB), KB), :]
            lg = _dot(kv, qa_all)
            b = bias_s[kb]
            parts = []
            for h in range(HG):
                att = lg[:, h * QB:(h + 1) * QB] + b
                att_s[kb, :, h * QB:(h + 1) * QB] = att
                parts.append(_part(att, jnp.max))
            return jnp.maximum(mx, jnp.concatenate(parts, axis=1))

        mx = fin_max(_sweep(nkb, att_body, jnp.full((PART_ROWS, HG * QB), NEG_BIG, F32)))

        def prob(kb):
            return jnp.exp2(att_s[kb] - mx).astype(BF16)

        def pv_body(kb, o):
            return o + _dot(ckvt_s[kb], prob(kb))

        def pv_pair(j, o):
            return o + _dot(jnp.concatenate([ckvt_s[2 * j], ckvt_s[2 * j + 1]], axis=1),
                            jnp.concatenate([prob(2 * j), prob(2 * j + 1)], axis=0))

        o = _sweep(nkb, pv_body, jnp.zeros((KV_LORA + ONES_ROWS, HG * QB), F32), pv_pair)
        o = (o[:KV_LORA] / o[KV_LORA:KV_LORA + 1]).astype(BF16)
        for h in range(HG):
            hh = g * HG + h
            yt_s[hh * dh:(hh + 1) * dh, :] = _dot(wvt_ref[hh], o[:, h * QB:(h + 1) * QB])

    y_ref[0] = yt_s[...].T.astype(BF16)


def _dsa_call(cq, kw, ckv, wqt, wk, wqit, wvt):
    B, S, _ = cq.shape
    QB = DSA_Q_TILE
    KB = KEY_BLOCK
    nblk = S // KB
    top_k = min(TOPK_MAX, S // 4)
    c2 = lambda b, i: (0, 0)
    c3 = lambda b, i: (0, 0, 0)
    return pl.pallas_call(
        functools.partial(_dsa_kernel, top_k=top_k),
        grid=(B, S // QB),
        in_specs=[
            pl.BlockSpec((1, QB, Q_LORA), lambda b, i: (b, i, 0)),
            pl.BlockSpec((1, QB, LANES), lambda b, i: (b, i, 0)),
            pl.BlockSpec((1, S, LANES), lambda b, i: (b, 0, 0)),
            pl.BlockSpec((1, S, KV_LORA), lambda b, i: (b, 0, 0)),
            pl.BlockSpec((DSA_WIDTH, Q_LORA), c2),
            pl.BlockSpec((DSA_HEADS, KV_LORA, DSA_HEAD_DIM), c3),
            pl.BlockSpec((IDX_HEADS, LANES, Q_LORA), c3),
            pl.BlockSpec((DSA_HEADS, DSA_HEAD_DIM, KV_LORA), c3),
        ],
        out_specs=pl.BlockSpec((1, QB, DSA_WIDTH), lambda b, i: (b, i, 0)),
        out_shape=jax.ShapeDtypeStruct((B, S, DSA_WIDTH), BF16),
        scratch_shapes=[
            pltpu.VMEM((nblk, KB, LANES), BF16),
            pltpu.VMEM((nblk, KV_LORA + ONES_ROWS, KB), BF16),
            pltpu.VMEM((nblk, KB, QB), F32),
            pltpu.VMEM((nblk, KB, QB), F32),
            pltpu.VMEM((nblk, KB, ATT_HEAD_GROUP * QB), F32),
            pltpu.VMEM((DSA_WIDTH, QB), F32),
        ],
        compiler_params=pltpu.CompilerParams(
            dimension_semantics=("arbitrary", "arbitrary"), vmem_limit_bytes=VMEM_LIMIT),
    )(cq, kw, kw, ckv, wqt, wk, wqit, wvt)


def _out_ffn_kernel(x_ref, yd_ref, yr_ref, mod_ref, gpm_ref, gff_ref, gpf_ref,
                    wo_ref, wfc_ref, wdn_ref, o_ref):
    mod = mod_ref[0]
    tm = x_ref.shape[1]
    rb = min(FFN_ROW_BLOCK, tm)
    _, nc, fc, _ = wdn_ref.shape
    for r0 in range(0, tm, rb):
        rs = slice(r0, r0 + rb)
        y = _dot(yd_ref[0, rs, :], wo_ref[0, 0]) + _dot(yr_ref[0, rs, :], wo_ref[0, 1])
        x1 = x_ref[0, rs, :] + mod[2:3] * _rms(y, gpm_ref[...])
        h = (_rms(x1, gff_ref[...]) * (1.0 + mod[4:5]) + mod[3:4]).astype(BF16)
        acc = jnp.zeros_like(x1)
        for c in range(nc):
            gate = _dot(h, wfc_ref[0, :, c * fc:(c + 1) * fc])
            up = _dot(h, wfc_ref[0, :, (nc + c) * fc:(nc + c + 1) * fc])
            acc = acc + _dot((gate * _sigmoid(gate) * up).astype(BF16), wdn_ref[0, c])
        o_ref[0, rs, :] = x1 + mod[5:6] * _rms(acc, gpf_ref[...])


def _out_ffn_call(x, yd, yr, mod, gpm, gff, gpf, wo, wfc, wdn, layer, tm):
    B, S, D = x.shape
    _, nc, fc, _ = wdn.shape
    lay3 = lambda b, j: (layer, 0, 0)
    lay4 = lambda b, j: (layer, 0, 0, 0)
    tok = lambda b, j: (b, j, 0)
    c2 = lambda b, j: (0, 0)
    c3 = lambda b, j: (0, 0, 0)
    return pl.pallas_call(
        _out_ffn_kernel,
        grid=(B, S // tm),
        in_specs=[
            pl.BlockSpec((1, tm, D), tok),
            pl.BlockSpec((1, tm, DSA_WIDTH), tok),
            pl.BlockSpec((1, tm, RWKV_WIDTH), tok),
            pl.BlockSpec((1, 6, D), lambda b, j: (b, 0, 0)),
            pl.BlockSpec((1, D), c2),
            pl.BlockSpec((1, D), c2),
            pl.BlockSpec((1, D), c2),
            pl.BlockSpec((1, 2, DSA_WIDTH, D), lay4, pipeline_mode=pl.Buffered(1)),
            pl.BlockSpec((1, D, 2 * nc * fc), lay3, pipeline_mode=pl.Buffered(1)),
            pl.BlockSpec((1, nc, fc, D), lay4, pipeline_mode=pl.Buffered(1)),
        ],
        out_specs=pl.BlockSpec((1, tm, D), tok),
        out_shape=jax.ShapeDtypeStruct((B, S, D), F32),
        compiler_params=pltpu.CompilerParams(
            dimension_semantics=("arbitrary", "arbitrary"), vmem_limit_bytes=VMEM_LIMIT),
    )(x, yd, yr, mod, gpm, gff, gpf, wo, wfc, wdn)


def _pad_rows(w, rows, at):
    out = jnp.zeros((rows,) + w.shape[1:], w.dtype)
    return out.at[at:at + w.shape[0]].set(w)


def kernel(x, c, ada_w, ada_b, pre_g_mix, post_g_mix, pre_g_ffn, post_g_ffn, w_in, w_in_vres, mu_shift, mu_vres, w_out, q_norm_g, kv_norm_g, w_q_up, w_qi_up, w_k_up, w_v_up, kidx_ln_g, kidx_ln_b, w0, w2, a0, a2, g2, v0, v2, k_k, k_a, r_k, lnx_g, lnx_b, w_fc, w_down):
    B, S, D = x.shape
    L = w_in.shape[0]
    n_dsa = Q_LORA + KV_LORA + IDX_DIM + IDX_HEADS
    tm = min(FFN_TILE, S)
    ff_chunk = FF_CHUNK

    mod_all = _mod_call(c, ada_w, ada_b).reshape(L, B, 6, D)

    row = lambda t: t.reshape(1, -1)
    lane_pad = lambda t: jnp.pad(t, (0, LANES - t.shape[0])).reshape(1, LANES)

    n_rw = w_in.shape[2] - n_dsa
    zpad = lambda n: jnp.zeros((L, D, n), F32)
    mv_w = jnp.concatenate([jnp.zeros((1, D, MV_LORA), F32), w_in_vres], axis=0)
    w_in_all = jnp.concatenate(
        [w_in[:, :, n_dsa:], mv_w, zpad(RW_PAD - n_rw - MV_LORA),
         w_in[:, :, :n_dsa], zpad(P_PAD - COL_KW - IDX_DIM - IDX_HEADS)], axis=2).astype(BF16)
    mv_mu = jnp.concatenate([jnp.zeros((1, MV_LORA), F32), mu_vres], axis=0)
    mu_all = jnp.concatenate(
        [mu_shift, mv_mu, jnp.zeros((L, RW_PAD - n_rw - MV_LORA), F32)], axis=1)
    wo_all = w_out.astype(BF16).reshape(L, 2, DSA_WIDTH, D)
    wfc_all = w_fc.astype(BF16)
    wdn_all = w_down.astype(BF16).reshape(L, D_FF // ff_chunk, ff_chunk, D)

    v_first_src = None
    for l in range(L):
        rw, cq, ckv, kw = _inproj_call(
            x, mod_all[l], row(pre_g_mix[l]), w_in_all, l, mu_all[l:l + 1], row(q_norm_g[l]),
            row(kv_norm_g[l]), lane_pad(kidx_ln_g[l]), lane_pad(kidx_ln_b[l]), min(INPROJ_TILE, S))

        vec = jnp.stack([w0[l], a0[l], v0[l - 1] if l else jnp.zeros_like(w0[l]), k_k[l], k_a[l],
                         r_k[l].reshape(-1), lnx_g[l], lnx_b[l]])
        w2p = _pad_rows(w2[l], LANES, 0)
        a2p = _pad_rows(a2[l], LANES, DECAY_LORA).astype(BF16)
        g2p = _pad_rows(g2[l], 2 * LANES, 0).astype(BF16)
        v2p = _pad_rows(v2[l - 1], 2 * LANES, GATE_LORA).astype(BF16) if l else None
        y_rwkv = _rwkv_call(rw, v_first_src, vec, w2p, a2p, g2p, v2p)
        if l == 0:
            v_first_src = rw

        wqt = w_q_up[l].reshape(Q_LORA, DSA_WIDTH).T.astype(BF16)
        wk = jnp.transpose(w_k_up[l], (1, 0, 2)).astype(BF16)
        wqit = jnp.pad(jnp.transpose(w_qi_up[l], (1, 2, 0)),
                       ((0, 0), (0, LANES - IDX_DIM), (0, 0))).astype(BF16)
        wvt = jnp.transpose(w_v_up[l], (1, 2, 0)).astype(BF16)
        y_dsa = _dsa_call(cq, kw, ckv, wqt, wk, wqit, wvt)

        x = _out_ffn_call(x, y_dsa, y_rwkv, mod_all[l], row(post_g_mix[l]), row(pre_g_ffn[l]),
                          row(post_g_ffn[l]), wo_all, wfc_all, wdn_all, l, tm)
    return x
```
